```python
import math, functools
import jax, jax.numpy as jnp
from jax import lax
import numpy as np

D_MODEL = 1024
BATCH = 8
SEQ = 4096
DEPTH = 1
DEC_BATCH = 32
DEC_SEQ = 8
PAST_LEN = 16384
PAGE_SIZE = 128

SSM_EXPAND = 2
D_INNER = SSM_EXPAND * D_MODEL
SSM_HEAD_DIM = 64
SSM_HEADS = D_INNER // SSM_HEAD_DIM
SSM_GROUPS = 4
SSM_HPG = SSM_HEADS // SSM_GROUPS
SSM_STATE = 128
CONV_WIDTH = 4
CONV_DIM = D_INNER + 2 * SSM_GROUPS * SSM_STATE
SSD_CHUNK = 128
ATTN_HEADS = 16
ATTN_HEAD_DIM = 64
ATTN_WIDTH = ATTN_HEADS * ATTN_HEAD_DIM
KV_HEADS = 2
HEADS_PER_KV = ATTN_HEADS // KV_HEADS
KV_WIDTH = 2 * KV_HEADS * ATTN_HEAD_DIM
CMP_BLOCK = 64
SEL_BLOCK = CMP_BLOCK
SEL_TOPK = 16
SEL_BONUS = 1000.0
WINDOW = 512
WIN_QBLOCK = 128
SEL_QBLOCK = 32
ATTN_SCALE = ATTN_HEAD_DIM ** -0.5
PEER_HEADS = 8
PEER_KEYS = 128
N_EXPERTS = PEER_KEYS * PEER_KEYS
PEER_QDIM = 256
PEER_HALF = PEER_QDIM // 2
PEER_TOPK = 16
PEER_TOKEN_CHUNK = 128
EPS = 1e-6
PROJ_SIZES = (D_INNER, CONV_DIM, SSM_HEADS, ATTN_WIDTH, KV_WIDTH, KV_WIDTH, KV_WIDTH, 3 * ATTN_HEADS, 2 * D_MODEL)
PROJ_DIM = sum(PROJ_SIZES)

kernel_name = 'mamba2_nsa_peer_hybrid_step'


def _rmsnorm(x, w):
    xf = x.astype(jnp.float32)
    y = xf * lax.rsqrt(jnp.mean(xf * xf, axis=-1, keepdims=True) + EPS)
    return (y * w.astype(jnp.float32)).astype(x.dtype)


def _masked_softmax(s, mask):
    s = jnp.where(mask, s.astype(jnp.float32), -jnp.inf)
    m = jnp.max(s, axis=-1, keepdims=True)
    m = jnp.where(jnp.isfinite(m), m, 0.0)
    p = jnp.exp(s - m)
    return p / jnp.maximum(jnp.sum(p, axis=-1, keepdims=True), 1e-30)


def _split_proj(p):
    idx = np.cumsum(PROJ_SIZES)[:-1].tolist()
    return jnp.split(p, idx, axis=-1)


def _to_blocks(a, blk):
    b, T = a.shape[:2]
    return jnp.moveaxis(a.reshape(b, T // blk, blk, *a.shape[2:]), 1, 0)


def _from_blocks(a):
    nb, b, blk = a.shape[:3]
    return jnp.moveaxis(a, 0, 1).reshape(b, nb * blk, *a.shape[3:])


def _causal_conv(xbc, conv_prev, conv_w, conv_b):
    xp = jnp.concatenate([conv_prev.astype(xbc.dtype), xbc], axis=1)
    y = lax.conv_general_dilated(xp, conv_w.astype(xbc.dtype)[:, None, :], window_strides=(1,), padding='VALID',
                                 dimension_numbers=('NWC', 'WIO', 'NWC'), feature_group_count=CONV_DIM)
    return jax.nn.silu(y + conv_b.astype(xbc.dtype)), xp[:, -(CONV_WIDTH - 1):]


def _ssd(x, dt, A, Bm, Cm, h0, chunk):
    f32 = jnp.float32
    b, L = x.shape[:2]
    nc = L // chunk
    xc = x.astype(f32).reshape(b, nc, chunk, SSM_GROUPS, SSM_HPG, SSM_HEAD_DIM)
    dtc = dt.reshape(b, nc, chunk, SSM_GROUPS, SSM_HPG)
    Bc = Bm.astype(f32).reshape(b, nc, chunk, SSM_GROUPS, SSM_STATE)
    Cc = Cm.astype(f32).reshape(b, nc, chunk, SSM_GROUPS, SSM_STATE)
    a_cum = jnp.cumsum(dtc * A.reshape(SSM_GROUPS, SSM_HPG), axis=2)
    tril = jnp.tril(jnp.ones((chunk, chunk), bool))[:, :, None, None]
    diff = a_cum[:, :, :, None] - a_cum[:, :, None]
    decay = jnp.exp(jnp.where(tril, diff, -jnp.inf))
    xdt = xc * dtc[..., None]
    cb = jnp.einsum('bclgn,bcsgn->bclsg', Cc, Bc)
    y_diag = jnp.einsum('bclsgj,bcsgjp->bclgjp', cb[..., None] * decay, xdt)
    decay_to_end = jnp.exp(a_cum[:, :, -1:] - a_cum)
    chunk_states = jnp.einsum('bclgn,bclgjp->bcgjpn', Bc, xdt * decay_to_end[..., None])
    chunk_decay = jnp.exp(a_cum[:, :, -1])

    def step(h, inp):
        s_c, d_c = inp
        return h * d_c[..., None, None] + s_c, h

    h_init = h0.astype(f32).reshape(b, SSM_GROUPS, SSM_HPG, SSM_HEAD_DIM, SSM_STATE)
    h_last, h_prev = lax.scan(step, h_init, (jnp.moveaxis(chunk_states, 1, 0), jnp.moveaxis(chunk_decay, 1, 0)))
    h_prev = jnp.moveaxis(h_prev, 0, 1)
    y_off = jnp.einsum('bclgn,bcgjpn->bclgjp', Cc, h_prev) * jnp.exp(a_cum)[..., None]
    y = (y_diag + y_off).reshape(b, L, SSM_HEADS, SSM_HEAD_DIM)
    return y, h_last.reshape(b, SSM_HEADS, SSM_HEAD_DIM, SSM_STATE)


def _mamba_branch(z, xbc, dt_raw, conv_prev, ssm_prev, conv_w, conv_b, dt_bias, a_log, d_skip, ssm_norm_w):
    f32 = jnp.float32
    b, L, _ = xbc.shape
    xbc, conv_new = _causal_conv(xbc, conv_prev, conv_w, conv_b)
    gn = SSM_GROUPS * SSM_STATE
    xs = xbc[..., :D_INNER].reshape(b, L, SSM_HEADS, SSM_HEAD_DIM)
    Bm = xbc[..., D_INNER:D_INNER + gn].reshape(b, L, SSM_GROUPS, SSM_STATE)
    Cm = xbc[..., D_INNER + gn:].reshape(b, L, SSM_GROUPS, SSM_STATE)
    dt = jax.nn.softplus(dt_raw.astype(f32) + dt_bias.astype(f32))
    A = -jnp.exp(a_log.astype(f32))
    y, ssm_new = _ssd(xs, dt, A, Bm, Cm, ssm_prev, math.gcd(L, SSD_CHUNK))
    y = y + d_skip.astype(f32)[:, None] * xs.astype(f32)
    y = y.reshape(b, L, D_INNER) * jax.nn.silu(z.astype(f32))
    y = _rmsnorm(y, ssm_norm_w).astype(z.dtype)
    return y, conv_new, ssm_new.astype(ssm_prev.dtype)


def _cmp_attend(q, cmp_kv, q_pos):
    n = cmp_kv.shape[1]
    blk_end = jnp.arange(n) * CMP_BLOCK + (CMP_BLOCK - 1)
    s = jnp.einsum('btgjd,bngd->btgjn', q, cmp_kv[..., 0, :]).astype(jnp.float32) * ATTN_SCALE
    mask = (blk_end[None, :] <= q_pos[:, None])[None, :, None, None, :]
    p = _masked_softmax(s, mask)
    o = jnp.einsum('btgjn,bngd->btgjd', p.astype(cmp_kv.dtype), cmp_kv[..., 1, :])
    return o, jnp.sum(p, axis=3)


def _select_blocks(imp, q_pos, n_sel):
    imp = jnp.pad(imp, ((0, 0), (0, 0), (0, 0), (0, n_sel - imp.shape[-1])))
    j = jnp.arange(n_sel)[None, :]
    cur = (q_pos // SEL_BLOCK)[:, None]
    forced = ((j == 0) | (j == cur) | (j == cur - 1)).astype(jnp.float32)
    score = jnp.where((j <= cur)[None, :, None, :], imp + SEL_BONUS * forced[None, :, None, :], -jnp.inf)
    _, idx = lax.top_k(score, min(SEL_TOPK, n_sel))
    return idx


def _block_positions(idx):
    pos = idx[..., None] * SEL_BLOCK + jnp.arange(SEL_BLOCK)
    return pos.reshape(*idx.shape[:-1], -1)


def _sel_attend(q, rows, pos, q_pos):
    s = jnp.einsum('btgjd,btgsd->btgjs', q, rows[..., 0, :]).astype(jnp.float32) * ATTN_SCALE
    mask = (pos <= q_pos[None, :, None, None])[:, :, :, None, :]
    p = _masked_softmax(s, mask)
    return jnp.einsum('btgjs,btgsd->btgjd', p.astype(rows.dtype), rows[..., 1, :])


def _band_attend(q, kv, q_pos, k_pos):
    s = jnp.einsum('btgjd,bsgd->btgjs', q, kv[..., 0, :]).astype(jnp.float32) * ATTN_SCALE
    rel = q_pos[:, None] - k_pos[None, :]
    mask = ((rel >= 0) & (rel <= WINDOW) & (k_pos[None, :] >= 0))[None, :, None, None, :]
    p = _masked_softmax(s, mask)
    return jnp.einsum('btgjs,bsgd->btgjd', p.astype(kv.dtype), kv[..., 1, :])


def _combine(gates, o_cmp, o_sel, o_win):
    o = gates[..., 0:1] * o_cmp.astype(jnp.float32) + gates[..., 1:2] * o_sel.astype(jnp.float32) \
        + gates[..., 2:3] * o_win.astype(jnp.float32)
    return o.astype(o_cmp.dtype)


def _nsa_prompt(q, kv_cmp, kv_sel, kv_win, gates):
    b, T = q.shape[:2]
    q_pos = jnp.arange(T)
    n_blk = T // CMP_BLOCK
    cmp_kv = kv_cmp.reshape(b, n_blk, CMP_BLOCK, KV_HEADS, 2, ATTN_HEAD_DIM).mean(axis=2)
    o_cmp, imp = _cmp_attend(q, cmp_kv, q_pos)
    idx = _select_blocks(imp, q_pos, n_blk)
    bidx = jnp.arange(b)[:, None, None, None]
    gidx = jnp.arange(KV_HEADS)[None, None, :, None]

    def sel_block(args):
        qb, ib, pb = args
        pos = _block_positions(ib)
        return _sel_attend(qb, kv_sel[bidx, pos, gidx], pos, pb)

    o_sel = _from_blocks(lax.map(sel_block, (_to_blocks(q, SEL_QBLOCK), _to_blocks(idx, SEL_QBLOCK),
                                             q_pos.reshape(-1, SEL_QBLOCK))))
    kv_pad = jnp.pad(kv_win, ((0, 0), (WINDOW, 0), (0, 0), (0, 0), (0, 0)))
    span = WINDOW + WIN_QBLOCK

    def win_block(args):
        qb, start = args
        kb = lax.dynamic_slice_in_dim(kv_pad, start, span, axis=1)
        return _band_attend(qb, kb, start + jnp.arange(WIN_QBLOCK), start - WINDOW + jnp.arange(span))

    starts = jnp.arange(T // WIN_QBLOCK) * WIN_QBLOCK
    o_win = _from_blocks(lax.map(win_block, (_to_blocks(q, WIN_QBLOCK), starts)))
    return _combine(gates, o_cmp, o_sel, o_win), kv_win[:, -min(WINDOW, T):]


def _gather_paged_rows(pool, page_table, new_rows, pos):
    past = page_table.shape[1] * PAGE_SIZE
    b = page_table.shape[0]
    bidx = jnp.arange(b)[:, None, None, None]
    gidx = jnp.arange(KV_HEADS)[None, None, :, None]
    past_pos = jnp.minimum(pos, past - 1)
    phys = page_table[bidx, past_pos // PAGE_SIZE]
    from_pool = pool[phys, past_pos % PAGE_SIZE, gidx]
    new_pos = jnp.clip(pos - past, 0, new_rows.shape[1] - 1)
    from_new = new_rows[bidx, new_pos, gidx]
    return jnp.where((pos < past)[..., None, None], from_pool.astype(new_rows.dtype), from_new)


def _nsa_sample(pool_cmp, pool_sel, win_buf, page_table, q, kv_cmp, kv_sel, kv_win, gates):
    b, T = q.shape[:2]
    n_pages = page_table.shape[1]
    past = n_pages * PAGE_SIZE
    q_pos = past + jnp.arange(T)
    past_cmp = pool_cmp[page_table].reshape(b, past // CMP_BLOCK, CMP_BLOCK, KV_HEADS, 2, ATTN_HEAD_DIM).mean(axis=2)
    n_new = T // CMP_BLOCK
    new_cmp = kv_cmp[:, :n_new * CMP_BLOCK].reshape(b, n_new, CMP_BLOCK, KV_HEADS, 2, ATTN_HEAD_DIM).mean(axis=2)
    cmp_kv = jnp.concatenate([past_cmp.astype(kv_cmp.dtype), new_cmp], axis=1)
    o_cmp, imp = _cmp_attend(q, cmp_kv, q_pos)
    n_sel = -(-(past + T) // SEL_BLOCK)
    idx = _select_blocks(imp, q_pos, n_sel)
    pos = _block_positions(idx)
    o_sel = _sel_attend(q, _gather_paged_rows(pool_sel, page_table, kv_sel, pos), pos, q_pos)
    w_eff = win_buf.shape[1]
    keys = jnp.concatenate([win_buf.astype(kv_win.dtype), kv_win], axis=1)
    o_win = _band_attend(q, keys, q_pos, past - w_eff + jnp.arange(w_eff + T))
    return _combine(gates, o_cmp, o_sel, o_win), keys[:, -w_eff:]


def _peer(h, peer_wq, peer_sub_keys, peer_u, peer_v):
    n = h.shape[0]
    q = (h @ peer_wq).reshape(n, PEER_HEADS, 2, PEER_HALF)
    s = jnp.einsum('nhcd,hckd->nhck', q, peer_sub_keys).astype(jnp.float32)
    top_s, top_i = lax.top_k(s, PEER_TOPK)
    cand_s = (top_s[:, :, 0, :, None] + top_s[:, :, 1, None, :]).reshape(n, PEER_HEADS, -1)
    cand_i = (top_i[:, :, 0, :, None] * PEER_KEYS + top_i[:, :, 1, None, :]).reshape(n, PEER_HEADS, -1)
    best_s, best_pos = lax.top_k(cand_s, PEER_TOPK)
    experts = jnp.take_along_axis(cand_i, best_pos, axis=-1)
    g = jax.nn.softmax(best_s, axis=-1)
    act = jax.nn.gelu(jnp.einsum('nd,nhkd->nhk', h, peer_u[experts]).astype(jnp.float32))
    return jnp.einsum('nhk,nhkd->nd', (g * act).astype(peer_v.dtype), peer_v[experts])


def _peer_tokens(h, peer_wq, peer_sub_keys, peer_u, peer_v):
    shape = h.shape
    hf = h.reshape(-1, D_MODEL)
    n = hf.shape[0]
    if n % PEER_TOKEN_CHUNK == 0 and n > PEER_TOKEN_CHUNK:
        out = lax.map(lambda c: _peer(c, peer_wq, peer_sub_keys, peer_u, peer_v),
                      hf.reshape(-1, PEER_TOKEN_CHUNK, D_MODEL)).reshape(n, D_MODEL)
    else:
        out = _peer(hf, peer_wq, peer_sub_keys, peer_u, peer_v)
    return out.reshape(shape)


def _block(x, conv_prev, ssm_prev, nsa_core, norm_mix_w, w_in, conv_w, conv_b, dt_bias, a_log, d_skip, ssm_norm_w,
           w_ssm_branch, w_attn_branch, w_out, norm_ffn_w, peer_wq, peer_sub_keys, peer_u, peer_v):
    b, T, _ = x.shape
    h = _rmsnorm(x, norm_mix_w)
    z, xbc, dt_raw, q, kv_c, kv_s, kv_w, nsa_g, br_g = _split_proj(h @ w_in)
    y_ssm, conv_new, ssm_new = _mamba_branch(z, xbc, dt_raw, conv_prev, ssm_prev, conv_w, conv_b, dt_bias, a_log,
                                             d_skip, ssm_norm_w)
    q = q.reshape(b, T, KV_HEADS, HEADS_PER_KV, ATTN_HEAD_DIM)
    kv_c = kv_c.reshape(b, T, KV_HEADS, 2, ATTN_HEAD_DIM)
    kv_s = kv_s.reshape(b, T, KV_HEADS, 2, ATTN_HEAD_DIM)
    kv_w = kv_w.reshape(b, T, KV_HEADS, 2, ATTN_HEAD_DIM)
    gates = jax.nn.sigmoid(nsa_g.astype(jnp.float32)).reshape(b, T, KV_HEADS, HEADS_PER_KV, 3)
    o_attn, win_new = nsa_core(q, kv_c, kv_s, kv_w, gates)
    y_attn = o_attn.reshape(b, T, ATTN_WIDTH) @ w_attn_branch
    g = jax.nn.sigmoid(br_g.astype(jnp.float32))
    merged = (g[..., :D_MODEL] * (y_ssm @ w_ssm_branch).astype(jnp.float32)
              + g[..., D_MODEL:] * y_attn.astype(jnp.float32)).astype(x.dtype)
    x = x + merged @ w_out
    x = x + _peer_tokens(_rmsnorm(x, norm_ffn_w), peer_wq, peer_sub_keys, peer_u, peer_v)
    return x, kv_c, kv_s, win_new, conv_new, ssm_new


def setup_inputs(seed: int = 0) -> dict:
    key = jax.random.key(seed)
    ks = jax.random.split(key, 24)
    f32 = jnp.float32
    n_pages = PAST_LEN // PAGE_SIZE
    n_pool = (DEC_BATCH * n_pages * 5) // 4
    w_eff = min(WINDOW, PAST_LEN)

    def nrm(k, shape, scale):
        return jax.random.normal(k, shape, f32) * scale

    dt0 = jnp.exp(jax.random.uniform(ks[10], (DEPTH, SSM_HEADS), f32) * (math.log(0.1) - math.log(0.001)) + math.log(0.001))
    return {
        'x_prompt': nrm(ks[0], (BATCH, SEQ, D_MODEL), 1.0),
        'x_sample': nrm(ks[1], (DEC_BATCH, DEC_SEQ, D_MODEL), 1.0),
        'cache_cmp_kv': nrm(ks[2], (DEPTH, n_pool, PAGE_SIZE, KV_HEADS, 2, ATTN_HEAD_DIM), 1.0),
        'cache_sel_kv': nrm(ks[3], (DEPTH, n_pool, PAGE_SIZE, KV_HEADS, 2, ATTN_HEAD_DIM), 1.0),
        'cache_win_kv': nrm(ks[4], (DEPTH, DEC_BATCH, w_eff, KV_HEADS, 2, ATTN_HEAD_DIM), 1.0),
        'state_ssm': nrm(ks[5], (DEPTH, DEC_BATCH, SSM_HEADS, SSM_HEAD_DIM, SSM_STATE), 0.3),
        'state_conv': nrm(ks[6], (DEPTH, DEC_BATCH, CONV_WIDTH - 1, CONV_DIM), 1.0),
        'page_table': jax.random.permutation(ks[7], n_pool)[:DEC_BATCH * n_pages].reshape(DEC_BATCH, n_pages).astype(jnp.int32),
        'norm_mix_w': 1.0 + nrm(ks[8], (DEPTH, D_MODEL), 0.02),
        'w_in': nrm(ks[9], (DEPTH, D_MODEL, PROJ_DIM), D_MODEL ** -0.5),
        'conv_w': nrm(ks[11], (DEPTH, CONV_WIDTH, CONV_DIM), CONV_WIDTH ** -0.5),
        'conv_b': nrm(ks[12], (DEPTH, CONV_DIM), 0.02),
        'dt_bias': dt0 + jnp.log(-jnp.expm1(-dt0)),
        'a_log': jnp.log(jax.random.uniform(ks[13], (DEPTH, SSM_HEADS), f32, 1.0, 16.0)),
        'd_skip': 1.0 + nrm(ks[14], (DEPTH, SSM_HEADS), 0.02),
        'ssm_norm_w': 1.0 + nrm(ks[15], (DEPTH, D_INNER), 0.02),
        'w_ssm_branch': nrm(ks[16], (DEPTH, D_INNER, D_MODEL), D_INNER ** -0.5),
        'w_attn_branch': nrm(ks[17], (DEPTH, ATTN_WIDTH, D_MODEL), ATTN_WIDTH ** -0.5),
        'w_out': nrm(ks[18], (DEPTH, D_MODEL, D_MODEL), D_MODEL ** -0.5),
        'norm_ffn_w': 1.0 + nrm(ks[19], (DEPTH, D_MODEL), 0.02),
        'peer_wq': nrm(ks[20], (DEPTH, D_MODEL, PEER_HEADS * PEER_QDIM), D_MODEL ** -0.5),
        'peer_sub_keys': nrm(ks[21], (DEPTH, PEER_HEADS, 2, PEER_KEYS, PEER_HALF), PEER_HALF ** -0.5),
        'peer_u': nrm(ks[22], (DEPTH, N_EXPERTS, D_MODEL), D_MODEL ** -0.5),
        'peer_v': nrm(ks[23], (DEPTH, N_EXPERTS, D_MODEL), PEER_HEADS ** -0.5),
        'final_norm_w': 1.0 + nrm(jax.random.fold_in(key, 99), (D_MODEL,), 0.02),
    }


def reference(x_prompt, x_sample, cache_cmp_kv, cache_sel_kv, cache_win_kv, state_ssm, state_conv, page_table,
              norm_mix_w, w_in, conv_w, conv_b, dt_bias, a_log, d_skip, ssm_norm_w, w_ssm_branch, w_attn_branch,
              w_out, norm_ffn_w, peer_wq, peer_sub_keys, peer_u, peer_v, final_norm_w):
    xp, xs = x_prompt, x_sample
    cmp_p, cmp_s, sel_p, sel_s, win_p, win_s = [], [], [], [], [], []
    ssm_p, ssm_s, conv_p, conv_s = [], [], [], []
    for layer in range(DEPTH):
        lw = [a[layer] for a in (norm_mix_w, w_in, conv_w, conv_b, dt_bias, a_log, d_skip, ssm_norm_w, w_ssm_branch,
                                 w_attn_branch, w_out, norm_ffn_w, peer_wq, peer_sub_keys, peer_u, peer_v)]
        bp = xp.shape[0]
        conv0 = jnp.zeros((bp, CONV_WIDTH - 1, CONV_DIM), xp.dtype)
        ssm0 = jnp.zeros((bp, SSM_HEADS, SSM_HEAD_DIM, SSM_STATE), xp.dtype)
        xp, kc, ks, kw, cv, ss = _block(xp, conv0, ssm0, _nsa_prompt, *lw)
        cmp_p.append(kc); sel_p.append(ks); win_p.append(kw); conv_p.append(cv); ssm_p.append(ss)
        core = functools.partial(_nsa_sample, cache_cmp_kv[layer], cache_sel_kv[layer], cache_win_kv[layer], page_table)
        xs, kc, ks, kw, cv, ss = _block(xs, state_conv[layer], state_ssm[layer], core, *lw)
        cmp_s.append(kc); sel_s.append(ks); win_s.append(kw); conv_s.append(cv); ssm_s.append(ss)
    y_prompt = _rmsnorm(xp, final_norm_w)
    y_sample = _rmsnorm(xs, final_norm_w)
    return (y_prompt, y_sample, jnp.stack(cmp_p), jnp.stack(cmp_s), jnp.stack(sel_p), jnp.stack(sel_s),
            jnp.stack(win_p), jnp.stack(win_s), jnp.stack(ssm_p), jnp.stack(ssm_s), jnp.stack(conv_p), jnp.stack(conv_s))
```

```python
import math, functools
import jax, jax.numpy as jnp
from jax import lax
import numpy as np
from jax.experimental import pallas as pl
from jax.experimental.pallas import tpu as pltpu

D_MODEL = 1024
BATCH = 8
SEQ = 4096
DEPTH = 1
DEC_BATCH = 32
DEC_SEQ = 8
PAST_LEN = 16384
PAGE_SIZE = 128

SSM_EXPAND = 2
D_INNER = SSM_EXPAND * D_MODEL
SSM_HEAD_DIM = 64
SSM_HEADS = D_INNER // SSM_HEAD_DIM
SSM_GROUPS = 4
SSM_HPG = SSM_HEADS // SSM_GROUPS
SSM_STATE = 128
CONV_WIDTH = 4
CONV_DIM = D_INNER + 2 * SSM_GROUPS * SSM_STATE
SSD_CHUNK = 128
ATTN_HEADS = 16
ATTN_HEAD_DIM = 64
ATTN_WIDTH = ATTN_HEADS * ATTN_HEAD_DIM
KV_HEADS = 2
HEADS_PER_KV = ATTN_HEADS // KV_HEADS
KV_WIDTH = 2 * KV_HEADS * ATTN_HEAD_DIM
CMP_BLOCK = 64
SEL_BLOCK = CMP_BLOCK
SEL_TOPK = 16
SEL_BONUS = 1000.0
WINDOW = 512
WIN_QBLOCK = 128
SEL_QBLOCK = 32
ATTN_SCALE = ATTN_HEAD_DIM ** -0.5
PEER_HEADS = 8
PEER_KEYS = 128
N_EXPERTS = PEER_KEYS * PEER_KEYS
PEER_QDIM = 256
PEER_HALF = PEER_QDIM // 2
PEER_TOPK = 16
PEER_TOKEN_CHUNK = 128
EPS = 1e-6
PROJ_SIZES = (D_INNER, CONV_DIM, SSM_HEADS, ATTN_WIDTH, KV_WIDTH, KV_WIDTH, KV_WIDTH, 3 * ATTN_HEADS, 2 * D_MODEL)
PROJ_DIM = sum(PROJ_SIZES)


def _rmsnorm(x, w):
    xf = x.astype(jnp.float32)
    y = xf * lax.rsqrt(jnp.mean(xf * xf, axis=-1, keepdims=True) + EPS)
    return (y * w.astype(jnp.float32)).astype(x.dtype)


def _rmsnorm_kernel(x_ref, w_ref, o_ref):
    x = x_ref[...]
    ms = jnp.mean(x * x, axis=-1, keepdims=True)
    o_ref[...] = x * lax.rsqrt(ms + EPS) * w_ref[...]


def _final_norm(x, w):
    shape = x.shape
    xf = x.reshape(-1, D_MODEL)
    n = xf.shape[0]
    rows = min(n, 512)
    out = pl.pallas_call(
        _rmsnorm_kernel,
        grid=(n // rows,),
        in_specs=[pl.BlockSpec((rows, D_MODEL), lambda i: (i, 0)),
                  pl.BlockSpec((1, D_MODEL), lambda i: (0, 0))],
        out_specs=pl.BlockSpec((rows, D_MODEL), lambda i: (i, 0)),
        out_shape=jax.ShapeDtypeStruct((n, D_MODEL), x.dtype),
        name="final_rmsnorm",
    )(xf, w.reshape(1, D_MODEL))
    return out.reshape(shape)


def _masked_softmax(s, mask):
    s = jnp.where(mask, s.astype(jnp.float32), -jnp.inf)
    m = jnp.max(s, axis=-1, keepdims=True)
    m = jnp.where(jnp.isfinite(m), m, 0.0)
    p = jnp.exp(s - m)
    return p / jnp.maximum(jnp.sum(p, axis=-1, keepdims=True), 1e-30)


def _split_proj(p):
    idx = np.cumsum(PROJ_SIZES)[:-1].tolist()
    return jnp.split(p, idx, axis=-1)


def _to_blocks(a, blk):
    b, T = a.shape[:2]
    return jnp.moveaxis(a.reshape(b, T // blk, blk, *a.shape[2:]), 1, 0)


def _from_blocks(a):
    nb, b, blk = a.shape[:3]
    return jnp.moveaxis(a, 0, 1).reshape(b, nb * blk, *a.shape[3:])


def _causal_conv(xbc, conv_prev, conv_w, conv_b):
    xp = jnp.concatenate([conv_prev.astype(xbc.dtype), xbc], axis=1)
    y = lax.conv_general_dilated(xp, conv_w.astype(xbc.dtype)[:, None, :], window_strides=(1,), padding='VALID',
                                 dimension_numbers=('NWC', 'WIO', 'NWC'), feature_group_count=CONV_DIM)
    return jax.nn.silu(y + conv_b.astype(xbc.dtype)), xp[:, -(CONV_WIDTH - 1):]


def _ssd(x, dt, A, Bm, Cm, h0, chunk):
    f32 = jnp.float32
    b, L = x.shape[:2]
    nc = L // chunk
    xc = x.astype(f32).reshape(b, nc, chunk, SSM_GROUPS, SSM_HPG, SSM_HEAD_DIM)
    dtc = dt.reshape(b, nc, chunk, SSM_GROUPS, SSM_HPG)
    Bc = Bm.astype(f32).reshape(b, nc, chunk, SSM_GROUPS, SSM_STATE)
    Cc = Cm.astype(f32).reshape(b, nc, chunk, SSM_GROUPS, SSM_STATE)
    a_cum = jnp.cumsum(dtc * A.reshape(SSM_GROUPS, SSM_HPG), axis=2)
    tril = jnp.tril(jnp.ones((chunk, chunk), bool))[:, :, None, None]
    diff = a_cum[:, :, :, None] - a_cum[:, :, None]
    decay = jnp.exp(jnp.where(tril, diff, -jnp.inf))
    xdt = xc * dtc[..., None]
    cb = jnp.einsum('bclgn,bcsgn->bclsg', Cc, Bc)
    y_diag = jnp.einsum('bclsgj,bcsgjp->bclgjp', cb[..., None] * decay, xdt)
    decay_to_end = jnp.exp(a_cum[:, :, -1:] - a_cum)
    chunk_states = jnp.einsum('bclgn,bclgjp->bcgjpn', Bc, xdt * decay_to_end[..., None])
    chunk_decay = jnp.exp(a_cum[:, :, -1])

    def step(h, inp):
        s_c, d_c = inp
        return h * d_c[..., None, None] + s_c, h

    h_init = h0.astype(f32).reshape(b, SSM_GROUPS, SSM_HPG, SSM_HEAD_DIM, SSM_STATE)
    h_last, h_prev = lax.scan(step, h_init, (jnp.moveaxis(chunk_states, 1, 0), jnp.moveaxis(chunk_decay, 1, 0)))
    h_prev = jnp.moveaxis(h_prev, 0, 1)
    y_off = jnp.einsum('bclgn,bcgjpn->bclgjp', Cc, h_prev) * jnp.exp(a_cum)[..., None]
    y = (y_diag + y_off).reshape(b, L, SSM_HEADS, SSM_HEAD_DIM)
    return y, h_last.reshape(b, SSM_HEADS, SSM_HEAD_DIM, SSM_STATE)


def _mamba_branch(z, xbc, dt_raw, conv_prev, ssm_prev, conv_w, conv_b, dt_bias, a_log, d_skip, ssm_norm_w):
    f32 = jnp.float32
    b, L, _ = xbc.shape
    xbc, conv_new = _causal_conv(xbc, conv_prev, conv_w, conv_b)
    gn = SSM_GROUPS * SSM_STATE
    xs = xbc[..., :D_INNER].reshape(b, L, SSM_HEADS, SSM_HEAD_DIM)
    Bm = xbc[..., D_INNER:D_INNER + gn].reshape(b, L, SSM_GROUPS, SSM_STATE)
    Cm = xbc[..., D_INNER + gn:].reshape(b, L, SSM_GROUPS, SSM_STATE)
    dt = jax.nn.softplus(dt_raw.astype(f32) + dt_bias.astype(f32))
    A = -jnp.exp(a_log.astype(f32))
    y, ssm_new = _ssd(xs, dt, A, Bm, Cm, ssm_prev, math.gcd(L, SSD_CHUNK))
    y = y + d_skip.astype(f32)[:, None] * xs.astype(f32)
    y = y.reshape(b, L, D_INNER) * jax.nn.silu(z.astype(f32))
    y = _rmsnorm(y, ssm_norm_w).astype(z.dtype)
    return y, conv_new, ssm_new.astype(ssm_prev.dtype)


def _cmp_attend(q, cmp_kv, q_pos):
    n = cmp_kv.shape[1]
    blk_end = jnp.arange(n) * CMP_BLOCK + (CMP_BLOCK - 1)
    s = jnp.einsum('btgjd,bngd->btgjn', q, cmp_kv[..., 0, :]).astype(jnp.float32) * ATTN_SCALE
    mask = (blk_end[None, :] <= q_pos[:, None])[None, :, None, None, :]
    p = _masked_softmax(s, mask)
    o = jnp.einsum('btgjn,bngd->btgjd', p.astype(cmp_kv.dtype), cmp_kv[..., 1, :])
    return o, jnp.sum(p, axis=3)


def _select_blocks(imp, q_pos, n_sel):
    imp = jnp.pad(imp, ((0, 0), (0, 0), (0, 0), (0, n_sel - imp.shape[-1])))
    j = jnp.arange(n_sel)[None, :]
    cur = (q_pos // SEL_BLOCK)[:, None]
    forced = ((j == 0) | (j == cur) | (j == cur - 1)).astype(jnp.float32)
    score = jnp.where((j <= cur)[None, :, None, :], imp + SEL_BONUS * forced[None, :, None, :], -jnp.inf)
    _, idx = lax.top_k(score, min(SEL_TOPK, n_sel))
    return idx


def _block_positions(idx):
    pos = idx[..., None] * SEL_BLOCK + jnp.arange(SEL_BLOCK)
    return pos.reshape(*idx.shape[:-1], -1)


def _sel_attend(q, rows, pos, q_pos):
    s = jnp.einsum('btgjd,btgsd->btgjs', q, rows[..., 0, :]).astype(jnp.float32) * ATTN_SCALE
    mask = (pos <= q_pos[None, :, None, None])[:, :, :, None, :]
    p = _masked_softmax(s, mask)
    return jnp.einsum('btgjs,btgsd->btgjd', p.astype(rows.dtype), rows[..., 1, :])


def _band_attend(q, kv, q_pos, k_pos):
    s = jnp.einsum('btgjd,bsgd->btgjs', q, kv[..., 0, :]).astype(jnp.float32) * ATTN_SCALE
    rel = q_pos[:, None] - k_pos[None, :]
    mask = ((rel >= 0) & (rel <= WINDOW) & (k_pos[None, :] >= 0))[None, :, None, None, :]
    p = _masked_softmax(s, mask)
    return jnp.einsum('btgjs,bsgd->btgjd', p.astype(kv.dtype), kv[..., 1, :])


def _combine(gates, o_cmp, o_sel, o_win):
    o = gates[..., 0:1] * o_cmp.astype(jnp.float32) + gates[..., 1:2] * o_sel.astype(jnp.float32) \
        + gates[..., 2:3] * o_win.astype(jnp.float32)
    return o.astype(o_cmp.dtype)


def _nsa_prompt(q, kv_cmp, kv_sel, kv_win, gates):
    b, T = q.shape[:2]
    q_pos = jnp.arange(T)
    n_blk = T // CMP_BLOCK
    cmp_kv = kv_cmp.reshape(b, n_blk, CMP_BLOCK, KV_HEADS, 2, ATTN_HEAD_DIM).mean(axis=2)
    o_cmp, imp = _cmp_attend(q, cmp_kv, q_pos)
    idx = _select_blocks(imp, q_pos, n_blk)
    bidx = jnp.arange(b)[:, None, None, None]
    gidx = jnp.arange(KV_HEADS)[None, None, :, None]

    def sel_block(args):
        qb, ib, pb = args
        pos = _block_positions(ib)
        return _sel_attend(qb, kv_sel[bidx, pos, gidx], pos, pb)

    o_sel = _from_blocks(lax.map(sel_block, (_to_blocks(q, SEL_QBLOCK), _to_blocks(idx, SEL_QBLOCK),
                                             q_pos.reshape(-1, SEL_QBLOCK))))
    kv_pad = jnp.pad(kv_win, ((0, 0), (WINDOW, 0), (0, 0), (0, 0), (0, 0)))
    span = WINDOW + WIN_QBLOCK

    def win_block(args):
        qb, start = args
        kb = lax.dynamic_slice_in_dim(kv_pad, start, span, axis=1)
        return _band_attend(qb, kb, start + jnp.arange(WIN_QBLOCK), start - WINDOW + jnp.arange(span))

    starts = jnp.arange(T // WIN_QBLOCK) * WIN_QBLOCK
    o_win = _from_blocks(lax.map(win_block, (_to_blocks(q, WIN_QBLOCK), starts)))
    return _combine(gates, o_cmp, o_sel, o_win), kv_win[:, -min(WINDOW, T):]


def _gather_paged_rows(pool, page_table, new_rows, pos):
    past = page_table.shape[1] * PAGE_SIZE
    b = page_table.shape[0]
    bidx = jnp.arange(b)[:, None, None, None]
    gidx = jnp.arange(KV_HEADS)[None, None, :, None]
    past_pos = jnp.minimum(pos, past - 1)
    phys = page_table[bidx, past_pos // PAGE_SIZE]
    from_pool = pool[phys, past_pos % PAGE_SIZE, gidx]
    new_pos = jnp.clip(pos - past, 0, new_rows.shape[1] - 1)
    from_new = new_rows[bidx, new_pos, gidx]
    return jnp.where((pos < past)[..., None, None], from_pool.astype(new_rows.dtype), from_new)


def _nsa_sample(pool_cmp, pool_sel, win_buf, page_table, q, kv_cmp, kv_sel, kv_win, gates):
    b, T = q.shape[:2]
    n_pages = page_table.shape[1]
    past = n_pages * PAGE_SIZE
    q_pos = past + jnp.arange(T)
    past_cmp = pool_cmp[page_table].reshape(b, past // CMP_BLOCK, CMP_BLOCK, KV_HEADS, 2, ATTN_HEAD_DIM).mean(axis=2)
    n_new = T // CMP_BLOCK
    new_cmp = kv_cmp[:, :n_new * CMP_BLOCK].reshape(b, n_new, CMP_BLOCK, KV_HEADS, 2, ATTN_HEAD_DIM).mean(axis=2)
    cmp_kv = jnp.concatenate([past_cmp.astype(kv_cmp.dtype), new_cmp], axis=1)
    o_cmp, imp = _cmp_attend(q, cmp_kv, q_pos)
    n_sel = -(-(past + T) // SEL_BLOCK)
    idx = _select_blocks(imp, q_pos, n_sel)
    pos = _block_positions(idx)
    o_sel = _sel_attend(q, _gather_paged_rows(pool_sel, page_table, kv_sel, pos), pos, q_pos)
    w_eff = win_buf.shape[1]
    keys = jnp.concatenate([win_buf.astype(kv_win.dtype), kv_win], axis=1)
    o_win = _band_attend(q, keys, q_pos, past - w_eff + jnp.arange(w_eff + T))
    return _combine(gates, o_cmp, o_sel, o_win), keys[:, -w_eff:]


def _peer(h, peer_wq, peer_sub_keys, peer_u, peer_v):
    n = h.shape[0]
    q = (h @ peer_wq).reshape(n, PEER_HEADS, 2, PEER_HALF)
    s = jnp.einsum('nhcd,hckd->nhck', q, peer_sub_keys).astype(jnp.float32)
    top_s, top_i = lax.top_k(s, PEER_TOPK)
    cand_s = (top_s[:, :, 0, :, None] + top_s[:, :, 1, None, :]).reshape(n, PEER_HEADS, -1)
    cand_i = (top_i[:, :, 0, :, None] * PEER_KEYS + top_i[:, :, 1, None, :]).reshape(n, PEER_HEADS, -1)
    best_s, best_pos = lax.top_k(cand_s, PEER_TOPK)
    experts = jnp.take_along_axis(cand_i, best_pos, axis=-1)
    g = jax.nn.softmax(best_s, axis=-1)
    act = jax.nn.gelu(jnp.einsum('nd,nhkd->nhk', h, peer_u[experts]).astype(jnp.float32))
    return jnp.einsum('nhk,nhkd->nd', (g * act).astype(peer_v.dtype), peer_v[experts])


def _peer_tokens(h, peer_wq, peer_sub_keys, peer_u, peer_v):
    shape = h.shape
    hf = h.reshape(-1, D_MODEL)
    n = hf.shape[0]
    if n % PEER_TOKEN_CHUNK == 0 and n > PEER_TOKEN_CHUNK:
        out = lax.map(lambda c: _peer(c, peer_wq, peer_sub_keys, peer_u, peer_v),
                      hf.reshape(-1, PEER_TOKEN_CHUNK, D_MODEL)).reshape(n, D_MODEL)
    else:
        out = _peer(hf, peer_wq, peer_sub_keys, peer_u, peer_v)
    return out.reshape(shape)


def _block(x, conv_prev, ssm_prev, nsa_core, norm_mix_w, w_in, conv_w, conv_b, dt_bias, a_log, d_skip, ssm_norm_w,
           w_ssm_branch, w_attn_branch, w_out, norm_ffn_w, peer_wq, peer_sub_keys, peer_u, peer_v):
    b, T, _ = x.shape
    h = _rmsnorm(x, norm_mix_w)
    z, xbc, dt_raw, q, kv_c, kv_s, kv_w, nsa_g, br_g = _split_proj(h @ w_in)
    y_ssm, conv_new, ssm_new = _mamba_branch(z, xbc, dt_raw, conv_prev, ssm_prev, conv_w, conv_b, dt_bias, a_log,
                                             d_skip, ssm_norm_w)
    q = q.reshape(b, T, KV_HEADS, HEADS_PER_KV, ATTN_HEAD_DIM)
    kv_c = kv_c.reshape(b, T, KV_HEADS, 2, ATTN_HEAD_DIM)
    kv_s = kv_s.reshape(b, T, KV_HEADS, 2, ATTN_HEAD_DIM)
    kv_w = kv_w.reshape(b, T, KV_HEADS, 2, ATTN_HEAD_DIM)
    gates = jax.nn.sigmoid(nsa_g.astype(jnp.float32)).reshape(b, T, KV_HEADS, HEADS_PER_KV, 3)
    o_attn, win_new = nsa_core(q, kv_c, kv_s, kv_w, gates)
    y_attn = o_attn.reshape(b, T, ATTN_WIDTH) @ w_attn_branch
    g = jax.nn.sigmoid(br_g.astype(jnp.float32))
    merged = (g[..., :D_MODEL] * (y_ssm @ w_ssm_branch).astype(jnp.float32)
              + g[..., D_MODEL:] * y_attn.astype(jnp.float32)).astype(x.dtype)
    x = x + merged @ w_out
    x = x + _peer_tokens(_rmsnorm(x, norm_ffn_w), peer_wq, peer_sub_keys, peer_u, peer_v)
    return x, kv_c, kv_s, win_new, conv_new, ssm_new


def kernel(x_prompt, x_sample, cache_cmp_kv, cache_sel_kv, cache_win_kv, state_ssm, state_conv, page_table,
           norm_mix_w, w_in, conv_w, conv_b, dt_bias, a_log, d_skip, ssm_norm_w, w_ssm_branch, w_attn_branch,
           w_out, norm_ffn_w, peer_wq, peer_sub_keys, peer_u, peer_v, final_norm_w):
    xp, xs = x_prompt, x_sample
    cmp_p, cmp_s, sel_p, sel_s, win_p, win_s = [], [], [], [], [], []
    ssm_p, ssm_s, conv_p, conv_s = [], [], [], []
    for layer in range(DEPTH):
        lw = [a[layer] for a in (norm_mix_w, w_in, conv_w, conv_b, dt_bias, a_log, d_skip, ssm_norm_w, w_ssm_branch,
                                 w_attn_branch, w_out, norm_ffn_w, peer_wq, peer_sub_keys, peer_u, peer_v)]
        bp = xp.shape[0]
        conv0 = jnp.zeros((bp, CONV_WIDTH - 1, CONV_DIM), xp.dtype)
        ssm0 = jnp.zeros((bp, SSM_HEADS, SSM_HEAD_DIM, SSM_STATE), xp.dtype)
        xp, kc, ks, kw, cv, ss = _block(xp, conv0, ssm0, _nsa_prompt, *lw)
        cmp_p.append(kc); sel_p.append(ks); win_p.append(kw); conv_p.append(cv); ssm_p.append(ss)
        core = functools.partial(_nsa_sample, cache_cmp_kv[layer], cache_sel_kv[layer], cache_win_kv[layer], page_table)
        xs, kc, ks, kw, cv, ss = _block(xs, state_conv[layer], state_ssm[layer], core, *lw)
        cmp_s.append(kc); sel_s.append(ks); win_s.append(kw); conv_s.append(cv); ssm_s.append(ss)
    y_prompt = _final_norm(xp, final_norm_w)
    y_sample = _final_norm(xs, final_norm_w)
    return (y_prompt, y_sample, jnp.stack(cmp_p), jnp.stack(cmp_s), jnp.stack(sel_p), jnp.stack(sel_s),
            jnp.stack(win_p), jnp.stack(win_s), jnp.stack(ssm_p), jnp.stack(ssm_s), jnp.stack(conv_p), jnp.stack(conv_s))
```

```python
import math, functools
import jax, jax.numpy as jnp
from jax import lax
import numpy as np
from jax.experimental import pallas as pl
from jax.experimental.pallas import tpu as pltpu

D_MODEL = 1024
BATCH = 8
SEQ = 4096
DEPTH = 1
DEC_BATCH = 32
DEC_SEQ = 8
PAST_LEN = 16384
PAGE_SIZE = 128

SSM_EXPAND = 2
D_INNER = SSM_EXPAND * D_MODEL
SSM_HEAD_DIM = 64
SSM_HEADS = D_INNER // SSM_HEAD_DIM
SSM_GROUPS = 4
SSM_HPG = SSM_HEADS // SSM_GROUPS
SSM_STATE = 128
CONV_WIDTH = 4
CONV_DIM = D_INNER + 2 * SSM_GROUPS * SSM_STATE
SSD_CHUNK = 128
ATTN_HEADS = 16
ATTN_HEAD_DIM = 64
ATTN_WIDTH = ATTN_HEADS * ATTN_HEAD_DIM
KV_HEADS = 2
HEADS_PER_KV = ATTN_HEADS // KV_HEADS
KV_WIDTH = 2 * KV_HEADS * ATTN_HEAD_DIM
CMP_BLOCK = 64
SEL_BLOCK = CMP_BLOCK
SEL_TOPK = 16
SEL_BONUS = 1000.0
WINDOW = 512
WIN_QBLOCK = 128
SEL_QBLOCK = 32
ATTN_SCALE = ATTN_HEAD_DIM ** -0.5
PEER_HEADS = 8
PEER_KEYS = 128
N_EXPERTS = PEER_KEYS * PEER_KEYS
PEER_QDIM = 256
PEER_HALF = PEER_QDIM // 2
PEER_TOPK = 16
PEER_TOKEN_CHUNK = 128
EPS = 1e-6
PROJ_SIZES = (D_INNER, CONV_DIM, SSM_HEADS, ATTN_WIDTH, KV_WIDTH, KV_WIDTH, KV_WIDTH, 3 * ATTN_HEADS, 2 * D_MODEL)
PROJ_DIM = sum(PROJ_SIZES)


def _rmsnorm(x, w):
    xf = x.astype(jnp.float32)
    y = xf * lax.rsqrt(jnp.mean(xf * xf, axis=-1, keepdims=True) + EPS)
    return (y * w.astype(jnp.float32)).astype(x.dtype)


def _rmsnorm_kernel(x_ref, w_ref, o_ref):
    x = x_ref[...]
    ms = jnp.mean(x * x, axis=-1, keepdims=True)
    o_ref[...] = x * lax.rsqrt(ms + EPS) * w_ref[...]


def _final_norm(x, w):
    shape = x.shape
    xf = x.reshape(-1, D_MODEL)
    n = xf.shape[0]
    rows = min(n, 512)
    out = pl.pallas_call(
        _rmsnorm_kernel,
        grid=(n // rows,),
        in_specs=[pl.BlockSpec((rows, D_MODEL), lambda i: (i, 0)),
                  pl.BlockSpec((1, D_MODEL), lambda i: (0, 0))],
        out_specs=pl.BlockSpec((rows, D_MODEL), lambda i: (i, 0)),
        out_shape=jax.ShapeDtypeStruct((n, D_MODEL), x.dtype),
        name="final_rmsnorm",
    )(xf, w.reshape(1, D_MODEL))
    return out.reshape(shape)


def _masked_softmax(s, mask):
    s = jnp.where(mask, s.astype(jnp.float32), -jnp.inf)
    m = jnp.max(s, axis=-1, keepdims=True)
    m = jnp.where(jnp.isfinite(m), m, 0.0)
    p = jnp.exp(s - m)
    return p / jnp.maximum(jnp.sum(p, axis=-1, keepdims=True), 1e-30)


def _split_proj(p):
    idx = np.cumsum(PROJ_SIZES)[:-1].tolist()
    return jnp.split(p, idx, axis=-1)


def _to_blocks(a, blk):
    b, T = a.shape[:2]
    return jnp.moveaxis(a.reshape(b, T // blk, blk, *a.shape[2:]), 1, 0)


def _from_blocks(a):
    nb, b, blk = a.shape[:3]
    return jnp.moveaxis(a, 0, 1).reshape(b, nb * blk, *a.shape[3:])


def _causal_conv(xbc, conv_prev, conv_w, conv_b):
    xp = jnp.concatenate([conv_prev.astype(xbc.dtype), xbc], axis=1)
    y = lax.conv_general_dilated(xp, conv_w.astype(xbc.dtype)[:, None, :], window_strides=(1,), padding='VALID',
                                 dimension_numbers=('NWC', 'WIO', 'NWC'), feature_group_count=CONV_DIM)
    return jax.nn.silu(y + conv_b.astype(xbc.dtype)), xp[:, -(CONV_WIDTH - 1):]


def _ssd(x, dt, A, Bm, Cm, h0, chunk):
    f32 = jnp.float32
    b, L = x.shape[:2]
    nc = L // chunk
    xc = x.astype(f32).reshape(b, nc, chunk, SSM_GROUPS, SSM_HPG, SSM_HEAD_DIM)
    dtc = dt.reshape(b, nc, chunk, SSM_GROUPS, SSM_HPG)
    Bc = Bm.astype(f32).reshape(b, nc, chunk, SSM_GROUPS, SSM_STATE)
    Cc = Cm.astype(f32).reshape(b, nc, chunk, SSM_GROUPS, SSM_STATE)
    a_cum = jnp.cumsum(dtc * A.reshape(SSM_GROUPS, SSM_HPG), axis=2)
    tril = jnp.tril(jnp.ones((chunk, chunk), bool))[:, :, None, None]
    diff = a_cum[:, :, :, None] - a_cum[:, :, None]
    decay = jnp.exp(jnp.where(tril, diff, -jnp.inf))
    xdt = xc * dtc[..., None]
    cb = jnp.einsum('bclgn,bcsgn->bclsg', Cc, Bc)
    y_diag = jnp.einsum('bclsgj,bcsgjp->bclgjp', cb[..., None] * decay, xdt)
    decay_to_end = jnp.exp(a_cum[:, :, -1:] - a_cum)
    chunk_states = jnp.einsum('bclgn,bclgjp->bcgjpn', Bc, xdt * decay_to_end[..., None])
    chunk_decay = jnp.exp(a_cum[:, :, -1])

    def step(h, inp):
        s_c, d_c = inp
        return h * d_c[..., None, None] + s_c, h

    h_init = h0.astype(f32).reshape(b, SSM_GROUPS, SSM_HPG, SSM_HEAD_DIM, SSM_STATE)
    h_last, h_prev = lax.scan(step, h_init, (jnp.moveaxis(chunk_states, 1, 0), jnp.moveaxis(chunk_decay, 1, 0)))
    h_prev = jnp.moveaxis(h_prev, 0, 1)
    y_off = jnp.einsum('bclgn,bcgjpn->bclgjp', Cc, h_prev) * jnp.exp(a_cum)[..., None]
    y = (y_diag + y_off).reshape(b, L, SSM_HEADS, SSM_HEAD_DIM)
    return y, h_last.reshape(b, SSM_HEADS, SSM_HEAD_DIM, SSM_STATE)


def _mamba_branch(z, xbc, dt_raw, conv_prev, ssm_prev, conv_w, conv_b, dt_bias, a_log, d_skip, ssm_norm_w):
    f32 = jnp.float32
    b, L, _ = xbc.shape
    xbc, conv_new = _causal_conv(xbc, conv_prev, conv_w, conv_b)
    gn = SSM_GROUPS * SSM_STATE
    xs = xbc[..., :D_INNER].reshape(b, L, SSM_HEADS, SSM_HEAD_DIM)
    Bm = xbc[..., D_INNER:D_INNER + gn].reshape(b, L, SSM_GROUPS, SSM_STATE)
    Cm = xbc[..., D_INNER + gn:].reshape(b, L, SSM_GROUPS, SSM_STATE)
    dt = jax.nn.softplus(dt_raw.astype(f32) + dt_bias.astype(f32))
    A = -jnp.exp(a_log.astype(f32))
    y, ssm_new = _ssd(xs, dt, A, Bm, Cm, ssm_prev, math.gcd(L, SSD_CHUNK))
    y = y + d_skip.astype(f32)[:, None] * xs.astype(f32)
    y = y.reshape(b, L, D_INNER) * jax.nn.silu(z.astype(f32))
    y = _rmsnorm(y, ssm_norm_w).astype(z.dtype)
    return y, conv_new, ssm_new.astype(ssm_prev.dtype)


def _cmp_attend(q, cmp_kv, q_pos):
    n = cmp_kv.shape[1]
    blk_end = jnp.arange(n) * CMP_BLOCK + (CMP_BLOCK - 1)
    s = jnp.einsum('btgjd,bngd->btgjn', q, cmp_kv[..., 0, :]).astype(jnp.float32) * ATTN_SCALE
    mask = (blk_end[None, :] <= q_pos[:, None])[None, :, None, None, :]
    p = _masked_softmax(s, mask)
    o = jnp.einsum('btgjn,bngd->btgjd', p.astype(cmp_kv.dtype), cmp_kv[..., 1, :])
    return o, jnp.sum(p, axis=3)


def _select_blocks(imp, q_pos, n_sel):
    imp = jnp.pad(imp, ((0, 0), (0, 0), (0, 0), (0, n_sel - imp.shape[-1])))
    j = jnp.arange(n_sel)[None, :]
    cur = (q_pos // SEL_BLOCK)[:, None]
    forced = ((j == 0) | (j == cur) | (j == cur - 1)).astype(jnp.float32)
    score = jnp.where((j <= cur)[None, :, None, :], imp + SEL_BONUS * forced[None, :, None, :], -jnp.inf)
    _, idx = lax.top_k(score, min(SEL_TOPK, n_sel))
    return idx


def _block_positions(idx):
    pos = idx[..., None] * SEL_BLOCK + jnp.arange(SEL_BLOCK)
    return pos.reshape(*idx.shape[:-1], -1)


def _sel_attend(q, rows, pos, q_pos):
    s = jnp.einsum('btgjd,btgsd->btgjs', q, rows[..., 0, :]).astype(jnp.float32) * ATTN_SCALE
    mask = (pos <= q_pos[None, :, None, None])[:, :, :, None, :]
    p = _masked_softmax(s, mask)
    return jnp.einsum('btgjs,btgsd->btgjd', p.astype(rows.dtype), rows[..., 1, :])


def _band_attend(q, kv, q_pos, k_pos):
    s = jnp.einsum('btgjd,bsgd->btgjs', q, kv[..., 0, :]).astype(jnp.float32) * ATTN_SCALE
    rel = q_pos[:, None] - k_pos[None, :]
    mask = ((rel >= 0) & (rel <= WINDOW) & (k_pos[None, :] >= 0))[None, :, None, None, :]
    p = _masked_softmax(s, mask)
    return jnp.einsum('btgjs,bsgd->btgjd', p.astype(kv.dtype), kv[..., 1, :])


def _combine(gates, o_cmp, o_sel, o_win):
    o = gates[..., 0:1] * o_cmp.astype(jnp.float32) + gates[..., 1:2] * o_sel.astype(jnp.float32) \
        + gates[..., 2:3] * o_win.astype(jnp.float32)
    return o.astype(o_cmp.dtype)


NSA_TQ = 128
NSA_TK = 512
NSA_WSPAN = WINDOW + NSA_TQ
NEG_BIG = -1e30


def _nsa_prompt_kernel(q_ref, kvc_ref, ks_ref, vst_ref, kw_ref, vwt_ref, gate_ref, o_ref,
                       pk_ref, pvt_ref, m_ref, l_ref, acc_ref, *, seq):
    f32, bf16 = jnp.float32, jnp.bfloat16
    D, J, TQ, TK = ATTN_HEAD_DIM, HEADS_PER_KV, NSA_TQ, NSA_TK
    n_blk = seq // CMP_BLOCK
    i = pl.program_id(2)
    t0 = i * TQ

    @pl.when(i == 0)
    def _pool():
        c = kvc_ref[0]
        pooled = jnp.sum(c.reshape(n_blk, CMP_BLOCK, 2 * D), axis=1) * (1.0 / CMP_BLOCK)
        pk_ref[...] = pooled[:, :D].astype(bf16)
        pvt_ref[...] = pooled[:, D:].T.astype(bf16)

    qt = (q_ref[0] * ATTN_SCALE).T
    qs = jnp.concatenate([qt[j * D:(j + 1) * D, :] for j in range(J)], axis=1).astype(bf16)
    qpos1 = t0 + lax.broadcasted_iota(jnp.int32, (1, TQ), 1)
    qpos = jnp.concatenate([qpos1] * J, axis=1)

    s = jnp.dot(pk_ref[...], qs, preferred_element_type=f32)
    blk = lax.broadcasted_iota(jnp.int32, (n_blk, 1), 0)
    vis = (blk * CMP_BLOCK + (CMP_BLOCK - 1)) <= qpos
    s = jnp.where(vis, s, -jnp.inf)
    m = jnp.max(s, axis=0, keepdims=True)
    m = jnp.where(m == -jnp.inf, 0.0, m)
    p = jnp.exp(s - m)
    p = p / jnp.maximum(jnp.sum(p, axis=0, keepdims=True), 1e-30)
    o_cmp = jnp.dot(pvt_ref[...], p.astype(bf16), preferred_element_type=f32)
    imp = p[:, 0:TQ]
    for j in range(1, J):
        imp = imp + p[:, j * TQ:(j + 1) * TQ]

    cur = qpos1 // CMP_BLOCK
    blk2 = lax.broadcasted_iota(jnp.int32, (n_blk, TQ), 0)
    forced = ((blk2 == 0) | (blk2 == cur) | (blk2 == cur - 1)).astype(f32)
    score = jnp.where(blk2 <= cur, imp + SEL_BONUS * forced, -jnp.inf)
    sel = jnp.zeros((n_blk, TQ), f32)
    for _ in range(min(SEL_TOPK, n_blk)):
        mx = jnp.max(score, axis=0, keepdims=True)
        first = jnp.min(jnp.where(score == mx, blk2, n_blk), axis=0, keepdims=True)
        pick = blk2 == first
        sel = jnp.where(pick, 1.0, sel)
        score = jnp.where(pick, -jnp.inf, score)
    sel_b = sel.astype(bf16)

    m_ref[...] = jnp.full(m_ref.shape, NEG_BIG, f32)
    l_ref[...] = jnp.zeros(l_ref.shape, f32)
    acc_ref[...] = jnp.zeros(acc_ref.shape, f32)
    n_kt = (t0 + TQ + TK - 1) // TK

    def sel_tile(kt, carry):
        k0 = pl.multiple_of(kt * TK, TK)
        kpos = k0 + lax.broadcasted_iota(jnp.int32, (TK, 1), 0)
        expand = ((kpos // CMP_BLOCK) == lax.broadcasted_iota(jnp.int32, (TK, n_blk), 1)).astype(bf16)
        chosen = jnp.dot(expand, sel_b, preferred_element_type=f32)
        ok = (chosen > 0.5) & (kpos <= qpos1)
        bias1 = jnp.where(ok, 0.0, NEG_BIG)
        bias = jnp.concatenate([bias1] * J, axis=1)
        st = jnp.dot(ks_ref[0, 0, pl.ds(k0, TK), :], qs, preferred_element_type=f32) + bias
        m_old = m_ref[...]
        m_new = jnp.maximum(m_old, jnp.max(st, axis=0, keepdims=True))
        alpha = jnp.exp(m_old - m_new)
        pt = jnp.exp(st - m_new)
        l_ref[...] = alpha * l_ref[...] + jnp.sum(pt, axis=0, keepdims=True)
        acc_ref[...] = alpha * acc_ref[...] + jnp.dot(vst_ref[0, 0, :, pl.ds(k0, TK)], pt.astype(bf16),
                                                      preferred_element_type=f32)
        m_ref[...] = m_new
        return carry

    lax.fori_loop(0, n_kt, sel_tile, 0)
    o_sel = acc_ref[...] / l_ref[...]

    w0 = pl.multiple_of(jnp.maximum(t0 - WINDOW, 0), TQ)
    span = min(NSA_WSPAN, seq)
    kposw = w0 + lax.broadcasted_iota(jnp.int32, (span, 1), 0)
    rel = qpos1 - kposw
    biasw1 = jnp.where((rel >= 0) & (rel <= WINDOW), 0.0, NEG_BIG)
    biasw = jnp.concatenate([biasw1] * J, axis=1)
    sw = jnp.dot(kw_ref[0, 0, pl.ds(w0, span), :], qs, preferred_element_type=f32) + biasw
    pw = jnp.exp(sw - jnp.max(sw, axis=0, keepdims=True))
    lw = jnp.sum(pw, axis=0, keepdims=True)
    o_win = jnp.dot(vwt_ref[0, 0, :, pl.ds(w0, span)], pw.astype(bf16), preferred_element_type=f32) / lw

    def gate_row(br):
        g = jax.nn.sigmoid(gate_ref[0, 0, br])
        return jnp.concatenate([g[j:j + 1, :] for j in range(J)], axis=1)

    out = gate_row(0) * o_cmp + gate_row(1) * o_sel + gate_row(2) * o_win
    out = jnp.concatenate([out[:, j * TQ:(j + 1) * TQ] for j in range(J)], axis=0)
    o_ref[0] = out.T


def _nsa_prompt_attention(q, kv_c, kv_s, kv_w, nsa_g):
    b, T, _ = q.shape
    G, J, D, TQ = KV_HEADS, HEADS_PER_KV, ATTN_HEAD_DIM, NSA_TQ
    bf16 = jnp.bfloat16

    def split_kv(kv):
        r = kv.reshape(b, T, G, 2, D).astype(bf16)
        return jnp.transpose(r[:, :, :, 0], (0, 2, 1, 3)), jnp.transpose(r[:, :, :, 1], (0, 2, 3, 1))

    ks, vst = split_kv(kv_s)
    kw, vwt = split_kv(kv_w)
    gates_t = jnp.transpose(nsa_g.reshape(b, T, G, J, 3), (0, 2, 4, 3, 1))
    n_blk = T // CMP_BLOCK
    kern = functools.partial(_nsa_prompt_kernel, seq=T)
    return pl.pallas_call(
        kern,
        grid=(b, G, T // TQ),
        in_specs=[
            pl.BlockSpec((1, TQ, J * D), lambda bi, g, i: (bi, i, g)),
            pl.BlockSpec((1, T, 2 * D), lambda bi, g, i: (bi, 0, g)),
            pl.BlockSpec((1, 1, T, D), lambda bi, g, i: (bi, g, 0, 0)),
            pl.BlockSpec((1, 1, D, T), lambda bi, g, i: (bi, g, 0, 0)),
            pl.BlockSpec((1, 1, T, D), lambda bi, g, i: (bi, g, 0, 0)),
            pl.BlockSpec((1, 1, D, T), lambda bi, g, i: (bi, g, 0, 0)),
            pl.BlockSpec((1, 1, 3, J, TQ), lambda bi, g, i: (bi, g, 0, 0, i)),
        ],
        out_specs=pl.BlockSpec((1, TQ, J * D), lambda bi, g, i: (bi, i, g)),
        out_shape=jax.ShapeDtypeStruct((b, T, G * J * D), jnp.float32),
        scratch_shapes=[
            pltpu.VMEM((n_blk, D), bf16),
            pltpu.VMEM((D, n_blk), bf16),
            pltpu.VMEM((1, J * TQ), jnp.float32),
            pltpu.VMEM((1, J * TQ), jnp.float32),
            pltpu.VMEM((D, J * TQ), jnp.float32),
        ],
        compiler_params=pltpu.CompilerParams(
            dimension_semantics=("arbitrary", "arbitrary", "arbitrary"),
            vmem_limit_bytes=48 * 1024 * 1024),
        name="nsa_prompt",
    )(q, kv_c, ks, vst, kw, vwt, gates_t)


def _nsa_prompt(q, kv_c, kv_s, kv_w, nsa_g):
    b, T = q.shape[:2]
    win_new = kv_w.reshape(b, T, KV_HEADS, 2, ATTN_HEAD_DIM)[:, -min(WINDOW, T):]
    return _nsa_prompt_attention(q, kv_c, kv_s, kv_w, nsa_g), win_new


def _gather_paged_rows(pool, page_table, new_rows, pos):
    past = page_table.shape[1] * PAGE_SIZE
    b = page_table.shape[0]
    bidx = jnp.arange(b)[:, None, None, None]
    gidx = jnp.arange(KV_HEADS)[None, None, :, None]
    past_pos = jnp.minimum(pos, past - 1)
    phys = page_table[bidx, past_pos // PAGE_SIZE]
    from_pool = pool[phys, past_pos % PAGE_SIZE, gidx]
    new_pos = jnp.clip(pos - past, 0, new_rows.shape[1] - 1)
    from_new = new_rows[bidx, new_pos, gidx]
    return jnp.where((pos < past)[..., None, None], from_pool.astype(new_rows.dtype), from_new)


def _nsa_sample(pool_cmp, pool_sel, win_buf, page_table, q, kv_cmp, kv_sel, kv_win, nsa_g):
    b, T = q.shape[:2]
    q = q.reshape(b, T, KV_HEADS, HEADS_PER_KV, ATTN_HEAD_DIM)
    kv_cmp = kv_cmp.reshape(b, T, KV_HEADS, 2, ATTN_HEAD_DIM)
    kv_sel = kv_sel.reshape(b, T, KV_HEADS, 2, ATTN_HEAD_DIM)
    kv_win = kv_win.reshape(b, T, KV_HEADS, 2, ATTN_HEAD_DIM)
    gates = jax.nn.sigmoid(nsa_g.astype(jnp.float32)).reshape(b, T, KV_HEADS, HEADS_PER_KV, 3)
    n_pages = page_table.shape[1]
    past = n_pages * PAGE_SIZE
    q_pos = past + jnp.arange(T)
    past_cmp = pool_cmp[page_table].reshape(b, past // CMP_BLOCK, CMP_BLOCK, KV_HEADS, 2, ATTN_HEAD_DIM).mean(axis=2)
    n_new = T // CMP_BLOCK
    new_cmp = kv_cmp[:, :n_new * CMP_BLOCK].reshape(b, n_new, CMP_BLOCK, KV_HEADS, 2, ATTN_HEAD_DIM).mean(axis=2)
    cmp_kv = jnp.concatenate([past_cmp.astype(kv_cmp.dtype), new_cmp], axis=1)
    o_cmp, imp = _cmp_attend(q, cmp_kv, q_pos)
    n_sel = -(-(past + T) // SEL_BLOCK)
    idx = _select_blocks(imp, q_pos, n_sel)
    pos = _block_positions(idx)
    o_sel = _sel_attend(q, _gather_paged_rows(pool_sel, page_table, kv_sel, pos), pos, q_pos)
    w_eff = win_buf.shape[1]
    keys = jnp.concatenate([win_buf.astype(kv_win.dtype), kv_win], axis=1)
    o_win = _band_attend(q, keys, q_pos, past - w_eff + jnp.arange(w_eff + T))
    return _combine(gates, o_cmp, o_sel, o_win).reshape(b, T, ATTN_WIDTH), keys[:, -w_eff:]


PEER_PICKS = PEER_HEADS * PEER_TOPK
PEER_TT = 128
PEER_GT = 64
ROW_SUBLANES = D_MODEL // 128


def _top16_cols(s, n_rows, iota_rows):
    vals, rows = [], []
    for _ in range(PEER_TOPK):
        mx = jnp.max(s, axis=0, keepdims=True)
        first = jnp.min(jnp.where(s == mx, iota_rows, n_rows), axis=0, keepdims=True)
        vals.append(mx)
        rows.append(first)
        s = jnp.where(iota_rows == first, -jnp.inf, s)
    return jnp.concatenate(vals, axis=0), jnp.concatenate(rows, axis=0)


def _peer_route_kernel(h_ref, wq_ref, keys_ref, exp_ref, gate_ref):
    f32 = jnp.float32
    TT = PEER_TT
    q = jnp.dot(h_ref[...].astype(jnp.bfloat16), wq_ref[...], preferred_element_type=f32)
    iota_k = lax.broadcasted_iota(jnp.int32, (PEER_KEYS, TT), 0)
    iota_c = lax.broadcasted_iota(jnp.int32, (PEER_TOPK * PEER_TOPK, TT), 0)
    exp_rows, gate_rows = [], []
    for hd in range(PEER_HEADS):
        tops = []
        for c in range(2):
            j = hd * 2 + c
            qj = q[:, j * PEER_HALF:(j + 1) * PEER_HALF].astype(jnp.bfloat16)
            st = lax.dot_general(keys_ref[j], qj, (((1,), (1,)), ((), ())), preferred_element_type=f32)
            tops.append(_top16_cols(st, PEER_KEYS, iota_k))
        (s0, i0), (s1, i1) = tops
        cand_s = jnp.concatenate([s0[a:a + 1, :] + s1 for a in range(PEER_TOPK)], axis=0)
        cand_i = jnp.concatenate([i0[a:a + 1, :] * PEER_KEYS + i1 for a in range(PEER_TOPK)], axis=0)
        best_s, best_e = [], []
        s = cand_s
        for _ in range(PEER_TOPK):
            mx = jnp.max(s, axis=0, keepdims=True)
            first = jnp.min(jnp.where(s == mx, iota_c, PEER_TOPK * PEER_TOPK), axis=0, keepdims=True)
            pick = iota_c == first
            best_s.append(mx)
            best_e.append(jnp.sum(jnp.where(pick, cand_i, 0), axis=0, keepdims=True))
            s = jnp.where(pick, -jnp.inf, s)
        bs = jnp.concatenate(best_s, axis=0)
        be = jnp.concatenate(best_e, axis=0)
        ex = jnp.exp(bs - bs[0:1, :])
        gate_rows.append(ex / jnp.sum(ex, axis=0, keepdims=True))
        exp_rows.append(be)
    exp_ref[...] = _row_code(jnp.concatenate(exp_rows, axis=0)).T
    gate_ref[...] = jnp.concatenate(gate_rows, axis=0).T


def _peer_route(h, wq_b, keys_b):
    n = h.shape[0]
    TT = PEER_TT
    return pl.pallas_call(
        _peer_route_kernel,
        grid=(n // TT,),
        in_specs=[pl.BlockSpec((TT, D_MODEL), lambda i: (i, 0)),
                  pl.BlockSpec((D_MODEL, PEER_HEADS * PEER_QDIM), lambda i: (0, 0)),
                  pl.BlockSpec((PEER_HEADS * 2, PEER_KEYS, PEER_HALF), lambda i: (0, 0, 0))],
        out_specs=[pl.BlockSpec((TT, PEER_PICKS), lambda i: (i, 0)),
                   pl.BlockSpec((TT, PEER_PICKS), lambda i: (i, 0))],
        out_shape=[jax.ShapeDtypeStruct((n, PEER_PICKS), jnp.int32),
                   jax.ShapeDtypeStruct((n, PEER_PICKS), jnp.float32)],
        compiler_params=pltpu.CompilerParams(dimension_semantics=("arbitrary",),
                                             vmem_limit_bytes=40 * 1024 * 1024),
        name="peer_route",
    )(h, wq_b, keys_b)


def _pack_rows(w):
    b = lax.bitcast_convert_type(w.astype(jnp.bfloat16), jnp.uint16).astype(jnp.uint32)
    b = b.reshape(N_EXPERTS // 2, 2, D_MODEL)
    return ((b[:, 0] << 16) | b[:, 1]).reshape(N_EXPERTS // 2 * ROW_SUBLANES, 128)


def _row_code(e):
    return (e >> 1) * ROW_SUBLANES + (e & 1)


def _load_row(tab_ref, code):
    start = pl.multiple_of(code & ~(ROW_SUBLANES - 1), ROW_SUBLANES)
    w = tab_ref[pl.ds(start, ROW_SUBLANES), :]
    sh = (jnp.full((ROW_SUBLANES, 128), code, jnp.uint32) & jnp.uint32(1)) << jnp.uint32(4)
    w = (w << sh) & jnp.uint32(0xFFFF0000)
    return lax.bitcast_convert_type(w, jnp.float32)


PEER_GROUP = 8


def _peer_up_kernel(code_ref, h_ref, gate_ref, tab_ref, coef_ref, dots_ref):
    GT = PEER_GT

    def token(t, carry):
        h = h_ref[pl.ds(pl.multiple_of(t * ROW_SUBLANES, ROW_SUBLANES), ROW_SUBLANES), :]

        def group(g, tiles):
            rows = []
            base = t * PEER_PICKS + g * PEER_GROUP
            for kk in range(PEER_GROUP):
                rows.append(jnp.sum(_load_row(tab_ref, code_ref[base + kk]) * h, axis=0, keepdims=True))
            dots_ref[pl.ds(pl.multiple_of(t * PEER_PICKS + g * PEER_GROUP, PEER_GROUP), PEER_GROUP), :] = (
                jnp.concatenate(rows, axis=0))
            return tiles

        lax.fori_loop(0, PEER_PICKS // PEER_GROUP, group, 0)
        return carry

    lax.fori_loop(0, GT, token, 0)
    rows = [jnp.sum(dots_ref[t * PEER_PICKS:(t + 1) * PEER_PICKS, :].T, axis=0, keepdims=True) for t in range(GT)]
    coef_ref[...] = gate_ref[...] * jax.nn.gelu(jnp.concatenate(rows, axis=0))


def _peer_down_kernel(code_ref, coef_ref, tab_ref, o_ref):
    GT = PEER_GT

    def token(t, carry):
        def group(g, accs):
            accs = list(accs)
            base = t * PEER_PICKS + g * PEER_GROUP
            for kk in range(PEER_GROUP):
                accs[kk % 4] = accs[kk % 4] + _load_row(tab_ref, code_ref[base + kk]) * coef_ref[base + kk]
            return tuple(accs)

        zero = jnp.zeros((ROW_SUBLANES, 128), jnp.float32)
        accs = lax.fori_loop(0, PEER_PICKS // PEER_GROUP, group, (zero, zero, zero, zero))
        o_ref[pl.ds(pl.multiple_of(t * ROW_SUBLANES, ROW_SUBLANES), ROW_SUBLANES), :] = (accs[0] + accs[1]) + (accs[2] + accs[3])
        return carry

    lax.fori_loop(0, GT, token, 0)


def _peer_gather(h, experts, gates, u_tab, v_tab):
    n = h.shape[0]
    GT = PEER_GT
    tab_rows = N_EXPERTS // 2 * ROW_SUBLANES
    flat_smem = pl.BlockSpec((GT * PEER_PICKS,), lambda i: (i,), memory_space=pltpu.MemorySpace.SMEM)
    tab_spec = pl.BlockSpec((tab_rows, 128), lambda i: (0, 0), pipeline_mode=pl.Buffered(1))
    params = pltpu.CompilerParams(dimension_semantics=("arbitrary",), vmem_limit_bytes=56 * 1024 * 1024)
    codes = experts.reshape(n * PEER_PICKS)
    coef = pl.pallas_call(
        _peer_up_kernel,
        grid=(n // GT,),
        in_specs=[flat_smem,
                  pl.BlockSpec((GT * ROW_SUBLANES, 128), lambda i: (i, 0)),
                  pl.BlockSpec((GT, PEER_PICKS), lambda i: (i, 0)),
                  tab_spec],
        out_specs=pl.BlockSpec((GT, PEER_PICKS), lambda i: (i, 0)),
        out_shape=jax.ShapeDtypeStruct((n, PEER_PICKS), jnp.float32),
        scratch_shapes=[pltpu.VMEM((GT * PEER_PICKS, 128), jnp.float32)],
        compiler_params=params,
        name="peer_up",
    )(codes, h.reshape(n * ROW_SUBLANES, 128), gates, u_tab)
    out = pl.pallas_call(
        _peer_down_kernel,
        grid=(n // GT,),
        in_specs=[flat_smem, flat_smem, tab_spec],
        out_specs=pl.BlockSpec((GT * ROW_SUBLANES, 128), lambda i: (i, 0)),
        out_shape=jax.ShapeDtypeStruct((n * ROW_SUBLANES, 128), jnp.float32),
        compiler_params=params,
        name="peer_down",
    )(codes, coef.reshape(n * PEER_PICKS), v_tab)
    return out.reshape(n, D_MODEL)


def _peer_tokens(h, peer_wq, peer_sub_keys, u_tab, v_tab):
    shape = h.shape
    hf = h.reshape(-1, D_MODEL)
    wq_b = peer_wq.astype(jnp.bfloat16)
    keys_b = peer_sub_keys.reshape(PEER_HEADS * 2, PEER_KEYS, PEER_HALF).astype(jnp.bfloat16)
    experts, gates = _peer_route(hf, wq_b, keys_b)
    return _peer_gather(hf, experts, gates, u_tab, v_tab).reshape(shape)


def _block(x, conv_prev, ssm_prev, nsa_core, norm_mix_w, w_in, conv_w, conv_b, dt_bias, a_log, d_skip, ssm_norm_w,
           w_ssm_branch, w_attn_branch, w_out, norm_ffn_w, peer_wq, peer_sub_keys, u_tab, v_tab):
    b, T, _ = x.shape
    h = _rmsnorm(x, norm_mix_w)
    z, xbc, dt_raw, q, kv_c, kv_s, kv_w, nsa_g, br_g = _split_proj(h @ w_in)
    y_ssm, conv_new, ssm_new = _mamba_branch(z, xbc, dt_raw, conv_prev, ssm_prev, conv_w, conv_b, dt_bias, a_log,
                                             d_skip, ssm_norm_w)
    o_attn, win_new = nsa_core(q, kv_c, kv_s, kv_w, nsa_g)
    y_attn = o_attn @ w_attn_branch
    g = jax.nn.sigmoid(br_g.astype(jnp.float32))
    merged = (g[..., :D_MODEL] * (y_ssm @ w_ssm_branch).astype(jnp.float32)
              + g[..., D_MODEL:] * y_attn.astype(jnp.float32)).astype(x.dtype)
    x = x + merged @ w_out
    x = x + _peer_tokens(_rmsnorm(x, norm_ffn_w), peer_wq, peer_sub_keys, u_tab, v_tab)
    kv_shape = (b, T, KV_HEADS, 2, ATTN_HEAD_DIM)
    return x, kv_c.reshape(kv_shape), kv_s.reshape(kv_shape), win_new, conv_new, ssm_new


def kernel(x_prompt, x_sample, cache_cmp_kv, cache_sel_kv, cache_win_kv, state_ssm, state_conv, page_table,
           norm_mix_w, w_in, conv_w, conv_b, dt_bias, a_log, d_skip, ssm_norm_w, w_ssm_branch, w_attn_branch,
           w_out, norm_ffn_w, peer_wq, peer_sub_keys, peer_u, peer_v, final_norm_w):
    xp, xs = x_prompt, x_sample
    cmp_p, cmp_s, sel_p, sel_s, win_p, win_s = [], [], [], [], [], []
    ssm_p, ssm_s, conv_p, conv_s = [], [], [], []
    for layer in range(DEPTH):
        lw = [a[layer] for a in (norm_mix_w, w_in, conv_w, conv_b, dt_bias, a_log, d_skip, ssm_norm_w, w_ssm_branch,
                                 w_attn_branch, w_out, norm_ffn_w, peer_wq, peer_sub_keys)]
        lw += [_pack_rows(peer_u[layer]), _pack_rows(peer_v[layer])]
        bp = xp.shape[0]
        conv0 = jnp.zeros((bp, CONV_WIDTH - 1, CONV_DIM), xp.dtype)
        ssm0 = jnp.zeros((bp, SSM_HEADS, SSM_HEAD_DIM, SSM_STATE), xp.dtype)
        xp, kc, ks, kw, cv, ss = _block(xp, conv0, ssm0, _nsa_prompt, *lw)
        cmp_p.append(kc); sel_p.append(ks); win_p.append(kw); conv_p.append(cv); ssm_p.append(ss)
        core = functools.partial(_nsa_sample, cache_cmp_kv[layer], cache_sel_kv[layer], cache_win_kv[layer], page_table)
        xs, kc, ks, kw, cv, ss = _block(xs, state_conv[layer], state_ssm[layer], core, *lw)
        cmp_s.append(kc); sel_s.append(ks); win_s.append(kw); conv_s.append(cv); ssm_s.append(ss)
    y_prompt = _final_norm(xp, final_norm_w)
    y_sample = _final_norm(xs, final_norm_w)
    return (y_prompt, y_sample, jnp.stack(cmp_p), jnp.stack(cmp_s), jnp.stack(sel_p), jnp.stack(sel_s),
            jnp.stack(win_p), jnp.stack(win_s), jnp.stack(ssm_p), jnp.stack(ssm_s), jnp.stack(conv_p), jnp.stack(conv_s))
```

```python
import math, functools
import jax, jax.numpy as jnp
from jax import lax
import numpy as np
from jax.experimental import pallas as pl
from jax.experimental.pallas import tpu as pltpu

D_MODEL = 1024
BATCH = 8
SEQ = 4096
DEPTH = 1
DEC_BATCH = 32
DEC_SEQ = 8
PAST_LEN = 16384
PAGE_SIZE = 128

SSM_EXPAND = 2
D_INNER = SSM_EXPAND * D_MODEL
SSM_HEAD_DIM = 64
SSM_HEADS = D_INNER // SSM_HEAD_DIM
SSM_GROUPS = 4
SSM_HPG = SSM_HEADS // SSM_GROUPS
SSM_STATE = 128
CONV_WIDTH = 4
CONV_DIM = D_INNER + 2 * SSM_GROUPS * SSM_STATE
SSD_CHUNK = 128
ATTN_HEADS = 16
ATTN_HEAD_DIM = 64
ATTN_WIDTH = ATTN_HEADS * ATTN_HEAD_DIM
KV_HEADS = 2
HEADS_PER_KV = ATTN_HEADS // KV_HEADS
KV_WIDTH = 2 * KV_HEADS * ATTN_HEAD_DIM
CMP_BLOCK = 64
SEL_BLOCK = CMP_BLOCK
SEL_TOPK = 16
SEL_BONUS = 1000.0
WINDOW = 512
WIN_QBLOCK = 128
SEL_QBLOCK = 32
ATTN_SCALE = ATTN_HEAD_DIM ** -0.5
PEER_HEADS = 8
PEER_KEYS = 128
N_EXPERTS = PEER_KEYS * PEER_KEYS
PEER_QDIM = 256
PEER_HALF = PEER_QDIM // 2
PEER_TOPK = 16
PEER_TOKEN_CHUNK = 128
EPS = 1e-6
PROJ_SIZES = (D_INNER, CONV_DIM, SSM_HEADS, ATTN_WIDTH, KV_WIDTH, KV_WIDTH, KV_WIDTH, 3 * ATTN_HEADS, 2 * D_MODEL)
PROJ_DIM = sum(PROJ_SIZES)


def _rmsnorm(x, w):
    xf = x.astype(jnp.float32)
    y = xf * lax.rsqrt(jnp.mean(xf * xf, axis=-1, keepdims=True) + EPS)
    return (y * w.astype(jnp.float32)).astype(x.dtype)


PROJ_TM = 512
PROJ_VMEM_BYTES = 56 * 1024 * 1024


def _rms(x, w):
    return x * lax.rsqrt(jnp.mean(x * x, axis=-1, keepdims=True) + EPS) * w


def _resident(shape):
    return pl.BlockSpec(shape, lambda i: (0,) * len(shape), pipeline_mode=pl.Buffered(1))


def _norm_proj_kernel(x_ref, nw_ref, *refs):
    n_out = len(refs) // 2
    h = _rms(x_ref[...], nw_ref[...]).astype(jnp.bfloat16)
    for w_ref, o_ref in zip(refs[:n_out], refs[n_out:]):
        o_ref[...] = jnp.dot(h, w_ref[...], preferred_element_type=jnp.float32)


def _norm_proj(x, norm_w, weights, name):
    n = x.shape[0]
    tm = min(PROJ_TM, n)
    return pl.pallas_call(
        _norm_proj_kernel,
        grid=(n // tm,),
        in_specs=[pl.BlockSpec((tm, D_MODEL), lambda i: (i, 0)), _resident((1, D_MODEL))]
                 + [_resident(w.shape) for w in weights],
        out_specs=[pl.BlockSpec((tm, w.shape[1]), lambda i: (i, 0)) for w in weights],
        out_shape=[jax.ShapeDtypeStruct((n, w.shape[1]), jnp.float32) for w in weights],
        compiler_params=pltpu.CompilerParams(dimension_semantics=("arbitrary",), vmem_limit_bytes=PROJ_VMEM_BYTES),
        name=name,
    )(x, norm_w.reshape(1, D_MODEL), *weights)


def _merge_proj_kernel(x_ref, oa_ref, ys_ref, g_ref, nw_ref, wa_ref, ws_ref, wo_ref, x2_ref, h2_ref):
    f32, bf16 = jnp.float32, jnp.bfloat16
    y_attn = jnp.dot(oa_ref[...].astype(bf16), wa_ref[...], preferred_element_type=f32)
    y_ssm = jnp.dot(ys_ref[...].astype(bf16), ws_ref[...], preferred_element_type=f32)
    g = jax.nn.sigmoid(g_ref[...])
    merged = g[:, :D_MODEL] * y_ssm + g[:, D_MODEL:] * y_attn
    x2 = x_ref[...] + jnp.dot(merged.astype(bf16), wo_ref[...], preferred_element_type=f32)
    x2_ref[...] = x2
    h2_ref[...] = _rms(x2, nw_ref[...])


def _merge_proj(x, o_attn, y_ssm, br_g, norm_w, wa_b, ws_b, wo_b):
    n = x.shape[0]
    tm = min(PROJ_TM, n)
    rows = lambda c: pl.BlockSpec((tm, c), lambda i: (i, 0))
    return pl.pallas_call(
        _merge_proj_kernel,
        grid=(n // tm,),
        in_specs=[rows(D_MODEL), rows(ATTN_WIDTH), rows(D_INNER), rows(2 * D_MODEL), _resident((1, D_MODEL)),
                  _resident(wa_b.shape), _resident(ws_b.shape), _resident(wo_b.shape)],
        out_specs=[rows(D_MODEL), rows(D_MODEL)],
        out_shape=[jax.ShapeDtypeStruct((n, D_MODEL), jnp.float32)] * 2,
        compiler_params=pltpu.CompilerParams(dimension_semantics=("arbitrary",), vmem_limit_bytes=PROJ_VMEM_BYTES),
        name="merge_proj",
    )(x, o_attn, y_ssm, br_g, norm_w.reshape(1, D_MODEL), wa_b, ws_b, wo_b)


def _add_norm_kernel(x_ref, y_ref, w_ref, o_ref):
    o_ref[...] = _rms(x_ref[...] + y_ref[...], w_ref[...])


def _add_norm(x, y, w):
    n = x.shape[0]
    tm = min(PROJ_TM, n)
    rows = pl.BlockSpec((tm, D_MODEL), lambda i: (i, 0))
    return pl.pallas_call(
        _add_norm_kernel,
        grid=(n // tm,),
        in_specs=[rows, rows, _resident((1, D_MODEL))],
        out_specs=rows,
        out_shape=jax.ShapeDtypeStruct((n, D_MODEL), jnp.float32),
        name="add_norm",
    )(x, y, w.reshape(1, D_MODEL))


def _masked_softmax(s, mask):
    s = jnp.where(mask, s.astype(jnp.float32), -jnp.inf)
    m = jnp.max(s, axis=-1, keepdims=True)
    m = jnp.where(jnp.isfinite(m), m, 0.0)
    p = jnp.exp(s - m)
    return p / jnp.maximum(jnp.sum(p, axis=-1, keepdims=True), 1e-30)


def _split_proj(p):
    idx = np.cumsum(PROJ_SIZES)[:-1].tolist()
    return jnp.split(p, idx, axis=-1)


def _to_blocks(a, blk):
    b, T = a.shape[:2]
    return jnp.moveaxis(a.reshape(b, T // blk, blk, *a.shape[2:]), 1, 0)


def _from_blocks(a):
    nb, b, blk = a.shape[:3]
    return jnp.moveaxis(a, 0, 1).reshape(b, nb * blk, *a.shape[3:])


SMALL_WIDTH = 128
CONV_TAIL = 8


def _ssd_kernel(z_ref, xbc_ref, sm_ref, cprev_ref, h0_ref, cw_ref, cbias_ref, dtb_ref, alog_ref, dskip_ref, nw_ref,
                y_ref, hout_ref, state_ref, tail_ref, *, valid):
    f32, bf16 = jnp.float32, jnp.bfloat16
    L, P, N = SSD_CHUNK, SSM_HEAD_DIM, SSM_STATE
    c = pl.program_id(1)

    @pl.when(c == 0)
    def _init():
        state_ref[...] = h0_ref[0]
        tail_ref[...] = cprev_ref[0]

    xbc = xbc_ref[0]
    padded = jnp.concatenate([tail_ref[...], xbc], axis=0)
    acc = jnp.broadcast_to(cbias_ref[...], (L, CONV_DIM))
    for w in range(CONV_WIDTH):
        lo = CONV_TAIL - (CONV_WIDTH - 1) + w
        acc = acc + cw_ref[w:w + 1, :] * padded[lo:lo + L, :]
    xc = acc * jax.nn.sigmoid(acc)
    tail_ref[...] = xbc[L - CONV_TAIL:, :]

    row = lax.broadcasted_iota(jnp.int32, (L, 1), 0)
    live = row < (valid - c * L)
    dt = jnp.where(live, jax.nn.softplus(sm_ref[0] + dtb_ref[...]), 0.0)
    a = dt * (-jnp.exp(alog_ref[...]))
    tril = (row >= lax.broadcasted_iota(jnp.int32, (L, L), 1))
    a_cum = jnp.dot(tril.astype(f32), a, preferred_element_type=f32, precision=lax.Precision.HIGHEST)
    a_cum_t = a_cum.T
    a_last = a_cum[L - 1:L, :]
    grow = jnp.exp(a_cum)
    to_end = jnp.exp(a_last - a_cum)
    carry = jnp.exp(a_last)

    xs = xc[:, :D_INNER]
    ys = []
    for g in range(SSM_GROUPS):
        bg = xc[:, D_INNER + g * N:D_INNER + (g + 1) * N]
        cg = xc[:, D_INNER + SSM_GROUPS * N + g * N:D_INNER + SSM_GROUPS * N + (g + 1) * N]
        bg_b, cg_b = bg.astype(bf16), cg.astype(bf16)
        bg_t = bg.T.astype(bf16)
        cb = lax.dot_general(cg_b, bg_b, (((1,), (1,)), ((), ())), preferred_element_type=f32)
        for j in range(SSM_HPG):
            h = g * SSM_HPG + j
            diff = a_cum[:, h:h + 1] - a_cum_t[h:h + 1, :]
            m = (cb * jnp.exp(jnp.where(tril, diff, -jnp.inf))).astype(bf16)
            x_h = xs[:, h * P:(h + 1) * P]
            xdt = x_h * dt[:, h:h + 1]
            st = state_ref[h]
            y = jnp.dot(m, xdt.astype(bf16), preferred_element_type=f32)
            y = y + jnp.dot(cg_b, st.astype(bf16), preferred_element_type=f32) * grow[:, h:h + 1]
            state_ref[h] = st * carry[:, h:h + 1] + jnp.dot(bg_t, (xdt * to_end[:, h:h + 1]).astype(bf16),
                                                             preferred_element_type=f32)
            ys.append(y + dskip_ref[:, h:h + 1] * x_h)
    z = z_ref[0]
    y_ref[0] = _rms(jnp.concatenate(ys, axis=1) * (z * jax.nn.sigmoid(z)), nw_ref[...])

    @pl.when(c == pl.num_programs(1) - 1)
    def _emit():
        hout_ref[0] = state_ref[...]


def _mamba_branch(z, xbc, small, conv_prev, ssm_prev, conv_w, conv_b, dt_bias, a_log, d_skip, ssm_norm_w):
    b, L, _ = xbc.shape
    Lc = SSD_CHUNK
    Lp = -(-L // Lc) * Lc
    conv_new = jnp.concatenate([conv_prev, xbc], axis=1)[:, -(CONV_WIDTH - 1):]
    if Lp != L:
        padrows = lambda a: jnp.pad(a, ((0, 0), (0, Lp - L), (0, 0)))
        z, xbc, small = padrows(z), padrows(xbc), padrows(small)
    cprev = jnp.pad(conv_prev, ((0, 0), (CONV_TAIL - (CONV_WIDTH - 1), 0), (0, 0)))
    h0 = jnp.swapaxes(ssm_prev, 2, 3)
    lane = lambda v: jnp.pad(v.reshape(1, SSM_HEADS), ((0, 0), (0, SMALL_WIDTH - SSM_HEADS)))
    full = lambda shape: pl.BlockSpec(shape, lambda bi, c: (0,) * len(shape))
    kern = functools.partial(_ssd_kernel, valid=L)
    y, hout = pl.pallas_call(
        kern,
        grid=(b, Lp // Lc),
        in_specs=[pl.BlockSpec((1, Lc, D_INNER), lambda bi, c: (bi, c, 0)),
                  pl.BlockSpec((1, Lc, CONV_DIM), lambda bi, c: (bi, c, 0)),
                  pl.BlockSpec((1, Lc, SMALL_WIDTH), lambda bi, c: (bi, c, 0)),
                  pl.BlockSpec((1, CONV_TAIL, CONV_DIM), lambda bi, c: (bi, 0, 0)),
                  pl.BlockSpec((1, SSM_HEADS, SSM_STATE, SSM_HEAD_DIM), lambda bi, c: (bi, 0, 0, 0)),
                  full((CONV_WIDTH, CONV_DIM)), full((1, CONV_DIM)), full((1, SMALL_WIDTH)), full((1, SMALL_WIDTH)),
                  full((1, SMALL_WIDTH)), full((1, D_INNER))],
        out_specs=[pl.BlockSpec((1, Lc, D_INNER), lambda bi, c: (bi, c, 0)),
                   pl.BlockSpec((1, SSM_HEADS, SSM_STATE, SSM_HEAD_DIM), lambda bi, c: (bi, 0, 0, 0))],
        out_shape=[jax.ShapeDtypeStruct((b, Lp, D_INNER), jnp.float32),
                   jax.ShapeDtypeStruct((b, SSM_HEADS, SSM_STATE, SSM_HEAD_DIM), jnp.float32)],
        scratch_shapes=[pltpu.VMEM((SSM_HEADS, SSM_STATE, SSM_HEAD_DIM), jnp.float32),
                        pltpu.VMEM((CONV_TAIL, CONV_DIM), jnp.float32)],
        compiler_params=pltpu.CompilerParams(dimension_semantics=("arbitrary", "arbitrary"),
                                             vmem_limit_bytes=48 * 1024 * 1024),
        name="mamba_ssd",
    )(z, xbc, small, cprev, h0, conv_w, conv_b.reshape(1, CONV_DIM), lane(dt_bias), lane(a_log), lane(d_skip),
      ssm_norm_w.reshape(1, D_INNER))
    return y[:, :L], conv_new, jnp.swapaxes(hout, 2, 3)


def _cmp_attend(q, cmp_kv, q_pos):
    n = cmp_kv.shape[1]
    blk_end = jnp.arange(n) * CMP_BLOCK + (CMP_BLOCK - 1)
    s = jnp.einsum('btgjd,bngd->btgjn', q, cmp_kv[..., 0, :]).astype(jnp.float32) * ATTN_SCALE
    mask = (blk_end[None, :] <= q_pos[:, None])[None, :, None, None, :]
    p = _masked_softmax(s, mask)
    o = jnp.einsum('btgjn,bngd->btgjd', p.astype(cmp_kv.dtype), cmp_kv[..., 1, :])
    return o, jnp.sum(p, axis=3)


def _select_blocks(imp, q_pos, n_sel):
    imp = jnp.pad(imp, ((0, 0), (0, 0), (0, 0), (0, n_sel - imp.shape[-1])))
    j = jnp.arange(n_sel)[None, :]
    cur = (q_pos // SEL_BLOCK)[:, None]
    forced = ((j == 0) | (j == cur) | (j == cur - 1)).astype(jnp.float32)
    score = jnp.where((j <= cur)[None, :, None, :], imp + SEL_BONUS * forced[None, :, None, :], -jnp.inf)
    _, idx = lax.top_k(score, min(SEL_TOPK, n_sel))
    return idx


def _block_positions(idx):
    pos = idx[..., None] * SEL_BLOCK + jnp.arange(SEL_BLOCK)
    return pos.reshape(*idx.shape[:-1], -1)


def _sel_attend(q, rows, pos, q_pos):
    s = jnp.einsum('btgjd,btgsd->btgjs', q, rows[..., 0, :]).astype(jnp.float32) * ATTN_SCALE
    mask = (pos <= q_pos[None, :, None, None])[:, :, :, None, :]
    p = _masked_softmax(s, mask)
    return jnp.einsum('btgjs,btgsd->btgjd', p.astype(rows.dtype), rows[..., 1, :])


def _band_attend(q, kv, q_pos, k_pos):
    s = jnp.einsum('btgjd,bsgd->btgjs', q, kv[..., 0, :]).astype(jnp.float32) * ATTN_SCALE
    rel = q_pos[:, None] - k_pos[None, :]
    mask = ((rel >= 0) & (rel <= WINDOW) & (k_pos[None, :] >= 0))[None, :, None, None, :]
    p = _masked_softmax(s, mask)
    return jnp.einsum('btgjs,bsgd->btgjd', p.astype(kv.dtype), kv[..., 1, :])


def _combine(gates, o_cmp, o_sel, o_win):
    o = gates[..., 0:1] * o_cmp.astype(jnp.float32) + gates[..., 1:2] * o_sel.astype(jnp.float32) \
        + gates[..., 2:3] * o_win.astype(jnp.float32)
    return o.astype(o_cmp.dtype)


NSA_TQ = 128
NSA_TK = 512
NSA_WSPAN = WINDOW + NSA_TQ
NEG_BIG = -1e30


def _nsa_prompt_kernel(q_ref, kvc_ref, ks_ref, vst_ref, kw_ref, vwt_ref, gate_ref, o_ref,
                       pk_ref, pvt_ref, m_ref, l_ref, acc_ref, *, seq):
    f32, bf16 = jnp.float32, jnp.bfloat16
    D, J, TQ, TK = ATTN_HEAD_DIM, HEADS_PER_KV, NSA_TQ, NSA_TK
    n_blk = seq // CMP_BLOCK
    i = pl.program_id(2)
    t0 = i * TQ

    @pl.when(i == 0)
    def _pool():
        c = kvc_ref[0]
        pooled = jnp.sum(c.reshape(n_blk, CMP_BLOCK, 2 * D), axis=1) * (1.0 / CMP_BLOCK)
        pk_ref[...] = pooled[:, :D].astype(bf16)
        pvt_ref[...] = pooled[:, D:].T.astype(bf16)

    qt = (q_ref[0] * ATTN_SCALE).T
    qs = jnp.concatenate([qt[j * D:(j + 1) * D, :] for j in range(J)], axis=1).astype(bf16)
    qpos1 = t0 + lax.broadcasted_iota(jnp.int32, (1, TQ), 1)
    qpos = jnp.concatenate([qpos1] * J, axis=1)

    s = jnp.dot(pk_ref[...], qs, preferred_element_type=f32)
    blk = lax.broadcasted_iota(jnp.int32, (n_blk, 1), 0)
    vis = (blk * CMP_BLOCK + (CMP_BLOCK - 1)) <= qpos
    s = jnp.where(vis, s, -jnp.inf)
    m = jnp.max(s, axis=0, keepdims=True)
    m = jnp.where(m == -jnp.inf, 0.0, m)
    p = jnp.exp(s - m)
    p = p / jnp.maximum(jnp.sum(p, axis=0, keepdims=True), 1e-30)
    o_cmp = jnp.dot(pvt_ref[...], p.astype(bf16), preferred_element_type=f32)
    imp = p[:, 0:TQ]
    for j in range(1, J):
        imp = imp + p[:, j * TQ:(j + 1) * TQ]

    cur = qpos1 // CMP_BLOCK
    blk2 = lax.broadcasted_iota(jnp.int32, (n_blk, TQ), 0)
    forced = ((blk2 == 0) | (blk2 == cur) | (blk2 == cur - 1)).astype(f32)
    score = jnp.where(blk2 <= cur, imp + SEL_BONUS * forced, -jnp.inf)
    sel = jnp.zeros((n_blk, TQ), f32)
    for _ in range(min(SEL_TOPK, n_blk)):
        mx = jnp.max(score, axis=0, keepdims=True)
        first = jnp.min(jnp.where(score == mx, blk2, n_blk), axis=0, keepdims=True)
        pick = blk2 == first
        sel = jnp.where(pick, 1.0, sel)
        score = jnp.where(pick, -jnp.inf, score)
    sel_b = sel.astype(bf16)

    m_ref[...] = jnp.full(m_ref.shape, NEG_BIG, f32)
    l_ref[...] = jnp.zeros(l_ref.shape, f32)
    acc_ref[...] = jnp.zeros(acc_ref.shape, f32)
    n_kt = (t0 + TQ + TK - 1) // TK

    def sel_tile(kt, carry):
        k0 = pl.multiple_of(kt * TK, TK)
        kpos = k0 + lax.broadcasted_iota(jnp.int32, (TK, 1), 0)
        expand = ((kpos // CMP_BLOCK) == lax.broadcasted_iota(jnp.int32, (TK, n_blk), 1)).astype(bf16)
        chosen = jnp.dot(expand, sel_b, preferred_element_type=f32)
        ok = (chosen > 0.5) & (kpos <= qpos1)
        bias1 = jnp.where(ok, 0.0, NEG_BIG)
        bias = jnp.concatenate([bias1] * J, axis=1)
        st = jnp.dot(ks_ref[0, 0, pl.ds(k0, TK), :], qs, preferred_element_type=f32) + bias
        m_old = m_ref[...]
        m_new = jnp.maximum(m_old, jnp.max(st, axis=0, keepdims=True))
        alpha = jnp.exp(m_old - m_new)
        pt = jnp.exp(st - m_new)
        l_ref[...] = alpha * l_ref[...] + jnp.sum(pt, axis=0, keepdims=True)
        acc_ref[...] = alpha * acc_ref[...] + jnp.dot(vst_ref[0, 0, :, pl.ds(k0, TK)], pt.astype(bf16),
                                                      preferred_element_type=f32)
        m_ref[...] = m_new
        return carry

    lax.fori_loop(0, n_kt, sel_tile, 0)
    o_sel = acc_ref[...] / l_ref[...]

    w0 = pl.multiple_of(jnp.maximum(t0 - WINDOW, 0), TQ)
    span = min(NSA_WSPAN, seq)
    kposw = w0 + lax.broadcasted_iota(jnp.int32, (span, 1), 0)
    rel = qpos1 - kposw
    biasw1 = jnp.where((rel >= 0) & (rel <= WINDOW), 0.0, NEG_BIG)
    biasw = jnp.concatenate([biasw1] * J, axis=1)
    sw = jnp.dot(kw_ref[0, 0, pl.ds(w0, span), :], qs, preferred_element_type=f32) + biasw
    pw = jnp.exp(sw - jnp.max(sw, axis=0, keepdims=True))
    lw = jnp.sum(pw, axis=0, keepdims=True)
    o_win = jnp.dot(vwt_ref[0, 0, :, pl.ds(w0, span)], pw.astype(bf16), preferred_element_type=f32) / lw

    def gate_row(br):
        g = jax.nn.sigmoid(gate_ref[0, 0, br])
        return jnp.concatenate([g[j:j + 1, :] for j in range(J)], axis=1)

    out = gate_row(0) * o_cmp + gate_row(1) * o_sel + gate_row(2) * o_win
    out = jnp.concatenate([out[:, j * TQ:(j + 1) * TQ] for j in range(J)], axis=0)
    o_ref[0] = out.T


def _nsa_prompt_attention(q, kv_c, kv_s, kv_w, nsa_g):
    b, T, _ = q.shape
    G, J, D, TQ = KV_HEADS, HEADS_PER_KV, ATTN_HEAD_DIM, NSA_TQ
    bf16 = jnp.bfloat16

    def split_kv(kv):
        r = kv.reshape(b, T, G, 2, D).astype(bf16)
        return jnp.transpose(r[:, :, :, 0], (0, 2, 1, 3)), jnp.transpose(r[:, :, :, 1], (0, 2, 3, 1))

    ks, vst = split_kv(kv_s)
    kw, vwt = split_kv(kv_w)
    gates_t = jnp.transpose(nsa_g.reshape(b, T, G, J, 3), (0, 2, 4, 3, 1))
    n_blk = T // CMP_BLOCK
    kern = functools.partial(_nsa_prompt_kernel, seq=T)
    return pl.pallas_call(
        kern,
        grid=(b, G, T // TQ),
        in_specs=[
            pl.BlockSpec((1, TQ, J * D), lambda bi, g, i: (bi, i, g)),
            pl.BlockSpec((1, T, 2 * D), lambda bi, g, i: (bi, 0, g)),
            pl.BlockSpec((1, 1, T, D), lambda bi, g, i: (bi, g, 0, 0)),
            pl.BlockSpec((1, 1, D, T), lambda bi, g, i: (bi, g, 0, 0)),
            pl.BlockSpec((1, 1, T, D), lambda bi, g, i: (bi, g, 0, 0)),
            pl.BlockSpec((1, 1, D, T), lambda bi, g, i: (bi, g, 0, 0)),
            pl.BlockSpec((1, 1, 3, J, TQ), lambda bi, g, i: (bi, g, 0, 0, i)),
        ],
        out_specs=pl.BlockSpec((1, TQ, J * D), lambda bi, g, i: (bi, i, g)),
        out_shape=jax.ShapeDtypeStruct((b, T, G * J * D), jnp.float32),
        scratch_shapes=[
            pltpu.VMEM((n_blk, D), bf16),
            pltpu.VMEM((D, n_blk), bf16),
            pltpu.VMEM((1, J * TQ), jnp.float32),
            pltpu.VMEM((1, J * TQ), jnp.float32),
            pltpu.VMEM((D, J * TQ), jnp.float32),
        ],
        compiler_params=pltpu.CompilerParams(
            dimension_semantics=("arbitrary", "arbitrary", "arbitrary"),
            vmem_limit_bytes=48 * 1024 * 1024),
        name="nsa_prompt",
    )(q, kv_c, ks, vst, kw, vwt, gates_t)


def _nsa_prompt(q, kv_c, kv_s, kv_w, nsa_g):
    b, T = q.shape[:2]
    win_new = kv_w.reshape(b, T, KV_HEADS, 2, ATTN_HEAD_DIM)[:, -min(WINDOW, T):]
    return _nsa_prompt_attention(q, kv_c, kv_s, kv_w, nsa_g), win_new


def _gather_paged_rows(pool, page_table, new_rows, pos):
    past = page_table.shape[1] * PAGE_SIZE
    b = page_table.shape[0]
    bidx = jnp.arange(b)[:, None, None, None]
    gidx = jnp.arange(KV_HEADS)[None, None, :, None]
    past_pos = jnp.minimum(pos, past - 1)
    phys = page_table[bidx, past_pos // PAGE_SIZE]
    from_pool = pool[phys, past_pos % PAGE_SIZE, gidx]
    new_pos = jnp.clip(pos - past, 0, new_rows.shape[1] - 1)
    from_new = new_rows[bidx, new_pos, gidx]
    return jnp.where((pos < past)[..., None, None], from_pool.astype(new_rows.dtype), from_new)


def _nsa_sample(pool_cmp, pool_sel, win_buf, page_table, q, kv_cmp, kv_sel, kv_win, nsa_g):
    b, T = q.shape[:2]
    q = q.reshape(b, T, KV_HEADS, HEADS_PER_KV, ATTN_HEAD_DIM)
    kv_cmp = kv_cmp.reshape(b, T, KV_HEADS, 2, ATTN_HEAD_DIM)
    kv_sel = kv_sel.reshape(b, T, KV_HEADS, 2, ATTN_HEAD_DIM)
    kv_win = kv_win.reshape(b, T, KV_HEADS, 2, ATTN_HEAD_DIM)
    gates = jax.nn.sigmoid(nsa_g.astype(jnp.float32)).reshape(b, T, KV_HEADS, HEADS_PER_KV, 3)
    n_pages = page_table.shape[1]
    past = n_pages * PAGE_SIZE
    q_pos = past + jnp.arange(T)
    past_cmp = pool_cmp[page_table].reshape(b, past // CMP_BLOCK, CMP_BLOCK, KV_HEADS, 2, ATTN_HEAD_DIM).mean(axis=2)
    n_new = T // CMP_BLOCK
    new_cmp = kv_cmp[:, :n_new * CMP_BLOCK].reshape(b, n_new, CMP_BLOCK, KV_HEADS, 2, ATTN_HEAD_DIM).mean(axis=2)
    cmp_kv = jnp.concatenate([past_cmp.astype(kv_cmp.dtype), new_cmp], axis=1)
    o_cmp, imp = _cmp_attend(q, cmp_kv, q_pos)
    n_sel = -(-(past + T) // SEL_BLOCK)
    idx = _select_blocks(imp, q_pos, n_sel)
    pos = _block_positions(idx)
    o_sel = _sel_attend(q, _gather_paged_rows(pool_sel, page_table, kv_sel, pos), pos, q_pos)
    w_eff = win_buf.shape[1]
    keys = jnp.concatenate([win_buf.astype(kv_win.dtype), kv_win], axis=1)
    o_win = _band_attend(q, keys, q_pos, past - w_eff + jnp.arange(w_eff + T))
    return _combine(gates, o_cmp, o_sel, o_win).reshape(b, T, ATTN_WIDTH), keys[:, -w_eff:]


PEER_PICKS = PEER_HEADS * PEER_TOPK
PEER_TT = 128
PEER_GT = 64
ROW_SUBLANES = D_MODEL // 128


def _top16_cols(s, n_rows, iota_rows):
    vals, rows = [], []
    for _ in range(PEER_TOPK):
        mx = jnp.max(s, axis=0, keepdims=True)
        first = jnp.min(jnp.where(s == mx, iota_rows, n_rows), axis=0, keepdims=True)
        vals.append(mx)
        rows.append(first)
        s = jnp.where(iota_rows == first, -jnp.inf, s)
    return jnp.concatenate(vals, axis=0), jnp.concatenate(rows, axis=0)


def _peer_route_kernel(h_ref, wq_ref, keys_ref, exp_ref, gate_ref):
    f32 = jnp.float32
    TT = PEER_TT
    q = jnp.dot(h_ref[...].astype(jnp.bfloat16), wq_ref[...], preferred_element_type=f32)
    iota_k = lax.broadcasted_iota(jnp.int32, (PEER_KEYS, TT), 0)
    iota_c = lax.broadcasted_iota(jnp.int32, (PEER_TOPK * PEER_TOPK, TT), 0)
    exp_rows, gate_rows = [], []
    for hd in range(PEER_HEADS):
        tops = []
        for c in range(2):
            j = hd * 2 + c
            qj = q[:, j * PEER_HALF:(j + 1) * PEER_HALF].astype(jnp.bfloat16)
            st = lax.dot_general(keys_ref[j], qj, (((1,), (1,)), ((), ())), preferred_element_type=f32)
            tops.append(_top16_cols(st, PEER_KEYS, iota_k))
        (s0, i0), (s1, i1) = tops
        cand_s = jnp.concatenate([s0[a:a + 1, :] + s1 for a in range(PEER_TOPK)], axis=0)
        cand_i = jnp.concatenate([i0[a:a + 1, :] * PEER_KEYS + i1 for a in range(PEER_TOPK)], axis=0)
        best_s, best_e = [], []
        s = cand_s
        for _ in range(PEER_TOPK):
            mx = jnp.max(s, axis=0, keepdims=True)
            first = jnp.min(jnp.where(s == mx, iota_c, PEER_TOPK * PEER_TOPK), axis=0, keepdims=True)
            pick = iota_c == first
            best_s.append(mx)
            best_e.append(jnp.sum(jnp.where(pick, cand_i, 0), axis=0, keepdims=True))
            s = jnp.where(pick, -jnp.inf, s)
        bs = jnp.concatenate(best_s, axis=0)
        be = jnp.concatenate(best_e, axis=0)
        ex = jnp.exp(bs - bs[0:1, :])
        gate_rows.append(ex / jnp.sum(ex, axis=0, keepdims=True))
        exp_rows.append(be)
    exp_ref[...] = _row_code(jnp.concatenate(exp_rows, axis=0)).T
    gate_ref[...] = jnp.concatenate(gate_rows, axis=0).T


def _peer_route(h, wq_b, keys_b):
    n = h.shape[0]
    TT = PEER_TT
    return pl.pallas_call(
        _peer_route_kernel,
        grid=(n // TT,),
        in_specs=[pl.BlockSpec((TT, D_MODEL), lambda i: (i, 0)),
                  pl.BlockSpec((D_MODEL, PEER_HEADS * PEER_QDIM), lambda i: (0, 0)),
                  pl.BlockSpec((PEER_HEADS * 2, PEER_KEYS, PEER_HALF), lambda i: (0, 0, 0))],
        out_specs=[pl.BlockSpec((TT, PEER_PICKS), lambda i: (i, 0)),
                   pl.BlockSpec((TT, PEER_PICKS), lambda i: (i, 0))],
        out_shape=[jax.ShapeDtypeStruct((n, PEER_PICKS), jnp.int32),
                   jax.ShapeDtypeStruct((n, PEER_PICKS), jnp.float32)],
        compiler_params=pltpu.CompilerParams(dimension_semantics=("arbitrary",),
                                             vmem_limit_bytes=40 * 1024 * 1024),
        name="peer_route",
    )(h, wq_b, keys_b)


def _pack_rows(w):
    b = lax.bitcast_convert_type(w.astype(jnp.bfloat16), jnp.uint16).astype(jnp.uint32)
    b = b.reshape(N_EXPERTS // 2, 2, D_MODEL)
    return ((b[:, 0] << 16) | b[:, 1]).reshape(N_EXPERTS // 2 * ROW_SUBLANES, 128)


def _row_code(e):
    return (e >> 1) * ROW_SUBLANES + (e & 1)


def _load_row(tab_ref, code):
    start = pl.multiple_of(code & ~(ROW_SUBLANES - 1), ROW_SUBLANES)
    w = tab_ref[pl.ds(start, ROW_SUBLANES), :]
    sh = (jnp.full((ROW_SUBLANES, 128), code, jnp.uint32) & jnp.uint32(1)) << jnp.uint32(4)
    w = (w << sh) & jnp.uint32(0xFFFF0000)
    return lax.bitcast_convert_type(w, jnp.float32)


PEER_GROUP = 16


def _peer_up_kernel(code_ref, h_ref, gate_ref, tab_ref, coef_ref, dots_ref):
    GT = PEER_GT

    def token(t, carry):
        h = h_ref[pl.ds(pl.multiple_of(t * ROW_SUBLANES, ROW_SUBLANES), ROW_SUBLANES), :]

        def group(g, tiles):
            rows = []
            base = t * PEER_PICKS + g * PEER_GROUP
            for kk in range(PEER_GROUP):
                rows.append(jnp.sum(_load_row(tab_ref, code_ref[base + kk]) * h, axis=0, keepdims=True))
            dots_ref[pl.ds(pl.multiple_of(t * PEER_PICKS + g * PEER_GROUP, PEER_GROUP), PEER_GROUP), :] = (
                jnp.concatenate(rows, axis=0))
            return tiles

        lax.fori_loop(0, PEER_PICKS // PEER_GROUP, group, 0)
        return carry

    lax.fori_loop(0, GT, token, 0)
    rows = [jnp.sum(dots_ref[t * PEER_PICKS:(t + 1) * PEER_PICKS, :].T, axis=0, keepdims=True) for t in range(GT)]
    coef_ref[...] = gate_ref[...] * jax.nn.gelu(jnp.concatenate(rows, axis=0))


def _peer_down_kernel(code_ref, coef_ref, tab_ref, o_ref):
    GT = PEER_GT

    def token(t, carry):
        def group(g, accs):
            accs = list(accs)
            base = t * PEER_PICKS + g * PEER_GROUP
            for kk in range(PEER_GROUP):
                accs[kk % 4] = accs[kk % 4] + _load_row(tab_ref, code_ref[base + kk]) * coef_ref[base + kk]
            return tuple(accs)

        zero = jnp.zeros((ROW_SUBLANES, 128), jnp.float32)
        accs = lax.fori_loop(0, PEER_PICKS // PEER_GROUP, group, (zero, zero, zero, zero))
        o_ref[pl.ds(pl.multiple_of(t * ROW_SUBLANES, ROW_SUBLANES), ROW_SUBLANES), :] = (accs[0] + accs[1]) + (accs[2] + accs[3])
        return carry

    lax.fori_loop(0, GT, token, 0)


def _peer_gather(h, experts, gates, u_tab, v_tab):
    n = h.shape[0]
    GT = PEER_GT
    tab_rows = N_EXPERTS // 2 * ROW_SUBLANES
    flat_smem = pl.BlockSpec((GT * PEER_PICKS,), lambda i: (i,), memory_space=pltpu.MemorySpace.SMEM)
    tab_spec = pl.BlockSpec((tab_rows, 128), lambda i: (0, 0), pipeline_mode=pl.Buffered(1))
    params = pltpu.CompilerParams(dimension_semantics=("arbitrary",), vmem_limit_bytes=56 * 1024 * 1024)
    codes = experts.reshape(n * PEER_PICKS)
    coef = pl.pallas_call(
        _peer_up_kernel,
        grid=(n // GT,),
        in_specs=[flat_smem,
                  pl.BlockSpec((GT * ROW_SUBLANES, 128), lambda i: (i, 0)),
                  pl.BlockSpec((GT, PEER_PICKS), lambda i: (i, 0)),
                  tab_spec],
        out_specs=pl.BlockSpec((GT, PEER_PICKS), lambda i: (i, 0)),
        out_shape=jax.ShapeDtypeStruct((n, PEER_PICKS), jnp.float32),
        scratch_shapes=[pltpu.VMEM((GT * PEER_PICKS, 128), jnp.float32)],
        compiler_params=params,
        name="peer_up",
    )(codes, h.reshape(n * ROW_SUBLANES, 128), gates, u_tab)
    out = pl.pallas_call(
        _peer_down_kernel,
        grid=(n // GT,),
        in_specs=[flat_smem, flat_smem, tab_spec],
        out_specs=pl.BlockSpec((GT * ROW_SUBLANES, 128), lambda i: (i, 0)),
        out_shape=jax.ShapeDtypeStruct((n * ROW_SUBLANES, 128), jnp.float32),
        compiler_params=params,
        name="peer_down",
    )(codes, coef.reshape(n * PEER_PICKS), v_tab)
    return out.reshape(n, D_MODEL)


def _peer_tokens(h, peer_wq, peer_sub_keys, u_tab, v_tab):
    shape = h.shape
    hf = h.reshape(-1, D_MODEL)
    wq_b = peer_wq.astype(jnp.bfloat16)
    keys_b = peer_sub_keys.reshape(PEER_HEADS * 2, PEER_KEYS, PEER_HALF).astype(jnp.bfloat16)
    experts, gates = _peer_route(hf, wq_b, keys_b)
    return _peer_gather(hf, experts, gates, u_tab, v_tab).reshape(shape)


def _layer_weights(w_in, w_ssm_branch, w_attn_branch, w_out, peer_wq, peer_sub_keys, peer_u, peer_v):
    bf16 = jnp.bfloat16
    offs = np.cumsum((0,) + PROJ_SIZES).tolist()
    seg = lambda k: w_in[:, offs[k]:offs[k + 1]]
    pad = jnp.zeros((D_MODEL, SMALL_WIDTH - SSM_HEADS - 3 * ATTN_HEADS), w_in.dtype)
    return dict(
        in_ssm=[seg(0).astype(bf16), seg(1).astype(bf16)],
        in_attn=[seg(3).astype(bf16), jnp.concatenate([seg(4), seg(5), seg(6)], axis=1).astype(bf16),
                 seg(8).astype(bf16), jnp.concatenate([seg(2), seg(7), pad], axis=1).astype(bf16)],
        w_ssm=w_ssm_branch.astype(bf16), w_attn=w_attn_branch.astype(bf16), w_out=w_out.astype(bf16),
        wq=peer_wq.astype(bf16),
        keys=peer_sub_keys.reshape(PEER_HEADS * 2, PEER_KEYS, PEER_HALF).astype(bf16),
        u_tab=_pack_rows(peer_u), v_tab=_pack_rows(peer_v))


def _block(x, conv_prev, ssm_prev, nsa_core, lw, norm_mix_w, conv_w, conv_b, dt_bias, a_log, d_skip, ssm_norm_w,
           norm_ffn_w):
    b, T, _ = x.shape
    n = b * T
    xf = x.reshape(n, D_MODEL)
    z, xbc = _norm_proj(xf, norm_mix_w, lw["in_ssm"], "in_proj_ssm")
    q, kv, br_g, small = _norm_proj(xf, norm_mix_w, lw["in_attn"], "in_proj_attn")
    nsa_g = small[:, SSM_HEADS:SSM_HEADS + 3 * ATTN_HEADS]
    kv_c, kv_s, kv_w = (kv[:, k * KV_WIDTH:(k + 1) * KV_WIDTH].reshape(b, T, KV_WIDTH) for k in range(3))
    y_ssm, conv_new, ssm_new = _mamba_branch(z.reshape(b, T, D_INNER), xbc.reshape(b, T, CONV_DIM),
                                             small.reshape(b, T, SMALL_WIDTH), conv_prev, ssm_prev, conv_w, conv_b,
                                             dt_bias, a_log, d_skip, ssm_norm_w)
    o_attn, win_new = nsa_core(q.reshape(b, T, ATTN_WIDTH), kv_c, kv_s, kv_w, nsa_g.reshape(b, T, 3 * ATTN_HEADS))
    x2, h2 = _merge_proj(xf, o_attn.reshape(n, ATTN_WIDTH), y_ssm.reshape(n, D_INNER), br_g, norm_ffn_w,
                         lw["w_attn"], lw["w_ssm"], lw["w_out"])
    experts, gates = _peer_route(h2, lw["wq"], lw["keys"])
    peer = _peer_gather(h2, experts, gates, lw["u_tab"], lw["v_tab"])
    kv_shape = (b, T, KV_HEADS, 2, ATTN_HEAD_DIM)
    return x2, peer, kv_c.reshape(kv_shape), kv_s.reshape(kv_shape), win_new, conv_new, ssm_new


def kernel(x_prompt, x_sample, cache_cmp_kv, cache_sel_kv, cache_win_kv, state_ssm, state_conv, page_table,
           norm_mix_w, w_in, conv_w, conv_b, dt_bias, a_log, d_skip, ssm_norm_w, w_ssm_branch, w_attn_branch,
           w_out, norm_ffn_w, peer_wq, peer_sub_keys, peer_u, peer_v, final_norm_w):
    xp, xs = x_prompt, x_sample
    cmp_p, cmp_s, sel_p, sel_s, win_p, win_s = [], [], [], [], [], []
    ssm_p, ssm_s, conv_p, conv_s = [], [], [], []
    for layer in range(DEPTH):
        lw = _layer_weights(*(a[layer] for a in (w_in, w_ssm_branch, w_attn_branch, w_out, peer_wq, peer_sub_keys,
                                                   peer_u, peer_v)))
        rest = [a[layer] for a in (norm_mix_w, conv_w, conv_b, dt_bias, a_log, d_skip, ssm_norm_w, norm_ffn_w)]
        last = layer == DEPTH - 1
        bp = xp.shape[0]
        conv0 = jnp.zeros((bp, CONV_WIDTH - 1, CONV_DIM), xp.dtype)
        ssm0 = jnp.zeros((bp, SSM_HEADS, SSM_HEAD_DIM, SSM_STATE), xp.dtype)
        x2, peer, kc, ks, kw, cv, ss = _block(xp, conv0, ssm0, _nsa_prompt, lw, *rest)
        xp = (_add_norm(x2, peer, final_norm_w) if last else x2 + peer).reshape(xp.shape)
        cmp_p.append(kc); sel_p.append(ks); win_p.append(kw); conv_p.append(cv); ssm_p.append(ss)
        core = functools.partial(_nsa_sample, cache_cmp_kv[layer], cache_sel_kv[layer], cache_win_kv[layer], page_table)
        x2, peer, kc, ks, kw, cv, ss = _block(xs, state_conv[layer], state_ssm[layer], core, lw, *rest)
        xs = (_add_norm(x2, peer, final_norm_w) if last else x2 + peer).reshape(xs.shape)
        cmp_s.append(kc); sel_s.append(ks); win_s.append(kw); conv_s.append(cv); ssm_s.append(ss)
    return (xp, xs, jnp.stack(cmp_p), jnp.stack(cmp_s), jnp.stack(sel_p), jnp.stack(sel_s),
            jnp.stack(win_p), jnp.stack(win_s), jnp.stack(ssm_p), jnp.stack(ssm_s), jnp.stack(conv_p), jnp.stack(conv_s))
```

```python
import math, functools
import jax, jax.numpy as jnp
from jax import lax
import numpy as np
from jax.experimental import pallas as pl
from jax.experimental.pallas import tpu as pltpu

D_MODEL = 1024
BATCH = 8
SEQ = 4096
DEPTH = 1
DEC_BATCH = 32
DEC_SEQ = 8
PAST_LEN = 16384
PAGE_SIZE = 128

SSM_EXPAND = 2
D_INNER = SSM_EXPAND * D_MODEL
SSM_HEAD_DIM = 64
SSM_HEADS = D_INNER // SSM_HEAD_DIM
SSM_GROUPS = 4
SSM_HPG = SSM_HEADS // SSM_GROUPS
SSM_STATE = 128
CONV_WIDTH = 4
CONV_DIM = D_INNER + 2 * SSM_GROUPS * SSM_STATE
SSD_CHUNK = 128
ATTN_HEADS = 16
ATTN_HEAD_DIM = 64
ATTN_WIDTH = ATTN_HEADS * ATTN_HEAD_DIM
KV_HEADS = 2
HEADS_PER_KV = ATTN_HEADS // KV_HEADS
KV_WIDTH = 2 * KV_HEADS * ATTN_HEAD_DIM
CMP_BLOCK = 64
SEL_BLOCK = CMP_BLOCK
SEL_TOPK = 16
SEL_BONUS = 1000.0
WINDOW = 512
WIN_QBLOCK = 128
SEL_QBLOCK = 32
ATTN_SCALE = ATTN_HEAD_DIM ** -0.5
PEER_HEADS = 8
PEER_KEYS = 128
N_EXPERTS = PEER_KEYS * PEER_KEYS
PEER_QDIM = 256
PEER_HALF = PEER_QDIM // 2
PEER_TOPK = 16
PEER_TOKEN_CHUNK = 128
EPS = 1e-6
PROJ_SIZES = (D_INNER, CONV_DIM, SSM_HEADS, ATTN_WIDTH, KV_WIDTH, KV_WIDTH, KV_WIDTH, 3 * ATTN_HEADS, 2 * D_MODEL)
PROJ_DIM = sum(PROJ_SIZES)


def _rmsnorm(x, w):
    xf = x.astype(jnp.float32)
    y = xf * lax.rsqrt(jnp.mean(xf * xf, axis=-1, keepdims=True) + EPS)
    return (y * w.astype(jnp.float32)).astype(x.dtype)


PROJ_TM = 512
PROJ_VMEM_BYTES = 56 * 1024 * 1024


def _rms(x, w):
    return x * lax.rsqrt(jnp.mean(x * x, axis=-1, keepdims=True) + EPS) * w


def _resident(shape):
    return pl.BlockSpec(shape, lambda i: (0,) * len(shape), pipeline_mode=pl.Buffered(1))


def _norm_proj_kernel(x_ref, nw_ref, *refs):
    n_out = len(refs) // 2
    h = _rms(x_ref[...], nw_ref[...]).astype(jnp.bfloat16)
    for w_ref, o_ref in zip(refs[:n_out], refs[n_out:]):
        o_ref[...] = jnp.dot(h, w_ref[...], preferred_element_type=jnp.float32)


def _norm_proj(x, norm_w, weights, name):
    n = x.shape[0]
    tm = min(PROJ_TM, n)
    return pl.pallas_call(
        _norm_proj_kernel,
        grid=(n // tm,),
        in_specs=[pl.BlockSpec((tm, D_MODEL), lambda i: (i, 0)), _resident((1, D_MODEL))]
                 + [_resident(w.shape) for w in weights],
        out_specs=[pl.BlockSpec((tm, w.shape[1]), lambda i: (i, 0)) for w in weights],
        out_shape=[jax.ShapeDtypeStruct((n, w.shape[1]), jnp.float32) for w in weights],
        compiler_params=pltpu.CompilerParams(dimension_semantics=("arbitrary",), vmem_limit_bytes=PROJ_VMEM_BYTES),
        name=name,
    )(x, norm_w.reshape(1, D_MODEL), *weights)


def _merge_proj_kernel(x_ref, oa_ref, ys_ref, g_ref, nw_ref, wa_ref, ws_ref, wo_ref, x2_ref, h2_ref):
    f32, bf16 = jnp.float32, jnp.bfloat16
    y_attn = jnp.dot(oa_ref[...].astype(bf16), wa_ref[...], preferred_element_type=f32)
    y_ssm = jnp.dot(ys_ref[...].astype(bf16), ws_ref[...], preferred_element_type=f32)
    g = jax.nn.sigmoid(g_ref[...])
    merged = g[:, :D_MODEL] * y_ssm + g[:, D_MODEL:] * y_attn
    x2 = x_ref[...] + jnp.dot(merged.astype(bf16), wo_ref[...], preferred_element_type=f32)
    x2_ref[...] = x2
    h2_ref[...] = _rms(x2, nw_ref[...])


def _merge_proj(x, o_attn, y_ssm, br_g, norm_w, wa_b, ws_b, wo_b):
    n = x.shape[0]
    tm = min(PROJ_TM, n)
    rows = lambda c: pl.BlockSpec((tm, c), lambda i: (i, 0))
    return pl.pallas_call(
        _merge_proj_kernel,
        grid=(n // tm,),
        in_specs=[rows(D_MODEL), rows(ATTN_WIDTH), rows(D_INNER), rows(2 * D_MODEL), _resident((1, D_MODEL)),
                  _resident(wa_b.shape), _resident(ws_b.shape), _resident(wo_b.shape)],
        out_specs=[rows(D_MODEL), rows(D_MODEL)],
        out_shape=[jax.ShapeDtypeStruct((n, D_MODEL), jnp.float32)] * 2,
        compiler_params=pltpu.CompilerParams(dimension_semantics=("arbitrary",), vmem_limit_bytes=PROJ_VMEM_BYTES),
        name="merge_proj",
    )(x, o_attn, y_ssm, br_g, norm_w.reshape(1, D_MODEL), wa_b, ws_b, wo_b)


def _add_norm_kernel(x_ref, y_ref, w_ref, o_ref):
    o_ref[...] = _rms(x_ref[...] + y_ref[...], w_ref[...])


def _add_norm(x, y, w):
    n = x.shape[0]
    tm = min(PROJ_TM, n)
    rows = pl.BlockSpec((tm, D_MODEL), lambda i: (i, 0))
    return pl.pallas_call(
        _add_norm_kernel,
        grid=(n // tm,),
        in_specs=[rows, rows, _resident((1, D_MODEL))],
        out_specs=rows,
        out_shape=jax.ShapeDtypeStruct((n, D_MODEL), jnp.float32),
        name="add_norm",
    )(x, y, w.reshape(1, D_MODEL))


def _masked_softmax(s, mask):
    s = jnp.where(mask, s.astype(jnp.float32), -jnp.inf)
    m = jnp.max(s, axis=-1, keepdims=True)
    m = jnp.where(jnp.isfinite(m), m, 0.0)
    p = jnp.exp(s - m)
    return p / jnp.maximum(jnp.sum(p, axis=-1, keepdims=True), 1e-30)


def _split_proj(p):
    idx = np.cumsum(PROJ_SIZES)[:-1].tolist()
    return jnp.split(p, idx, axis=-1)


def _to_blocks(a, blk):
    b, T = a.shape[:2]
    return jnp.moveaxis(a.reshape(b, T // blk, blk, *a.shape[2:]), 1, 0)


def _from_blocks(a):
    nb, b, blk = a.shape[:3]
    return jnp.moveaxis(a, 0, 1).reshape(b, nb * blk, *a.shape[3:])


SMALL_WIDTH = 128
CONV_TAIL = 8


def _ssd_kernel(z_ref, xbc_ref, sm_ref, cprev_ref, h0_ref, cw_ref, cbias_ref, dtb_ref, alog_ref, dskip_ref, nw_ref,
                y_ref, hout_ref, state_ref, tail_ref, *, valid):
    f32, bf16 = jnp.float32, jnp.bfloat16
    L, P, N = SSD_CHUNK, SSM_HEAD_DIM, SSM_STATE
    c = pl.program_id(1)

    @pl.when(c == 0)
    def _init():
        state_ref[...] = h0_ref[0]
        tail_ref[...] = cprev_ref[0]

    xbc = xbc_ref[0]
    padded = jnp.concatenate([tail_ref[...], xbc], axis=0)
    acc = jnp.broadcast_to(cbias_ref[...], (L, CONV_DIM))
    for w in range(CONV_WIDTH):
        lo = CONV_TAIL - (CONV_WIDTH - 1) + w
        acc = acc + cw_ref[w:w + 1, :] * padded[lo:lo + L, :]
    xc = acc * jax.nn.sigmoid(acc)
    tail_ref[...] = xbc[L - CONV_TAIL:, :]

    row = lax.broadcasted_iota(jnp.int32, (L, 1), 0)
    live = row < (valid - c * L)
    dt = jnp.where(live, jax.nn.softplus(sm_ref[0] + dtb_ref[...]), 0.0)
    a = dt * (-jnp.exp(alog_ref[...]))
    tril = (row >= lax.broadcasted_iota(jnp.int32, (L, L), 1))
    a_cum = jnp.dot(tril.astype(f32), a, preferred_element_type=f32, precision=lax.Precision.HIGHEST)
    a_cum_t = a_cum.T
    a_last = a_cum[L - 1:L, :]
    grow = jnp.exp(a_cum)
    to_end = jnp.exp(a_last - a_cum)
    carry = jnp.exp(a_last)

    xs = xc[:, :D_INNER]
    ys = []
    for g in range(SSM_GROUPS):
        bg = xc[:, D_INNER + g * N:D_INNER + (g + 1) * N]
        cg = xc[:, D_INNER + SSM_GROUPS * N + g * N:D_INNER + SSM_GROUPS * N + (g + 1) * N]
        bg_b, cg_b = bg.astype(bf16), cg.astype(bf16)
        bg_t = bg.T.astype(bf16)
        cb = lax.dot_general(cg_b, bg_b, (((1,), (1,)), ((), ())), preferred_element_type=f32)
        for j in range(SSM_HPG):
            h = g * SSM_HPG + j
            diff = a_cum[:, h:h + 1] - a_cum_t[h:h + 1, :]
            m = (cb * jnp.exp(jnp.where(tril, diff, -jnp.inf))).astype(bf16)
            x_h = xs[:, h * P:(h + 1) * P]
            xdt = x_h * dt[:, h:h + 1]
            st = state_ref[h]
            y = jnp.dot(m, xdt.astype(bf16), preferred_element_type=f32)
            y = y + jnp.dot(cg_b, st.astype(bf16), preferred_element_type=f32) * grow[:, h:h + 1]
            state_ref[h] = st * carry[:, h:h + 1] + jnp.dot(bg_t, (xdt * to_end[:, h:h + 1]).astype(bf16),
                                                             preferred_element_type=f32)
            ys.append(y + dskip_ref[:, h:h + 1] * x_h)
    z = z_ref[0]
    y_ref[0] = _rms(jnp.concatenate(ys, axis=1) * (z * jax.nn.sigmoid(z)), nw_ref[...])

    @pl.when(c == pl.num_programs(1) - 1)
    def _emit():
        hout_ref[0] = state_ref[...]


def _mamba_branch(z, xbc, small, conv_prev, ssm_prev, conv_w, conv_b, dt_bias, a_log, d_skip, ssm_norm_w):
    b, L, _ = xbc.shape
    Lc = SSD_CHUNK
    Lp = -(-L // Lc) * Lc
    conv_new = jnp.concatenate([conv_prev, xbc], axis=1)[:, -(CONV_WIDTH - 1):]
    if Lp != L:
        padrows = lambda a: jnp.pad(a, ((0, 0), (0, Lp - L), (0, 0)))
        z, xbc, small = padrows(z), padrows(xbc), padrows(small)
    cprev = jnp.pad(conv_prev, ((0, 0), (CONV_TAIL - (CONV_WIDTH - 1), 0), (0, 0)))
    h0 = jnp.swapaxes(ssm_prev, 2, 3)
    lane = lambda v: jnp.pad(v.reshape(1, SSM_HEADS), ((0, 0), (0, SMALL_WIDTH - SSM_HEADS)))
    full = lambda shape: pl.BlockSpec(shape, lambda bi, c: (0,) * len(shape))
    kern = functools.partial(_ssd_kernel, valid=L)
    y, hout = pl.pallas_call(
        kern,
        grid=(b, Lp // Lc),
        in_specs=[pl.BlockSpec((1, Lc, D_INNER), lambda bi, c: (bi, c, 0)),
                  pl.BlockSpec((1, Lc, CONV_DIM), lambda bi, c: (bi, c, 0)),
                  pl.BlockSpec((1, Lc, SMALL_WIDTH), lambda bi, c: (bi, c, 0)),
                  pl.BlockSpec((1, CONV_TAIL, CONV_DIM), lambda bi, c: (bi, 0, 0)),
                  pl.BlockSpec((1, SSM_HEADS, SSM_STATE, SSM_HEAD_DIM), lambda bi, c: (bi, 0, 0, 0)),
                  full((CONV_WIDTH, CONV_DIM)), full((1, CONV_DIM)), full((1, SMALL_WIDTH)), full((1, SMALL_WIDTH)),
                  full((1, SMALL_WIDTH)), full((1, D_INNER))],
        out_specs=[pl.BlockSpec((1, Lc, D_INNER), lambda bi, c: (bi, c, 0)),
                   pl.BlockSpec((1, SSM_HEADS, SSM_STATE, SSM_HEAD_DIM), lambda bi, c: (bi, 0, 0, 0))],
        out_shape=[jax.ShapeDtypeStruct((b, Lp, D_INNER), jnp.float32),
                   jax.ShapeDtypeStruct((b, SSM_HEADS, SSM_STATE, SSM_HEAD_DIM), jnp.float32)],
        scratch_shapes=[pltpu.VMEM((SSM_HEADS, SSM_STATE, SSM_HEAD_DIM), jnp.float32),
                        pltpu.VMEM((CONV_TAIL, CONV_DIM), jnp.float32)],
        compiler_params=pltpu.CompilerParams(dimension_semantics=("arbitrary", "arbitrary"),
                                             vmem_limit_bytes=48 * 1024 * 1024),
        name="mamba_ssd",
    )(z, xbc, small, cprev, h0, conv_w, conv_b.reshape(1, CONV_DIM), lane(dt_bias), lane(a_log), lane(d_skip),
      ssm_norm_w.reshape(1, D_INNER))
    return y[:, :L], conv_new, jnp.swapaxes(hout, 2, 3)


def _cmp_attend(q, cmp_kv, q_pos):
    n = cmp_kv.shape[1]
    blk_end = jnp.arange(n) * CMP_BLOCK + (CMP_BLOCK - 1)
    s = jnp.einsum('btgjd,bngd->btgjn', q, cmp_kv[..., 0, :]).astype(jnp.float32) * ATTN_SCALE
    mask = (blk_end[None, :] <= q_pos[:, None])[None, :, None, None, :]
    p = _masked_softmax(s, mask)
    o = jnp.einsum('btgjn,bngd->btgjd', p.astype(cmp_kv.dtype), cmp_kv[..., 1, :])
    return o, jnp.sum(p, axis=3)


def _select_blocks(imp, q_pos, n_sel):
    imp = jnp.pad(imp, ((0, 0), (0, 0), (0, 0), (0, n_sel - imp.shape[-1])))
    j = jnp.arange(n_sel)[None, :]
    cur = (q_pos // SEL_BLOCK)[:, None]
    forced = ((j == 0) | (j == cur) | (j == cur - 1)).astype(jnp.float32)
    score = jnp.where((j <= cur)[None, :, None, :], imp + SEL_BONUS * forced[None, :, None, :], -jnp.inf)
    _, idx = lax.top_k(score, min(SEL_TOPK, n_sel))
    return idx


def _block_positions(idx):
    pos = idx[..., None] * SEL_BLOCK + jnp.arange(SEL_BLOCK)
    return pos.reshape(*idx.shape[:-1], -1)


def _sel_attend(q, rows, pos, q_pos):
    s = jnp.einsum('btgjd,btgsd->btgjs', q, rows[..., 0, :]).astype(jnp.float32) * ATTN_SCALE
    mask = (pos <= q_pos[None, :, None, None])[:, :, :, None, :]
    p = _masked_softmax(s, mask)
    return jnp.einsum('btgjs,btgsd->btgjd', p.astype(rows.dtype), rows[..., 1, :])


def _band_attend(q, kv, q_pos, k_pos):
    s = jnp.einsum('btgjd,bsgd->btgjs', q, kv[..., 0, :]).astype(jnp.float32) * ATTN_SCALE
    rel = q_pos[:, None] - k_pos[None, :]
    mask = ((rel >= 0) & (rel <= WINDOW) & (k_pos[None, :] >= 0))[None, :, None, None, :]
    p = _masked_softmax(s, mask)
    return jnp.einsum('btgjs,bsgd->btgjd', p.astype(kv.dtype), kv[..., 1, :])


def _combine(gates, o_cmp, o_sel, o_win):
    o = gates[..., 0:1] * o_cmp.astype(jnp.float32) + gates[..., 1:2] * o_sel.astype(jnp.float32) \
        + gates[..., 2:3] * o_win.astype(jnp.float32)
    return o.astype(o_cmp.dtype)


NSA_TQ = 128
NSA_TK = 512
NSA_WSPAN = WINDOW + NSA_TQ
NEG_BIG = -1e30


def _nsa_prompt_kernel(q_ref, kvc_ref, ks_ref, vst_ref, kw_ref, vwt_ref, gate_ref, o_ref,
                       pk_ref, pvt_ref, m_ref, l_ref, acc_ref, *, seq):
    f32, bf16 = jnp.float32, jnp.bfloat16
    D, J, TQ, TK = ATTN_HEAD_DIM, HEADS_PER_KV, NSA_TQ, NSA_TK
    n_blk = seq // CMP_BLOCK
    i = pl.program_id(2)
    t0 = i * TQ

    @pl.when(i == 0)
    def _pool():
        c = kvc_ref[0]
        pooled = jnp.sum(c.reshape(n_blk, CMP_BLOCK, 2 * D), axis=1) * (1.0 / CMP_BLOCK)
        pk_ref[...] = pooled[:, :D].astype(bf16)
        pvt_ref[...] = pooled[:, D:].T.astype(bf16)

    qt = (q_ref[0] * ATTN_SCALE).T
    qs = jnp.concatenate([qt[j * D:(j + 1) * D, :] for j in range(J)], axis=1).astype(bf16)
    qpos1 = t0 + lax.broadcasted_iota(jnp.int32, (1, TQ), 1)
    qpos = jnp.concatenate([qpos1] * J, axis=1)

    s = jnp.dot(pk_ref[...], qs, preferred_element_type=f32)
    blk = lax.broadcasted_iota(jnp.int32, (n_blk, 1), 0)
    vis = (blk * CMP_BLOCK + (CMP_BLOCK - 1)) <= qpos
    s = jnp.where(vis, s, -jnp.inf)
    m = jnp.max(s, axis=0, keepdims=True)
    m = jnp.where(m == -jnp.inf, 0.0, m)
    p = jnp.exp(s - m)
    p = p / jnp.maximum(jnp.sum(p, axis=0, keepdims=True), 1e-30)
    o_cmp = jnp.dot(pvt_ref[...], p.astype(bf16), preferred_element_type=f32)
    imp = p[:, 0:TQ]
    for j in range(1, J):
        imp = imp + p[:, j * TQ:(j + 1) * TQ]

    cur = qpos1 // CMP_BLOCK
    blk2 = lax.broadcasted_iota(jnp.int32, (n_blk, TQ), 0)
    forced = ((blk2 == 0) | (blk2 == cur) | (blk2 == cur - 1)).astype(f32)
    score = jnp.where(blk2 <= cur, imp + SEL_BONUS * forced, -jnp.inf)
    sel = jnp.zeros((n_blk, TQ), f32)
    for _ in range(min(SEL_TOPK, n_blk)):
        mx = jnp.max(score, axis=0, keepdims=True)
        first = jnp.min(jnp.where(score == mx, blk2, n_blk), axis=0, keepdims=True)
        pick = blk2 == first
        sel = jnp.where(pick, 1.0, sel)
        score = jnp.where(pick, -jnp.inf, score)
    sel_b = sel.astype(bf16)

    m_ref[...] = jnp.full(m_ref.shape, NEG_BIG, f32)
    l_ref[...] = jnp.zeros(l_ref.shape, f32)
    acc_ref[...] = jnp.zeros(acc_ref.shape, f32)
    n_kt = (t0 + TQ + TK - 1) // TK

    def sel_tile(kt, carry):
        k0 = pl.multiple_of(kt * TK, TK)
        kpos = k0 + lax.broadcasted_iota(jnp.int32, (TK, 1), 0)
        expand = ((kpos // CMP_BLOCK) == lax.broadcasted_iota(jnp.int32, (TK, n_blk), 1)).astype(bf16)
        chosen = jnp.dot(expand, sel_b, preferred_element_type=f32)
        ok = (chosen > 0.5) & (kpos <= qpos1)
        bias1 = jnp.where(ok, 0.0, NEG_BIG)
        bias = jnp.concatenate([bias1] * J, axis=1)
        st = jnp.dot(ks_ref[0, 0, pl.ds(k0, TK), :], qs, preferred_element_type=f32) + bias
        m_old = m_ref[...]
        m_new = jnp.maximum(m_old, jnp.max(st, axis=0, keepdims=True))
        alpha = jnp.exp(m_old - m_new)
        pt = jnp.exp(st - m_new)
        l_ref[...] = alpha * l_ref[...] + jnp.sum(pt, axis=0, keepdims=True)
        acc_ref[...] = alpha * acc_ref[...] + jnp.dot(vst_ref[0, 0, :, pl.ds(k0, TK)], pt.astype(bf16),
                                                      preferred_element_type=f32)
        m_ref[...] = m_new
        return carry

    lax.fori_loop(0, n_kt, sel_tile, 0)
    o_sel = acc_ref[...] / l_ref[...]

    w0 = pl.multiple_of(jnp.maximum(t0 - WINDOW, 0), TQ)
    span = min(NSA_WSPAN, seq)
    kposw = w0 + lax.broadcasted_iota(jnp.int32, (span, 1), 0)
    rel = qpos1 - kposw
    biasw1 = jnp.where((rel >= 0) & (rel <= WINDOW), 0.0, NEG_BIG)
    biasw = jnp.concatenate([biasw1] * J, axis=1)
    sw = jnp.dot(kw_ref[0, 0, pl.ds(w0, span), :], qs, preferred_element_type=f32) + biasw
    pw = jnp.exp(sw - jnp.max(sw, axis=0, keepdims=True))
    lw = jnp.sum(pw, axis=0, keepdims=True)
    o_win = jnp.dot(vwt_ref[0, 0, :, pl.ds(w0, span)], pw.astype(bf16), preferred_element_type=f32) / lw

    def gate_row(br):
        g = jax.nn.sigmoid(gate_ref[0, 0, br])
        return jnp.concatenate([g[j:j + 1, :] for j in range(J)], axis=1)

    out = gate_row(0) * o_cmp + gate_row(1) * o_sel + gate_row(2) * o_win
    out = jnp.concatenate([out[:, j * TQ:(j + 1) * TQ] for j in range(J)], axis=0)
    o_ref[0] = out.T


def _nsa_prompt_attention(q, kv_c, kv_s, kv_w, nsa_g):
    b, T, _ = q.shape
    G, J, D, TQ = KV_HEADS, HEADS_PER_KV, ATTN_HEAD_DIM, NSA_TQ
    bf16 = jnp.bfloat16

    def split_kv(kv):
        r = kv.reshape(b, T, G, 2, D).astype(bf16)
        return jnp.transpose(r[:, :, :, 0], (0, 2, 1, 3)), jnp.transpose(r[:, :, :, 1], (0, 2, 3, 1))

    ks, vst = split_kv(kv_s)
    kw, vwt = split_kv(kv_w)
    gates_t = jnp.transpose(nsa_g.reshape(b, T, G, J, 3), (0, 2, 4, 3, 1))
    n_blk = T // CMP_BLOCK
    kern = functools.partial(_nsa_prompt_kernel, seq=T)
    return pl.pallas_call(
        kern,
        grid=(b, G, T // TQ),
        in_specs=[
            pl.BlockSpec((1, TQ, J * D), lambda bi, g, i: (bi, i, g)),
            pl.BlockSpec((1, T, 2 * D), lambda bi, g, i: (bi, 0, g)),
            pl.BlockSpec((1, 1, T, D), lambda bi, g, i: (bi, g, 0, 0)),
            pl.BlockSpec((1, 1, D, T), lambda bi, g, i: (bi, g, 0, 0)),
            pl.BlockSpec((1, 1, T, D), lambda bi, g, i: (bi, g, 0, 0)),
            pl.BlockSpec((1, 1, D, T), lambda bi, g, i: (bi, g, 0, 0)),
            pl.BlockSpec((1, 1, 3, J, TQ), lambda bi, g, i: (bi, g, 0, 0, i)),
        ],
        out_specs=pl.BlockSpec((1, TQ, J * D), lambda bi, g, i: (bi, i, g)),
        out_shape=jax.ShapeDtypeStruct((b, T, G * J * D), jnp.float32),
        scratch_shapes=[
            pltpu.VMEM((n_blk, D), bf16),
            pltpu.VMEM((D, n_blk), bf16),
            pltpu.VMEM((1, J * TQ), jnp.float32),
            pltpu.VMEM((1, J * TQ), jnp.float32),
            pltpu.VMEM((D, J * TQ), jnp.float32),
        ],
        compiler_params=pltpu.CompilerParams(
            dimension_semantics=("arbitrary", "arbitrary", "arbitrary"),
            vmem_limit_bytes=48 * 1024 * 1024),
        name="nsa_prompt",
    )(q, kv_c, ks, vst, kw, vwt, gates_t)


def _nsa_prompt(q, kv_c, kv_s, kv_w, nsa_g):
    b, T = q.shape[:2]
    win_new = kv_w.reshape(b, T, KV_HEADS, 2, ATTN_HEAD_DIM)[:, -min(WINDOW, T):]
    return _nsa_prompt_attention(q, kv_c, kv_s, kv_w, nsa_g), win_new


NSAS_PPS = 8


def _nsa_sample_kernel(pt_ref, qs_ref, gate_ref, *refs, n_pages, t_len, w_eff):
    f32, bf16 = jnp.float32, jnp.bfloat16
    PPS, G, D, J, T = NSAS_PPS, KV_HEADS, ATTN_HEAD_DIM, HEADS_PER_KV, t_len
    cmp_refs, sel_refs = refs[:PPS], refs[PPS:2 * PPS]
    tail_ref, win_ref, o_ref, pooled_ref, ocmp_ref, sel_ref, m_ref, l_ref, acc_ref = refs[2 * PPS:]
    C = J * T
    ph, pg = pl.program_id(1), pl.program_id(2)
    n_pg = n_pages // PPS
    past = n_pages * PAGE_SIZE
    n_blk = past // CMP_BLOCK
    n_sel_pad = sel_ref.shape[1]
    blk_per_page = PAGE_SIZE // CMP_BLOCK
    col_t = lax.broadcasted_iota(jnp.int32, (1, C), 1) % T
    qpos = past + col_t

    def queries(g):
        return (qs_ref[0, g] * ATTN_SCALE).astype(bf16)

    def flash(g, kv, ok):
        k, v = kv[:, :D], kv[:, D:]
        st = jnp.where(ok, jnp.dot(k.astype(bf16), queries(g), preferred_element_type=f32), NEG_BIG)
        m_old = m_ref[g]
        m_new = jnp.maximum(m_old, jnp.max(st, axis=0, keepdims=True))
        alpha = jnp.exp(m_old - m_new)
        p = jnp.exp(st - m_new)
        l_ref[g] = alpha * l_ref[g] + jnp.sum(p, axis=0, keepdims=True)
        acc_ref[g] = alpha * acc_ref[g] + jnp.dot(v.T.astype(bf16), p.astype(bf16), preferred_element_type=f32)
        m_ref[g] = m_new

    @pl.when(ph == 0)
    def _compressed():
        means = []
        for r in range(PPS):
            page = cmp_refs[r][0]
            means.append(jnp.sum(page.reshape(blk_per_page, CMP_BLOCK, G * 2 * D), axis=1) * (1.0 / CMP_BLOCK))
        rows = PPS * blk_per_page
        pooled_ref[pl.ds(pl.multiple_of(pg * rows, rows), rows), :] = jnp.concatenate(means, axis=0)

        @pl.when(pg == n_pg - 1)
        def _select():
            blk = lax.broadcasted_iota(jnp.int32, (n_blk, 1), 0)
            vis = (blk * CMP_BLOCK + (CMP_BLOCK - 1)) <= qpos
            fold = (lax.broadcasted_iota(jnp.int32, (C, T), 0) % T == lax.broadcasted_iota(jnp.int32, (C, T), 1))
            blk2 = lax.broadcasted_iota(jnp.int32, (n_sel_pad, T), 0)
            cur = (past + lax.broadcasted_iota(jnp.int32, (1, T), 1)) // SEL_BLOCK
            forced = ((blk2 == 0) | (blk2 == cur) | (blk2 == cur - 1)).astype(f32)
            for g in range(G):
                pooled = pooled_ref[:, g * 2 * D:(g + 1) * 2 * D]
                s = jnp.dot(pooled[:, :D].astype(bf16), queries(g), preferred_element_type=f32)
                s = jnp.where(vis, s, -jnp.inf)
                m = jnp.max(s, axis=0, keepdims=True)
                m = jnp.where(m == -jnp.inf, 0.0, m)
                p = jnp.exp(s - m)
                p = p / jnp.maximum(jnp.sum(p, axis=0, keepdims=True), 1e-30)
                ocmp_ref[g] = jnp.dot(pooled[:, D:].T.astype(bf16), p.astype(bf16), preferred_element_type=f32)
                imp = jnp.dot(p, fold.astype(f32), preferred_element_type=f32, precision=lax.Precision.HIGHEST)
                imp = jnp.concatenate([imp, jnp.zeros((n_sel_pad - n_blk, T), f32)], axis=0)
                score = jnp.where(blk2 <= cur, imp + SEL_BONUS * forced, -jnp.inf)
                sel = jnp.zeros((n_sel_pad, T), f32)
                for _ in range(SEL_TOPK):
                    mx = jnp.max(score, axis=0, keepdims=True)
                    first = jnp.min(jnp.where(score == mx, blk2, n_sel_pad), axis=0, keepdims=True)
                    pick = blk2 == first
                    sel = jnp.where(pick, 1.0, sel)
                    score = jnp.where(pick, -jnp.inf, score)
                sel_ref[g] = jnp.concatenate([sel] * J, axis=1)

    @pl.when(ph == 1)
    def _selected():
        @pl.when(pg == 0)
        def _init():
            m_ref[...] = jnp.full(m_ref.shape, NEG_BIG, f32)
            l_ref[...] = jnp.zeros(l_ref.shape, f32)
            acc_ref[...] = jnp.zeros(acc_ref.shape, f32)

        key = lax.broadcasted_iota(jnp.int32, (PAGE_SIZE, 1), 0)
        for r in range(PPS):
            page = sel_refs[r][0]
            first_blk = (pg * PPS + r) * blk_per_page
            for g in range(G):
                chosen = jnp.where(key < CMP_BLOCK, sel_ref[g, pl.ds(first_blk, 1), :], sel_ref[g, pl.ds(first_blk + 1, 1), :])
                flash(g, page[:, g * 2 * D:(g + 1) * 2 * D], chosen > 0.5)

        @pl.when(pg == n_pg - 1)
        def _finish():
            tail = tail_ref[0]
            wkeys = win_ref[0]
            span = wkeys.shape[0]
            wi = lax.broadcasted_iota(jnp.int32, (span, 1), 0)
            rel = qpos - (past - w_eff + wi)
            wok = (rel >= 0) & (rel <= WINDOW) & (wi < w_eff + T)
            for g in range(G):
                ok = (sel_ref[g, n_blk:n_blk + 1, :] > 0.5) & (key <= col_t) & (key < T)
                flash(g, tail[:, g * 2 * D:(g + 1) * 2 * D], ok)
                o_sel = acc_ref[g] / l_ref[g]
                kvw = wkeys[:, g * 2 * D:(g + 1) * 2 * D]
                sw = jnp.where(wok, jnp.dot(kvw[:, :D].astype(bf16), queries(g), preferred_element_type=f32), NEG_BIG)
                pw = jnp.exp(sw - jnp.max(sw, axis=0, keepdims=True))
                o_win = jnp.dot(kvw[:, D:].T.astype(bf16), pw.astype(bf16), preferred_element_type=f32)
                o_win = o_win / jnp.sum(pw, axis=0, keepdims=True)
                gates = jax.nn.sigmoid(gate_ref[0, g])
                o_ref[0, g] = gates[0:1] * ocmp_ref[g] + gates[1:2] * o_sel + gates[2:3] * o_win


def _nsa_sample(pool_cmp, pool_sel, win_buf, page_table, q, kv_cmp, kv_sel, kv_win, nsa_g):
    b, T, _ = q.shape
    assert T < CMP_BLOCK and page_table.shape[1] % NSAS_PPS == 0
    G, J, D, PPS = KV_HEADS, HEADS_PER_KV, ATTN_HEAD_DIM, NSAS_PPS
    n_pages = page_table.shape[1]
    n_pg = n_pages // PPS
    past = n_pages * PAGE_SIZE
    n_blk = past // CMP_BLOCK
    n_sel_pad = -(-(n_blk + 1) // 8) * 8
    w_eff = win_buf.shape[1]
    C = J * T
    n_pool = pool_cmp.shape[0]
    pool_cmp = pool_cmp.reshape(n_pool, PAGE_SIZE, KV_WIDTH)
    pool_sel = pool_sel.reshape(n_pool, PAGE_SIZE, KV_WIDTH)
    qs = jnp.transpose(q.reshape(b, T, G, J, D), (0, 2, 4, 3, 1)).reshape(b, G, D, C)
    gates = jnp.transpose(nsa_g.reshape(b, T, G, J, 3), (0, 2, 4, 3, 1)).reshape(b, G, 3, C)
    tail = jnp.pad(kv_sel, ((0, 0), (0, PAGE_SIZE - T), (0, 0)))
    keys = jnp.concatenate([win_buf.reshape(b, w_eff, KV_WIDTH), kv_win], axis=1)
    span = -(-(w_eff + T) // PAGE_SIZE) * PAGE_SIZE
    wkeys = jnp.pad(keys, ((0, 0), (0, span - (w_eff + T)), (0, 0)))

    def cmp_map(r):
        return lambda bi, ph, pg, pt: (pt[bi, jnp.where(ph == 0, pg, n_pg - 1) * PPS + r], 0, 0)

    def sel_map(r):
        return lambda bi, ph, pg, pt: (pt[bi, jnp.where(ph == 1, pg, 0) * PPS + r], 0, 0)

    page_block = (1, PAGE_SIZE, KV_WIDTH)
    per_b = lambda shape: pl.BlockSpec(shape, lambda bi, ph, pg, pt: (bi,) + (0,) * (len(shape) - 1))
    kern = functools.partial(_nsa_sample_kernel, n_pages=n_pages, t_len=T, w_eff=w_eff)
    out = pl.pallas_call(
        kern,
        grid_spec=pltpu.PrefetchScalarGridSpec(
            num_scalar_prefetch=1,
            grid=(b, 2, n_pg),
            in_specs=[per_b((1, G, D, C)), per_b((1, G, 3, C))]
                     + [pl.BlockSpec(page_block, cmp_map(r)) for r in range(PPS)]
                     + [pl.BlockSpec(page_block, sel_map(r)) for r in range(PPS)]
                     + [per_b((1, PAGE_SIZE, KV_WIDTH)), per_b((1, span, KV_WIDTH))],
            out_specs=per_b((1, G, D, C)),
            scratch_shapes=[pltpu.VMEM((n_blk, KV_WIDTH), jnp.float32),
                            pltpu.VMEM((G, D, C), jnp.float32),
                            pltpu.VMEM((G, n_sel_pad, C), jnp.float32),
                            pltpu.VMEM((G, 1, C), jnp.float32),
                            pltpu.VMEM((G, 1, C), jnp.float32),
                            pltpu.VMEM((G, D, C), jnp.float32)]),
        out_shape=jax.ShapeDtypeStruct((b, G, D, C), jnp.float32),
        compiler_params=pltpu.CompilerParams(dimension_semantics=("arbitrary", "arbitrary", "arbitrary"),
                                             vmem_limit_bytes=32 * 1024 * 1024),
        name="nsa_sample",
    )(page_table, qs, gates, *([pool_cmp] * PPS), *([pool_sel] * PPS), tail, wkeys)
    o_attn = jnp.transpose(out.reshape(b, G, D, J, T), (0, 4, 1, 3, 2)).reshape(b, T, ATTN_WIDTH)
    return o_attn, keys[:, -w_eff:].reshape(b, w_eff, KV_HEADS, 2, D)


PEER_PICKS = PEER_HEADS * PEER_TOPK
PEER_TT = 128
PEER_GT = 64
ROW_SUBLANES = D_MODEL // 128


def _top16_cols(s, n_rows, iota_rows):
    vals, rows = [], []
    for _ in range(PEER_TOPK):
        mx = jnp.max(s, axis=0, keepdims=True)
        first = jnp.min(jnp.where(s == mx, iota_rows, n_rows), axis=0, keepdims=True)
        vals.append(mx)
        rows.append(first)
        s = jnp.where(iota_rows == first, -jnp.inf, s)
    return jnp.concatenate(vals, axis=0), jnp.concatenate(rows, axis=0)


def _peer_route_kernel(h_ref, wq_ref, keys_ref, exp_ref, gate_ref):
    f32 = jnp.float32
    TT = PEER_TT
    q = jnp.dot(h_ref[...].astype(jnp.bfloat16), wq_ref[...], preferred_element_type=f32)
    iota_k = lax.broadcasted_iota(jnp.int32, (PEER_KEYS, TT), 0)
    iota_c = lax.broadcasted_iota(jnp.int32, (PEER_TOPK * PEER_TOPK, TT), 0)
    exp_rows, gate_rows = [], []
    for hd in range(PEER_HEADS):
        tops = []
        for c in range(2):
            j = hd * 2 + c
            qj = q[:, j * PEER_HALF:(j + 1) * PEER_HALF].astype(jnp.bfloat16)
            st = lax.dot_general(keys_ref[j], qj, (((1,), (1,)), ((), ())), preferred_element_type=f32)
            tops.append(_top16_cols(st, PEER_KEYS, iota_k))
        (s0, i0), (s1, i1) = tops
        cand_s = jnp.concatenate([s0[a:a + 1, :] + s1 for a in range(PEER_TOPK)], axis=0)
        cand_i = jnp.concatenate([i0[a:a + 1, :] * PEER_KEYS + i1 for a in range(PEER_TOPK)], axis=0)
        best_s, best_e = [], []
        s = cand_s
        for _ in range(PEER_TOPK):
            mx = jnp.max(s, axis=0, keepdims=True)
            first = jnp.min(jnp.where(s == mx, iota_c, PEER_TOPK * PEER_TOPK), axis=0, keepdims=True)
            pick = iota_c == first
            best_s.append(mx)
            best_e.append(jnp.sum(jnp.where(pick, cand_i, 0), axis=0, keepdims=True))
            s = jnp.where(pick, -jnp.inf, s)
        bs = jnp.concatenate(best_s, axis=0)
        be = jnp.concatenate(best_e, axis=0)
        ex = jnp.exp(bs - bs[0:1, :])
        gate_rows.append(ex / jnp.sum(ex, axis=0, keepdims=True))
        exp_rows.append(be)
    exp_ref[...] = _row_code(jnp.concatenate(exp_rows, axis=0)).T
    gate_ref[...] = jnp.concatenate(gate_rows, axis=0).T


def _peer_route(h, wq_b, keys_b):
    n = h.shape[0]
    TT = PEER_TT
    return pl.pallas_call(
        _peer_route_kernel,
        grid=(n // TT,),
        in_specs=[pl.BlockSpec((TT, D_MODEL), lambda i: (i, 0)),
                  pl.BlockSpec((D_MODEL, PEER_HEADS * PEER_QDIM), lambda i: (0, 0)),
                  pl.BlockSpec((PEER_HEADS * 2, PEER_KEYS, PEER_HALF), lambda i: (0, 0, 0))],
        out_specs=[pl.BlockSpec((TT, PEER_PICKS), lambda i: (i, 0)),
                   pl.BlockSpec((TT, PEER_PICKS), lambda i: (i, 0))],
        out_shape=[jax.ShapeDtypeStruct((n, PEER_PICKS), jnp.int32),
                   jax.ShapeDtypeStruct((n, PEER_PICKS), jnp.float32)],
        compiler_params=pltpu.CompilerParams(dimension_semantics=("arbitrary",),
                                             vmem_limit_bytes=40 * 1024 * 1024),
        name="peer_route",
    )(h, wq_b, keys_b)


def _pack_rows(w):
    b = lax.bitcast_convert_type(w.astype(jnp.bfloat16), jnp.uint16).astype(jnp.uint32)
    b = b.reshape(N_EXPERTS // 2, 2, D_MODEL)
    return ((b[:, 0] << 16) | b[:, 1]).reshape(N_EXPERTS // 2 * ROW_SUBLANES, 128)


def _row_code(e):
    return (e >> 1) * ROW_SUBLANES + (e & 1)


def _load_row(tab_ref, start, parity_word):
    w = tab_ref[pl.ds(pl.multiple_of(start, ROW_SUBLANES), ROW_SUBLANES), :]
    w = (w << ((parity_word & jnp.uint32(1)) << jnp.uint32(4))) & jnp.uint32(0xFFFF0000)
    return lax.bitcast_convert_type(w, jnp.float32)


def _splat_u32(s):
    return lax.bitcast_convert_type(jnp.full((ROW_SUBLANES, 128), s, jnp.int32), jnp.uint32)


def _sublane_sums(tiles):
    sub = lax.broadcasted_iota(jnp.int32, (ROW_SUBLANES, 128), 0)
    step = ROW_SUBLANES // 2
    while step >= 1:
        low = (sub % (2 * step)) < step
        tiles = [jnp.where(low, a + pltpu.roll(a, ROW_SUBLANES - step, 0), pltpu.roll(b, step, 0) + b)
                 for a, b in zip(tiles[:len(tiles) // 2], tiles[len(tiles) // 2:])]
        step //= 2
    return tiles[0]


PEER_GROUP = 16


def _peer_up_kernel(code_ref, h_ref, gate_ref, tab_ref, coef_ref, dots_ref):
    GT = PEER_GT

    def token(t, carry):
        h = h_ref[pl.ds(pl.multiple_of(t * ROW_SUBLANES, ROW_SUBLANES), ROW_SUBLANES), :]

        def group(g, tiles):
            prods = []
            base = t * PEER_PICKS + g * PEER_GROUP
            for kk in range(PEER_GROUP):
                code = code_ref[base + kk]
                prods.append(_load_row(tab_ref, code & ~(ROW_SUBLANES - 1), _splat_u32(code)) * h)
            rows = [_sublane_sums(prods[i:i + ROW_SUBLANES]) for i in range(0, PEER_GROUP, ROW_SUBLANES)]
            dots_ref[pl.ds(pl.multiple_of(t * PEER_PICKS + g * PEER_GROUP, PEER_GROUP), PEER_GROUP), :] = (
                jnp.concatenate(rows, axis=0))
            return tiles

        lax.fori_loop(0, PEER_PICKS // PEER_GROUP, group, 0)
        return carry

    lax.fori_loop(0, GT, token, 0)
    rows = [jnp.sum(dots_ref[t * PEER_PICKS:(t + 1) * PEER_PICKS, :].T, axis=0, keepdims=True) for t in range(GT)]
    coef_ref[...] = gate_ref[...] * jax.nn.gelu(jnp.concatenate(rows, axis=0))


def _peer_down_kernel(start_ref, cbits_ref, tab_ref, o_ref):
    GT = PEER_GT

    def token(t, carry):
        def group(g, accs):
            accs = list(accs)
            base = t * PEER_PICKS + g * PEER_GROUP
            for kk in range(PEER_GROUP):
                cb = _splat_u32(cbits_ref[base + kk])
                coef = lax.bitcast_convert_type(cb & jnp.uint32(0xFFFFFFFE), jnp.float32)
                accs[kk % 4] = accs[kk % 4] + _load_row(tab_ref, start_ref[base + kk], cb) * coef
            return tuple(accs)

        zero = jnp.zeros((ROW_SUBLANES, 128), jnp.float32)
        accs = lax.fori_loop(0, PEER_PICKS // PEER_GROUP, group, (zero, zero, zero, zero))
        o_ref[pl.ds(pl.multiple_of(t * ROW_SUBLANES, ROW_SUBLANES), ROW_SUBLANES), :] = (accs[0] + accs[1]) + (accs[2] + accs[3])
        return carry

    lax.fori_loop(0, GT, token, 0)


def _peer_gather(h, experts, gates, u_tab, v_tab):
    n = h.shape[0]
    GT = PEER_GT
    tab_rows = N_EXPERTS // 2 * ROW_SUBLANES
    flat_smem = pl.BlockSpec((GT * PEER_PICKS,), lambda i: (i,), memory_space=pltpu.MemorySpace.SMEM)
    tab_spec = pl.BlockSpec((tab_rows, 128), lambda i: (0, 0), pipeline_mode=pl.Buffered(1))
    params = pltpu.CompilerParams(dimension_semantics=("arbitrary",), vmem_limit_bytes=56 * 1024 * 1024)
    codes = experts.reshape(n * PEER_PICKS)
    coef = pl.pallas_call(
        _peer_up_kernel,
        grid=(n // GT,),
        in_specs=[flat_smem,
                  pl.BlockSpec((GT * ROW_SUBLANES, 128), lambda i: (i, 0)),
                  pl.BlockSpec((GT, PEER_PICKS), lambda i: (i, 0)),
                  tab_spec],
        out_specs=pl.BlockSpec((GT, PEER_PICKS), lambda i: (i, 0)),
        out_shape=jax.ShapeDtypeStruct((n, PEER_PICKS), jnp.float32),
        scratch_shapes=[pltpu.VMEM((GT * PEER_PICKS, 128), jnp.float32)],
        compiler_params=params,
        name="peer_up",
    )(codes, h.reshape(n * ROW_SUBLANES, 128), gates, u_tab)
    out = pl.pallas_call(
        _peer_down_kernel,
        grid=(n // GT,),
        in_specs=[flat_smem, flat_smem, tab_spec],
        out_specs=pl.BlockSpec((GT * ROW_SUBLANES, 128), lambda i: (i, 0)),
        out_shape=jax.ShapeDtypeStruct((n * ROW_SUBLANES, 128), jnp.float32),
        compiler_params=params,
        name="peer_down",
    )(codes & ~(ROW_SUBLANES - 1),
      (lax.bitcast_convert_type(coef, jnp.int32).reshape(n * PEER_PICKS) & ~1) | (codes & 1), v_tab)
    return out.reshape(n, D_MODEL)


def _peer_tokens(h, peer_wq, peer_sub_keys, u_tab, v_tab):
    shape = h.shape
    hf = h.reshape(-1, D_MODEL)
    wq_b = peer_wq.astype(jnp.bfloat16)
    keys_b = peer_sub_keys.reshape(PEER_HEADS * 2, PEER_KEYS, PEER_HALF).astype(jnp.bfloat16)
    experts, gates = _peer_route(hf, wq_b, keys_b)
    return _peer_gather(hf, experts, gates, u_tab, v_tab).reshape(shape)


def _layer_weights(w_in, w_ssm_branch, w_attn_branch, w_out, peer_wq, peer_sub_keys, peer_u, peer_v):
    bf16 = jnp.bfloat16
    offs = np.cumsum((0,) + PROJ_SIZES).tolist()
    seg = lambda k: w_in[:, offs[k]:offs[k + 1]]
    pad = jnp.zeros((D_MODEL, SMALL_WIDTH - SSM_HEADS - 3 * ATTN_HEADS), w_in.dtype)
    return dict(
        in_ssm=[seg(0).astype(bf16), seg(1).astype(bf16)],
        in_attn=[seg(3).astype(bf16), jnp.concatenate([seg(4), seg(5), seg(6)], axis=1).astype(bf16),
                 seg(8).astype(bf16), jnp.concatenate([seg(2), seg(7), pad], axis=1).astype(bf16)],
        w_ssm=w_ssm_branch.astype(bf16), w_attn=w_attn_branch.astype(bf16), w_out=w_out.astype(bf16),
        wq=peer_wq.astype(bf16),
        keys=peer_sub_keys.reshape(PEER_HEADS * 2, PEER_KEYS, PEER_HALF).astype(bf16),
        u_tab=_pack_rows(peer_u), v_tab=_pack_rows(peer_v))


def _block(x, conv_prev, ssm_prev, nsa_core, lw, norm_mix_w, conv_w, conv_b, dt_bias, a_log, d_skip, ssm_norm_w,
           norm_ffn_w):
    b, T, _ = x.shape
    n = b * T
    xf = x.reshape(n, D_MODEL)
    z, xbc = _norm_proj(xf, norm_mix_w, lw["in_ssm"], "in_proj_ssm")
    q, kv, br_g, small = _norm_proj(xf, norm_mix_w, lw["in_attn"], "in_proj_attn")
    nsa_g = small[:, SSM_HEADS:SSM_HEADS + 3 * ATTN_HEADS]
    kv_c, kv_s, kv_w = (kv[:, k * KV_WIDTH:(k + 1) * KV_WIDTH].reshape(b, T, KV_WIDTH) for k in range(3))
    y_ssm, conv_new, ssm_new = _mamba_branch(z.reshape(b, T, D_INNER), xbc.reshape(b, T, CONV_DIM),
                                             small.reshape(b, T, SMALL_WIDTH), conv_prev, ssm_prev, conv_w, conv_b,
                                             dt_bias, a_log, d_skip, ssm_norm_w)
    o_attn, win_new = nsa_core(q.reshape(b, T, ATTN_WIDTH), kv_c, kv_s, kv_w, nsa_g.reshape(b, T, 3 * ATTN_HEADS))
    x2, h2 = _merge_proj(xf, o_attn.reshape(n, ATTN_WIDTH), y_ssm.reshape(n, D_INNER), br_g, norm_ffn_w,
                         lw["w_attn"], lw["w_ssm"], lw["w_out"])
    experts, gates = _peer_route(h2, lw["wq"], lw["keys"])
    peer = _peer_gather(h2, experts, gates, lw["u_tab"], lw["v_tab"])
    kv_shape = (b, T, KV_HEADS, 2, ATTN_HEAD_DIM)
    return x2, peer, kv_c.reshape(kv_shape), kv_s.reshape(kv_shape), win_new, conv_new, ssm_new


def kernel(x_prompt, x_sample, cache_cmp_kv, cache_sel_kv, cache_win_kv, state_ssm, state_conv, page_table,
           norm_mix_w, w_in, conv_w, conv_b, dt_bias, a_log, d_skip, ssm_norm_w, w_ssm_branch, w_attn_branch,
           w_out, norm_ffn_w, peer_wq, peer_sub_keys, peer_u, peer_v, final_norm_w):
    xp, xs = x_prompt, x_sample
    cmp_p, cmp_s, sel_p, sel_s, win_p, win_s = [], [], [], [], [], []
    ssm_p, ssm_s, conv_p, conv_s = [], [], [], []
    for layer in range(DEPTH):
        lw = _layer_weights(*(a[layer] for a in (w_in, w_ssm_branch, w_attn_branch, w_out, peer_wq, peer_sub_keys,
                                                   peer_u, peer_v)))
        rest = [a[layer] for a in (norm_mix_w, conv_w, conv_b, dt_bias, a_log, d_skip, ssm_norm_w, norm_ffn_w)]
        last = layer == DEPTH - 1
        bp = xp.shape[0]
        conv0 = jnp.zeros((bp, CONV_WIDTH - 1, CONV_DIM), xp.dtype)
        ssm0 = jnp.zeros((bp, SSM_HEADS, SSM_HEAD_DIM, SSM_STATE), xp.dtype)
        x2, peer, kc, ks, kw, cv, ss = _block(xp, conv0, ssm0, _nsa_prompt, lw, *rest)
        xp = (_add_norm(x2, peer, final_norm_w) if last else x2 + peer).reshape(xp.shape)
        cmp_p.append(kc); sel_p.append(ks); win_p.append(kw); conv_p.append(cv); ssm_p.append(ss)
        core = functools.partial(_nsa_sample, cache_cmp_kv[layer], cache_sel_kv[layer], cache_win_kv[layer], page_table)
        x2, peer, kc, ks, kw, cv, ss = _block(xs, state_conv[layer], state_ssm[layer], core, lw, *rest)
        xs = (_add_norm(x2, peer, final_norm_w) if last else x2 + peer).reshape(xs.shape)
        cmp_s.append(kc); sel_s.append(ks); win_s.append(kw); conv_s.append(cv); ssm_s.append(ss)
    return (xp, xs, jnp.stack(cmp_p), jnp.stack(cmp_s), jnp.stack(sel_p), jnp.stack(sel_s),
            jnp.stack(win_p), jnp.stack(win_s), jnp.stack(ssm_p), jnp.stack(ssm_s), jnp.stack(conv_p), jnp.stack(conv_s))
```

```python
import math, functools
import jax, jax.numpy as jnp
from jax import lax
import numpy as np
from jax.experimental import pallas as pl
from jax.experimental.pallas import tpu as pltpu

D_MODEL = 1024
BATCH = 8
SEQ = 4096
DEPTH = 1
DEC_BATCH = 32
DEC_SEQ = 8
PAST_LEN = 16384
PAGE_SIZE = 128

SSM_EXPAND = 2
D_INNER = SSM_EXPAND * D_MODEL
SSM_HEAD_DIM = 64
SSM_HEADS = D_INNER // SSM_HEAD_DIM
SSM_GROUPS = 4
SSM_HPG = SSM_HEADS // SSM_GROUPS
SSM_STATE = 128
CONV_WIDTH = 4
CONV_DIM = D_INNER + 2 * SSM_GROUPS * SSM_STATE
SSD_CHUNK = 128
ATTN_HEADS = 16
ATTN_HEAD_DIM = 64
ATTN_WIDTH = ATTN_HEADS * ATTN_HEAD_DIM
KV_HEADS = 2
HEADS_PER_KV = ATTN_HEADS // KV_HEADS
KV_WIDTH = 2 * KV_HEADS * ATTN_HEAD_DIM
CMP_BLOCK = 64
SEL_BLOCK = CMP_BLOCK
SEL_TOPK = 16
SEL_BONUS = 1000.0
WINDOW = 512
WIN_QBLOCK = 128
SEL_QBLOCK = 32
ATTN_SCALE = ATTN_HEAD_DIM ** -0.5
PEER_HEADS = 8
PEER_KEYS = 128
N_EXPERTS = PEER_KEYS * PEER_KEYS
PEER_QDIM = 256
PEER_HALF = PEER_QDIM // 2
PEER_TOPK = 16
PEER_TOKEN_CHUNK = 128
EPS = 1e-6
PROJ_SIZES = (D_INNER, CONV_DIM, SSM_HEADS, ATTN_WIDTH, KV_WIDTH, KV_WIDTH, KV_WIDTH, 3 * ATTN_HEADS, 2 * D_MODEL)
PROJ_DIM = sum(PROJ_SIZES)


def _rmsnorm(x, w):
    xf = x.astype(jnp.float32)
    y = xf * lax.rsqrt(jnp.mean(xf * xf, axis=-1, keepdims=True) + EPS)
    return (y * w.astype(jnp.float32)).astype(x.dtype)


PROJ_TM = 512
PROJ_VMEM_BYTES = 56 * 1024 * 1024


def _rms(x, w):
    return x * lax.rsqrt(jnp.mean(x * x, axis=-1, keepdims=True) + EPS) * w


def _resident(shape):
    return pl.BlockSpec(shape, lambda i: (0,) * len(shape), pipeline_mode=pl.Buffered(1))


def _norm_proj_kernel(x_ref, nw_ref, *refs):
    n_out = len(refs) // 2
    h = _rms(x_ref[...], nw_ref[...]).astype(jnp.bfloat16)
    for w_ref, o_ref in zip(refs[:n_out], refs[n_out:]):
        o_ref[...] = jnp.dot(h, w_ref[...], preferred_element_type=jnp.float32)


def _norm_proj(x, norm_w, weights, name):
    n = x.shape[0]
    tm = min(PROJ_TM, n)
    return pl.pallas_call(
        _norm_proj_kernel,
        grid=(n // tm,),
        in_specs=[pl.BlockSpec((tm, D_MODEL), lambda i: (i, 0)), _resident((1, D_MODEL))]
                 + [_resident(w.shape) for w in weights],
        out_specs=[pl.BlockSpec((tm, w.shape[1]), lambda i: (i, 0)) for w in weights],
        out_shape=[jax.ShapeDtypeStruct((n, w.shape[1]), jnp.float32) for w in weights],
        compiler_params=pltpu.CompilerParams(dimension_semantics=("arbitrary",), vmem_limit_bytes=PROJ_VMEM_BYTES),
        name=name,
    )(x, norm_w.reshape(1, D_MODEL), *weights)


def _merge_proj_kernel(x_ref, oa_ref, ys_ref, g_ref, nw_ref, wa_ref, ws_ref, wo_ref, x2_ref, h2_ref):
    f32, bf16 = jnp.float32, jnp.bfloat16
    y_attn = jnp.dot(oa_ref[...].astype(bf16), wa_ref[...], preferred_element_type=f32)
    y_ssm = jnp.dot(ys_ref[...].astype(bf16), ws_ref[...], preferred_element_type=f32)
    g = jax.nn.sigmoid(g_ref[...])
    merged = g[:, :D_MODEL] * y_ssm + g[:, D_MODEL:] * y_attn
    x2 = x_ref[...] + jnp.dot(merged.astype(bf16), wo_ref[...], preferred_element_type=f32)
    x2_ref[...] = x2
    h2_ref[...] = _rms(x2, nw_ref[...])


def _merge_proj(x, o_attn, y_ssm, br_g, norm_w, wa_b, ws_b, wo_b):
    n = x.shape[0]
    tm = min(PROJ_TM, n)
    rows = lambda c: pl.BlockSpec((tm, c), lambda i: (i, 0))
    return pl.pallas_call(
        _merge_proj_kernel,
        grid=(n // tm,),
        in_specs=[rows(D_MODEL), rows(ATTN_WIDTH), rows(D_INNER), rows(2 * D_MODEL), _resident((1, D_MODEL)),
                  _resident(wa_b.shape), _resident(ws_b.shape), _resident(wo_b.shape)],
        out_specs=[rows(D_MODEL), rows(D_MODEL)],
        out_shape=[jax.ShapeDtypeStruct((n, D_MODEL), jnp.float32)] * 2,
        compiler_params=pltpu.CompilerParams(dimension_semantics=("arbitrary",), vmem_limit_bytes=PROJ_VMEM_BYTES),
        name="merge_proj",
    )(x, o_attn, y_ssm, br_g, norm_w.reshape(1, D_MODEL), wa_b, ws_b, wo_b)


def _add_norm_kernel(x_ref, y_ref, w_ref, o_ref):
    o_ref[...] = _rms(x_ref[...] + y_ref[...], w_ref[...])


def _add_norm(x, y, w):
    n = x.shape[0]
    tm = min(PROJ_TM, n)
    rows = pl.BlockSpec((tm, D_MODEL), lambda i: (i, 0))
    return pl.pallas_call(
        _add_norm_kernel,
        grid=(n // tm,),
        in_specs=[rows, rows, _resident((1, D_MODEL))],
        out_specs=rows,
        out_shape=jax.ShapeDtypeStruct((n, D_MODEL), jnp.float32),
        name="add_norm",
    )(x, y, w.reshape(1, D_MODEL))


def _masked_softmax(s, mask):
    s = jnp.where(mask, s.astype(jnp.float32), -jnp.inf)
    m = jnp.max(s, axis=-1, keepdims=True)
    m = jnp.where(jnp.isfinite(m), m, 0.0)
    p = jnp.exp(s - m)
    return p / jnp.maximum(jnp.sum(p, axis=-1, keepdims=True), 1e-30)


def _split_proj(p):
    idx = np.cumsum(PROJ_SIZES)[:-1].tolist()
    return jnp.split(p, idx, axis=-1)


def _to_blocks(a, blk):
    b, T = a.shape[:2]
    return jnp.moveaxis(a.reshape(b, T // blk, blk, *a.shape[2:]), 1, 0)


def _from_blocks(a):
    nb, b, blk = a.shape[:3]
    return jnp.moveaxis(a, 0, 1).reshape(b, nb * blk, *a.shape[3:])


SMALL_WIDTH = 128
CONV_TAIL = 8


def _ssd_kernel(z_ref, xbc_ref, sm_ref, cprev_ref, h0_ref, cw_ref, cbias_ref, dtb_ref, alog_ref, dskip_ref, nw_ref,
                y_ref, hout_ref, state_ref, tail_ref, *, valid):
    f32, bf16 = jnp.float32, jnp.bfloat16
    L, P, N = SSD_CHUNK, SSM_HEAD_DIM, SSM_STATE
    c = pl.program_id(1)

    @pl.when(c == 0)
    def _init():
        state_ref[...] = h0_ref[0]
        tail_ref[...] = cprev_ref[0]

    xbc = xbc_ref[0]
    padded = jnp.concatenate([tail_ref[...], xbc], axis=0)
    acc = jnp.broadcast_to(cbias_ref[...], (L, CONV_DIM))
    for w in range(CONV_WIDTH):
        lo = CONV_TAIL - (CONV_WIDTH - 1) + w
        acc = acc + cw_ref[w:w + 1, :] * padded[lo:lo + L, :]
    xc = acc * jax.nn.sigmoid(acc)
    tail_ref[...] = xbc[L - CONV_TAIL:, :]

    row = lax.broadcasted_iota(jnp.int32, (L, 1), 0)
    live = row < (valid - c * L)
    dt = jnp.where(live, jax.nn.softplus(sm_ref[0] + dtb_ref[...]), 0.0)
    a = dt * (-jnp.exp(alog_ref[...]))
    tril = (row >= lax.broadcasted_iota(jnp.int32, (L, L), 1))
    a_cum = jnp.dot(tril.astype(f32), a, preferred_element_type=f32, precision=lax.Precision.HIGHEST)
    a_cum_t = a_cum.T
    a_last = a_cum[L - 1:L, :]
    grow = jnp.exp(a_cum)
    to_end = jnp.exp(a_last - a_cum)
    carry = jnp.exp(a_last)

    xs = xc[:, :D_INNER]
    ys = []
    for g in range(SSM_GROUPS):
        bg = xc[:, D_INNER + g * N:D_INNER + (g + 1) * N]
        cg = xc[:, D_INNER + SSM_GROUPS * N + g * N:D_INNER + SSM_GROUPS * N + (g + 1) * N]
        bg_b, cg_b = bg.astype(bf16), cg.astype(bf16)
        bg_t = bg.T.astype(bf16)
        cb = lax.dot_general(cg_b, bg_b, (((1,), (1,)), ((), ())), preferred_element_type=f32)
        for j in range(SSM_HPG):
            h = g * SSM_HPG + j
            diff = a_cum[:, h:h + 1] - a_cum_t[h:h + 1, :]
            m = (cb * jnp.exp(jnp.where(tril, diff, -jnp.inf))).astype(bf16)
            x_h = xs[:, h * P:(h + 1) * P]
            xdt = x_h * dt[:, h:h + 1]
            st = state_ref[h]
            y = jnp.dot(m, xdt.astype(bf16), preferred_element_type=f32)
            y = y + jnp.dot(cg_b, st.astype(bf16), preferred_element_type=f32) * grow[:, h:h + 1]
            state_ref[h] = st * carry[:, h:h + 1] + jnp.dot(bg_t, (xdt * to_end[:, h:h + 1]).astype(bf16),
                                                             preferred_element_type=f32)
            ys.append(y + dskip_ref[:, h:h + 1] * x_h)
    z = z_ref[0]
    y_ref[0] = _rms(jnp.concatenate(ys, axis=1) * (z * jax.nn.sigmoid(z)), nw_ref[...])

    @pl.when(c == pl.num_programs(1) - 1)
    def _emit():
        hout_ref[0] = state_ref[...]


def _mamba_branch(z, xbc, small, conv_prev, ssm_prev, conv_w, conv_b, dt_bias, a_log, d_skip, ssm_norm_w):
    b, L, _ = xbc.shape
    Lc = SSD_CHUNK
    Lp = -(-L // Lc) * Lc
    conv_new = jnp.concatenate([conv_prev, xbc], axis=1)[:, -(CONV_WIDTH - 1):]
    if Lp != L:
        padrows = lambda a: jnp.pad(a, ((0, 0), (0, Lp - L), (0, 0)))
        z, xbc, small = padrows(z), padrows(xbc), padrows(small)
    cprev = jnp.pad(conv_prev, ((0, 0), (CONV_TAIL - (CONV_WIDTH - 1), 0), (0, 0)))
    h0 = jnp.swapaxes(ssm_prev, 2, 3)
    lane = lambda v: jnp.pad(v.reshape(1, SSM_HEADS), ((0, 0), (0, SMALL_WIDTH - SSM_HEADS)))
    full = lambda shape: pl.BlockSpec(shape, lambda bi, c: (0,) * len(shape))
    kern = functools.partial(_ssd_kernel, valid=L)
    y, hout = pl.pallas_call(
        kern,
        grid=(b, Lp // Lc),
        in_specs=[pl.BlockSpec((1, Lc, D_INNER), lambda bi, c: (bi, c, 0)),
                  pl.BlockSpec((1, Lc, CONV_DIM), lambda bi, c: (bi, c, 0)),
                  pl.BlockSpec((1, Lc, SMALL_WIDTH), lambda bi, c: (bi, c, 0)),
                  pl.BlockSpec((1, CONV_TAIL, CONV_DIM), lambda bi, c: (bi, 0, 0)),
                  pl.BlockSpec((1, SSM_HEADS, SSM_STATE, SSM_HEAD_DIM), lambda bi, c: (bi, 0, 0, 0)),
                  full((CONV_WIDTH, CONV_DIM)), full((1, CONV_DIM)), full((1, SMALL_WIDTH)), full((1, SMALL_WIDTH)),
                  full((1, SMALL_WIDTH)), full((1, D_INNER))],
        out_specs=[pl.BlockSpec((1, Lc, D_INNER), lambda bi, c: (bi, c, 0)),
                   pl.BlockSpec((1, SSM_HEADS, SSM_STATE, SSM_HEAD_DIM), lambda bi, c: (bi, 0, 0, 0))],
        out_shape=[jax.ShapeDtypeStruct((b, Lp, D_INNER), jnp.float32),
                   jax.ShapeDtypeStruct((b, SSM_HEADS, SSM_STATE, SSM_HEAD_DIM), jnp.float32)],
        scratch_shapes=[pltpu.VMEM((SSM_HEADS, SSM_STATE, SSM_HEAD_DIM), jnp.float32),
                        pltpu.VMEM((CONV_TAIL, CONV_DIM), jnp.float32)],
        compiler_params=pltpu.CompilerParams(dimension_semantics=("arbitrary", "arbitrary"),
                                             vmem_limit_bytes=48 * 1024 * 1024),
        name="mamba_ssd",
    )(z, xbc, small, cprev, h0, conv_w, conv_b.reshape(1, CONV_DIM), lane(dt_bias), lane(a_log), lane(d_skip),
      ssm_norm_w.reshape(1, D_INNER))
    return y[:, :L], conv_new, jnp.swapaxes(hout, 2, 3)


def _cmp_attend(q, cmp_kv, q_pos):
    n = cmp_kv.shape[1]
    blk_end = jnp.arange(n) * CMP_BLOCK + (CMP_BLOCK - 1)
    s = jnp.einsum('btgjd,bngd->btgjn', q, cmp_kv[..., 0, :]).astype(jnp.float32) * ATTN_SCALE
    mask = (blk_end[None, :] <= q_pos[:, None])[None, :, None, None, :]
    p = _masked_softmax(s, mask)
    o = jnp.einsum('btgjn,bngd->btgjd', p.astype(cmp_kv.dtype), cmp_kv[..., 1, :])
    return o, jnp.sum(p, axis=3)


def _select_blocks(imp, q_pos, n_sel):
    imp = jnp.pad(imp, ((0, 0), (0, 0), (0, 0), (0, n_sel - imp.shape[-1])))
    j = jnp.arange(n_sel)[None, :]
    cur = (q_pos // SEL_BLOCK)[:, None]
    forced = ((j == 0) | (j == cur) | (j == cur - 1)).astype(jnp.float32)
    score = jnp.where((j <= cur)[None, :, None, :], imp + SEL_BONUS * forced[None, :, None, :], -jnp.inf)
    _, idx = lax.top_k(score, min(SEL_TOPK, n_sel))
    return idx


def _block_positions(idx):
    pos = idx[..., None] * SEL_BLOCK + jnp.arange(SEL_BLOCK)
    return pos.reshape(*idx.shape[:-1], -1)


def _sel_attend(q, rows, pos, q_pos):
    s = jnp.einsum('btgjd,btgsd->btgjs', q, rows[..., 0, :]).astype(jnp.float32) * ATTN_SCALE
    mask = (pos <= q_pos[None, :, None, None])[:, :, :, None, :]
    p = _masked_softmax(s, mask)
    return jnp.einsum('btgjs,btgsd->btgjd', p.astype(rows.dtype), rows[..., 1, :])


def _band_attend(q, kv, q_pos, k_pos):
    s = jnp.einsum('btgjd,bsgd->btgjs', q, kv[..., 0, :]).astype(jnp.float32) * ATTN_SCALE
    rel = q_pos[:, None] - k_pos[None, :]
    mask = ((rel >= 0) & (rel <= WINDOW) & (k_pos[None, :] >= 0))[None, :, None, None, :]
    p = _masked_softmax(s, mask)
    return jnp.einsum('btgjs,bsgd->btgjd', p.astype(kv.dtype), kv[..., 1, :])


def _combine(gates, o_cmp, o_sel, o_win):
    o = gates[..., 0:1] * o_cmp.astype(jnp.float32) + gates[..., 1:2] * o_sel.astype(jnp.float32) \
        + gates[..., 2:3] * o_win.astype(jnp.float32)
    return o.astype(o_cmp.dtype)


NSA_TQ = 128
NSA_TK = 512
NSA_WSPAN = WINDOW + NSA_TQ
NEG_BIG = -1e30


def _nsa_prompt_kernel(q_ref, kvc_ref, ks_ref, vst_ref, kw_ref, vwt_ref, gate_ref, o_ref,
                       pk_ref, pvt_ref, m_ref, l_ref, acc_ref, *, seq):
    f32, bf16 = jnp.float32, jnp.bfloat16
    D, J, TQ, TK = ATTN_HEAD_DIM, HEADS_PER_KV, NSA_TQ, NSA_TK
    n_blk = seq // CMP_BLOCK
    i = pl.program_id(2)
    t0 = i * TQ

    @pl.when(i == 0)
    def _pool():
        c = kvc_ref[0]
        pooled = jnp.sum(c.reshape(n_blk, CMP_BLOCK, 2 * D), axis=1) * (1.0 / CMP_BLOCK)
        pk_ref[...] = pooled[:, :D].astype(bf16)
        pvt_ref[...] = pooled[:, D:].T.astype(bf16)

    qt = (q_ref[0] * ATTN_SCALE).T
    qs = jnp.concatenate([qt[j * D:(j + 1) * D, :] for j in range(J)], axis=1).astype(bf16)
    qpos1 = t0 + lax.broadcasted_iota(jnp.int32, (1, TQ), 1)
    qpos = jnp.concatenate([qpos1] * J, axis=1)

    s = jnp.dot(pk_ref[...], qs, preferred_element_type=f32)
    blk = lax.broadcasted_iota(jnp.int32, (n_blk, 1), 0)
    vis = (blk * CMP_BLOCK + (CMP_BLOCK - 1)) <= qpos
    s = jnp.where(vis, s, -jnp.inf)
    m = jnp.max(s, axis=0, keepdims=True)
    m = jnp.where(m == -jnp.inf, 0.0, m)
    p = jnp.exp(s - m)
    p = p / jnp.maximum(jnp.sum(p, axis=0, keepdims=True), 1e-30)
    o_cmp = jnp.dot(pvt_ref[...], p.astype(bf16), preferred_element_type=f32)
    imp = p[:, 0:TQ]
    for j in range(1, J):
        imp = imp + p[:, j * TQ:(j + 1) * TQ]

    cur = qpos1 // CMP_BLOCK
    blk2 = lax.broadcasted_iota(jnp.int32, (n_blk, TQ), 0)
    forced = ((blk2 == 0) | (blk2 == cur) | (blk2 == cur - 1)).astype(f32)
    score = jnp.where(blk2 <= cur, imp + SEL_BONUS * forced, -jnp.inf)
    sel = jnp.zeros((n_blk, TQ), f32)
    for _ in range(min(SEL_TOPK, n_blk)):
        mx = jnp.max(score, axis=0, keepdims=True)
        first = jnp.min(jnp.where(score == mx, blk2, n_blk), axis=0, keepdims=True)
        pick = blk2 == first
        sel = jnp.where(pick, 1.0, sel)
        score = jnp.where(pick, -jnp.inf, score)
    sel_b = sel.astype(bf16)

    m_ref[...] = jnp.full(m_ref.shape, NEG_BIG, f32)
    l_ref[...] = jnp.zeros(l_ref.shape, f32)
    acc_ref[...] = jnp.zeros(acc_ref.shape, f32)
    n_kt = (t0 + TQ + TK - 1) // TK

    def sel_tile(kt, carry):
        k0 = pl.multiple_of(kt * TK, TK)
        kpos = k0 + lax.broadcasted_iota(jnp.int32, (TK, 1), 0)
        expand = ((kpos // CMP_BLOCK) == lax.broadcasted_iota(jnp.int32, (TK, n_blk), 1)).astype(bf16)
        chosen = jnp.dot(expand, sel_b, preferred_element_type=f32)
        ok = (chosen > 0.5) & (kpos <= qpos1)
        bias1 = jnp.where(ok, 0.0, NEG_BIG)
        bias = jnp.concatenate([bias1] * J, axis=1)
        st = jnp.dot(ks_ref[0, 0, pl.ds(k0, TK), :], qs, preferred_element_type=f32) + bias
        m_old = m_ref[...]
        m_new = jnp.maximum(m_old, jnp.max(st, axis=0, keepdims=True))
        alpha = jnp.exp(m_old - m_new)
        pt = jnp.exp(st - m_new)
        l_ref[...] = alpha * l_ref[...] + jnp.sum(pt, axis=0, keepdims=True)
        acc_ref[...] = alpha * acc_ref[...] + jnp.dot(vst_ref[0, 0, :, pl.ds(k0, TK)], pt.astype(bf16),
                                                      preferred_element_type=f32)
        m_ref[...] = m_new
        return carry

    lax.fori_loop(0, n_kt, sel_tile, 0)
    o_sel = acc_ref[...] / l_ref[...]

    w0 = pl.multiple_of(jnp.maximum(t0 - WINDOW, 0), TQ)
    span = min(NSA_WSPAN, seq)
    kposw = w0 + lax.broadcasted_iota(jnp.int32, (span, 1), 0)
    rel = qpos1 - kposw
    biasw1 = jnp.where((rel >= 0) & (rel <= WINDOW), 0.0, NEG_BIG)
    biasw = jnp.concatenate([biasw1] * J, axis=1)
    sw = jnp.dot(kw_ref[0, 0, pl.ds(w0, span), :], qs, preferred_element_type=f32) + biasw
    pw = jnp.exp(sw - jnp.max(sw, axis=0, keepdims=True))
    lw = jnp.sum(pw, axis=0, keepdims=True)
    o_win = jnp.dot(vwt_ref[0, 0, :, pl.ds(w0, span)], pw.astype(bf16), preferred_element_type=f32) / lw

    def gate_row(br):
        g = jax.nn.sigmoid(gate_ref[0, 0, br])
        return jnp.concatenate([g[j:j + 1, :] for j in range(J)], axis=1)

    out = gate_row(0) * o_cmp + gate_row(1) * o_sel + gate_row(2) * o_win
    out = jnp.concatenate([out[:, j * TQ:(j + 1) * TQ] for j in range(J)], axis=0)
    o_ref[0] = out.T


def _nsa_prompt_attention(q, kv_c, kv_s, kv_w, nsa_g):
    b, T, _ = q.shape
    G, J, D, TQ = KV_HEADS, HEADS_PER_KV, ATTN_HEAD_DIM, NSA_TQ
    bf16 = jnp.bfloat16

    def split_kv(kv):
        r = kv.reshape(b, T, G, 2, D).astype(bf16)
        return jnp.transpose(r[:, :, :, 0], (0, 2, 1, 3)), jnp.transpose(r[:, :, :, 1], (0, 2, 3, 1))

    ks, vst = split_kv(kv_s)
    kw, vwt = split_kv(kv_w)
    gates_t = jnp.transpose(nsa_g.reshape(b, T, G, J, 3), (0, 2, 4, 3, 1))
    n_blk = T // CMP_BLOCK
    kern = functools.partial(_nsa_prompt_kernel, seq=T)
    return pl.pallas_call(
        kern,
        grid=(b, G, T // TQ),
        in_specs=[
            pl.BlockSpec((1, TQ, J * D), lambda bi, g, i: (bi, i, g)),
            pl.BlockSpec((1, T, 2 * D), lambda bi, g, i: (bi, 0, g)),
            pl.BlockSpec((1, 1, T, D), lambda bi, g, i: (bi, g, 0, 0)),
            pl.BlockSpec((1, 1, D, T), lambda bi, g, i: (bi, g, 0, 0)),
            pl.BlockSpec((1, 1, T, D), lambda bi, g, i: (bi, g, 0, 0)),
            pl.BlockSpec((1, 1, D, T), lambda bi, g, i: (bi, g, 0, 0)),
            pl.BlockSpec((1, 1, 3, J, TQ), lambda bi, g, i: (bi, g, 0, 0, i)),
        ],
        out_specs=pl.BlockSpec((1, TQ, J * D), lambda bi, g, i: (bi, i, g)),
        out_shape=jax.ShapeDtypeStruct((b, T, G * J * D), jnp.float32),
        scratch_shapes=[
            pltpu.VMEM((n_blk, D), bf16),
            pltpu.VMEM((D, n_blk), bf16),
            pltpu.VMEM((1, J * TQ), jnp.float32),
            pltpu.VMEM((1, J * TQ), jnp.float32),
            pltpu.VMEM((D, J * TQ), jnp.float32),
        ],
        compiler_params=pltpu.CompilerParams(
            dimension_semantics=("arbitrary", "arbitrary", "arbitrary"),
            vmem_limit_bytes=48 * 1024 * 1024),
        name="nsa_prompt",
    )(q, kv_c, ks, vst, kw, vwt, gates_t)


def _nsa_prompt(q, kv_c, kv_s, kv_w, nsa_g):
    b, T = q.shape[:2]
    win_new = kv_w.reshape(b, T, KV_HEADS, 2, ATTN_HEAD_DIM)[:, -min(WINDOW, T):]
    return _nsa_prompt_attention(q, kv_c, kv_s, kv_w, nsa_g), win_new


NSAS_PPS = 8


def _nsa_sample_kernel(pt_ref, qs_ref, gate_ref, *refs, n_pages, t_len, w_eff):
    f32, bf16 = jnp.float32, jnp.bfloat16
    PPS, G, D, J, T = NSAS_PPS, KV_HEADS, ATTN_HEAD_DIM, HEADS_PER_KV, t_len
    cmp_refs, sel_refs = refs[:PPS], refs[PPS:2 * PPS]
    tail_ref, win_ref, o_ref, pooled_ref, ocmp_ref, sel_ref, m_ref, l_ref, acc_ref = refs[2 * PPS:]
    C = J * T
    ph, pg = pl.program_id(1), pl.program_id(2)
    n_pg = n_pages // PPS
    past = n_pages * PAGE_SIZE
    n_blk = past // CMP_BLOCK
    n_sel_pad = sel_ref.shape[1]
    blk_per_page = PAGE_SIZE // CMP_BLOCK
    col_t = lax.broadcasted_iota(jnp.int32, (1, C), 1) % T
    qpos = past + col_t

    def queries(g):
        return (qs_ref[0, g] * ATTN_SCALE).astype(bf16)

    def flash(g, kv, ok):
        k, v = kv[:, :D], kv[:, D:]
        st = jnp.where(ok, jnp.dot(k.astype(bf16), queries(g), preferred_element_type=f32), NEG_BIG)
        m_old = m_ref[g]
        m_new = jnp.maximum(m_old, jnp.max(st, axis=0, keepdims=True))
        alpha = jnp.exp(m_old - m_new)
        p = jnp.exp(st - m_new)
        l_ref[g] = alpha * l_ref[g] + jnp.sum(p, axis=0, keepdims=True)
        acc_ref[g] = alpha * acc_ref[g] + jnp.dot(v.T.astype(bf16), p.astype(bf16), preferred_element_type=f32)
        m_ref[g] = m_new

    @pl.when(ph == 0)
    def _compressed():
        means = []
        for r in range(PPS):
            page = cmp_refs[r][0]
            means.append(jnp.sum(page.reshape(blk_per_page, CMP_BLOCK, G * 2 * D), axis=1) * (1.0 / CMP_BLOCK))
        rows = PPS * blk_per_page
        pooled_ref[pl.ds(pl.multiple_of(pg * rows, rows), rows), :] = jnp.concatenate(means, axis=0)

        @pl.when(pg == n_pg - 1)
        def _select():
            blk = lax.broadcasted_iota(jnp.int32, (n_blk, 1), 0)
            vis = (blk * CMP_BLOCK + (CMP_BLOCK - 1)) <= qpos
            fold = (lax.broadcasted_iota(jnp.int32, (C, T), 0) % T == lax.broadcasted_iota(jnp.int32, (C, T), 1))
            blk2 = lax.broadcasted_iota(jnp.int32, (n_sel_pad, T), 0)
            cur = (past + lax.broadcasted_iota(jnp.int32, (1, T), 1)) // SEL_BLOCK
            forced = ((blk2 == 0) | (blk2 == cur) | (blk2 == cur - 1)).astype(f32)
            for g in range(G):
                pooled = pooled_ref[:, g * 2 * D:(g + 1) * 2 * D]
                s = jnp.dot(pooled[:, :D].astype(bf16), queries(g), preferred_element_type=f32)
                s = jnp.where(vis, s, -jnp.inf)
                m = jnp.max(s, axis=0, keepdims=True)
                m = jnp.where(m == -jnp.inf, 0.0, m)
                p = jnp.exp(s - m)
                p = p / jnp.maximum(jnp.sum(p, axis=0, keepdims=True), 1e-30)
                ocmp_ref[g] = jnp.dot(pooled[:, D:].T.astype(bf16), p.astype(bf16), preferred_element_type=f32)
                imp = jnp.dot(p, fold.astype(f32), preferred_element_type=f32, precision=lax.Precision.HIGHEST)
                imp = jnp.concatenate([imp, jnp.zeros((n_sel_pad - n_blk, T), f32)], axis=0)
                score = jnp.where(blk2 <= cur, imp + SEL_BONUS * forced, -jnp.inf)
                sel = jnp.zeros((n_sel_pad, T), f32)
                for _ in range(SEL_TOPK):
                    mx = jnp.max(score, axis=0, keepdims=True)
                    first = jnp.min(jnp.where(score == mx, blk2, n_sel_pad), axis=0, keepdims=True)
                    pick = blk2 == first
                    sel = jnp.where(pick, 1.0, sel)
                    score = jnp.where(pick, -jnp.inf, score)
                sel_ref[g] = jnp.concatenate([sel] * J, axis=1)

    @pl.when(ph == 1)
    def _selected():
        @pl.when(pg == 0)
        def _init():
            m_ref[...] = jnp.full(m_ref.shape, NEG_BIG, f32)
            l_ref[...] = jnp.zeros(l_ref.shape, f32)
            acc_ref[...] = jnp.zeros(acc_ref.shape, f32)

        key = lax.broadcasted_iota(jnp.int32, (PAGE_SIZE, 1), 0)
        for r in range(PPS):
            page = sel_refs[r][0]
            first_blk = (pg * PPS + r) * blk_per_page
            for g in range(G):
                chosen = jnp.where(key < CMP_BLOCK, sel_ref[g, pl.ds(first_blk, 1), :], sel_ref[g, pl.ds(first_blk + 1, 1), :])
                flash(g, page[:, g * 2 * D:(g + 1) * 2 * D], chosen > 0.5)

        @pl.when(pg == n_pg - 1)
        def _finish():
            tail = tail_ref[0]
            wkeys = win_ref[0]
            span = wkeys.shape[0]
            wi = lax.broadcasted_iota(jnp.int32, (span, 1), 0)
            rel = qpos - (past - w_eff + wi)
            wok = (rel >= 0) & (rel <= WINDOW) & (wi < w_eff + T)
            for g in range(G):
                ok = (sel_ref[g, n_blk:n_blk + 1, :] > 0.5) & (key <= col_t) & (key < T)
                flash(g, tail[:, g * 2 * D:(g + 1) * 2 * D], ok)
                o_sel = acc_ref[g] / l_ref[g]
                kvw = wkeys[:, g * 2 * D:(g + 1) * 2 * D]
                sw = jnp.where(wok, jnp.dot(kvw[:, :D].astype(bf16), queries(g), preferred_element_type=f32), NEG_BIG)
                pw = jnp.exp(sw - jnp.max(sw, axis=0, keepdims=True))
                o_win = jnp.dot(kvw[:, D:].T.astype(bf16), pw.astype(bf16), preferred_element_type=f32)
                o_win = o_win / jnp.sum(pw, axis=0, keepdims=True)
                gates = jax.nn.sigmoid(gate_ref[0, g])
                o_ref[0, g] = gates[0:1] * ocmp_ref[g] + gates[1:2] * o_sel + gates[2:3] * o_win


def _nsa_sample(pool_cmp, pool_sel, win_buf, page_table, q, kv_cmp, kv_sel, kv_win, nsa_g):
    b, T, _ = q.shape
    assert T < CMP_BLOCK and page_table.shape[1] % NSAS_PPS == 0
    G, J, D, PPS = KV_HEADS, HEADS_PER_KV, ATTN_HEAD_DIM, NSAS_PPS
    n_pages = page_table.shape[1]
    n_pg = n_pages // PPS
    past = n_pages * PAGE_SIZE
    n_blk = past // CMP_BLOCK
    n_sel_pad = -(-(n_blk + 1) // 8) * 8
    w_eff = win_buf.shape[1]
    C = J * T
    n_pool = pool_cmp.shape[0]
    pool_cmp = pool_cmp.reshape(n_pool, PAGE_SIZE, KV_WIDTH)
    pool_sel = pool_sel.reshape(n_pool, PAGE_SIZE, KV_WIDTH)
    qs = jnp.transpose(q.reshape(b, T, G, J, D), (0, 2, 4, 3, 1)).reshape(b, G, D, C)
    gates = jnp.transpose(nsa_g.reshape(b, T, G, J, 3), (0, 2, 4, 3, 1)).reshape(b, G, 3, C)
    tail = jnp.pad(kv_sel, ((0, 0), (0, PAGE_SIZE - T), (0, 0)))
    keys = jnp.concatenate([win_buf.reshape(b, w_eff, KV_WIDTH), kv_win], axis=1)
    span = -(-(w_eff + T) // PAGE_SIZE) * PAGE_SIZE
    wkeys = jnp.pad(keys, ((0, 0), (0, span - (w_eff + T)), (0, 0)))

    def cmp_map(r):
        return lambda bi, ph, pg, pt: (pt[bi, jnp.where(ph == 0, pg, n_pg - 1) * PPS + r], 0, 0)

    def sel_map(r):
        return lambda bi, ph, pg, pt: (pt[bi, jnp.where(ph == 1, pg, 0) * PPS + r], 0, 0)

    page_block = (1, PAGE_SIZE, KV_WIDTH)
    per_b = lambda shape: pl.BlockSpec(shape, lambda bi, ph, pg, pt: (bi,) + (0,) * (len(shape) - 1))
    kern = functools.partial(_nsa_sample_kernel, n_pages=n_pages, t_len=T, w_eff=w_eff)
    out = pl.pallas_call(
        kern,
        grid_spec=pltpu.PrefetchScalarGridSpec(
            num_scalar_prefetch=1,
            grid=(b, 2, n_pg),
            in_specs=[per_b((1, G, D, C)), per_b((1, G, 3, C))]
                     + [pl.BlockSpec(page_block, cmp_map(r)) for r in range(PPS)]
                     + [pl.BlockSpec(page_block, sel_map(r)) for r in range(PPS)]
                     + [per_b((1, PAGE_SIZE, KV_WIDTH)), per_b((1, span, KV_WIDTH))],
            out_specs=per_b((1, G, D, C)),
            scratch_shapes=[pltpu.VMEM((n_blk, KV_WIDTH), jnp.float32),
                            pltpu.VMEM((G, D, C), jnp.float32),
                            pltpu.VMEM((G, n_sel_pad, C), jnp.float32),
                            pltpu.VMEM((G, 1, C), jnp.float32),
                            pltpu.VMEM((G, 1, C), jnp.float32),
                            pltpu.VMEM((G, D, C), jnp.float32)]),
        out_shape=jax.ShapeDtypeStruct((b, G, D, C), jnp.float32),
        compiler_params=pltpu.CompilerParams(dimension_semantics=("arbitrary", "arbitrary", "arbitrary"),
                                             vmem_limit_bytes=32 * 1024 * 1024),
        name="nsa_sample",
    )(page_table, qs, gates, *([pool_cmp] * PPS), *([pool_sel] * PPS), tail, wkeys)
    o_attn = jnp.transpose(out.reshape(b, G, D, J, T), (0, 4, 1, 3, 2)).reshape(b, T, ATTN_WIDTH)
    return o_attn, keys[:, -w_eff:].reshape(b, w_eff, KV_HEADS, 2, D)


PEER_PICKS = PEER_HEADS * PEER_TOPK
PEER_TT = 128
PEER_GT = 64
ROW_SUBLANES = D_MODEL // 128


def _top16_cols(s, n_rows, iota_rows):
    vals, rows = [], []
    for _ in range(PEER_TOPK):
        mx = jnp.max(s, axis=0, keepdims=True)
        first = jnp.min(jnp.where(s == mx, iota_rows, n_rows), axis=0, keepdims=True)
        vals.append(mx)
        rows.append(first)
        s = jnp.where(iota_rows == first, -jnp.inf, s)
    return jnp.concatenate(vals, axis=0), jnp.concatenate(rows, axis=0)


def _peer_route_kernel(h_ref, wq_ref, keys_ref, exp_ref, gate_ref):
    f32 = jnp.float32
    TT = PEER_TT
    q = jnp.dot(h_ref[...].astype(jnp.bfloat16), wq_ref[...], preferred_element_type=f32)
    iota_k = lax.broadcasted_iota(jnp.int32, (PEER_KEYS, TT), 0)
    iota_c = lax.broadcasted_iota(jnp.int32, (PEER_TOPK * PEER_TOPK, TT), 0)
    exp_rows, gate_rows = [], []
    for hd in range(PEER_HEADS):
        tops = []
        for c in range(2):
            j = hd * 2 + c
            qj = q[:, j * PEER_HALF:(j + 1) * PEER_HALF].astype(jnp.bfloat16)
            st = lax.dot_general(keys_ref[j], qj, (((1,), (1,)), ((), ())), preferred_element_type=f32)
            tops.append(_top16_cols(st, PEER_KEYS, iota_k))
        (s0, i0), (s1, i1) = tops
        cand_s = jnp.concatenate([s0[a:a + 1, :] + s1 for a in range(PEER_TOPK)], axis=0)
        cand_i = jnp.concatenate([i0[a:a + 1, :] * PEER_KEYS + i1 for a in range(PEER_TOPK)], axis=0)
        best_s, best_e = [], []
        s = cand_s
        for _ in range(PEER_TOPK):
            mx = jnp.max(s, axis=0, keepdims=True)
            first = jnp.min(jnp.where(s == mx, iota_c, PEER_TOPK * PEER_TOPK), axis=0, keepdims=True)
            pick = iota_c == first
            best_s.append(mx)
            best_e.append(jnp.sum(jnp.where(pick, cand_i, 0), axis=0, keepdims=True))
            s = jnp.where(pick, -jnp.inf, s)
        bs = jnp.concatenate(best_s, axis=0)
        be = jnp.concatenate(best_e, axis=0)
        ex = jnp.exp(bs - bs[0:1, :])
        gate_rows.append(ex / jnp.sum(ex, axis=0, keepdims=True))
        exp_rows.append(be)
    exp_ref[...] = _row_code(jnp.concatenate(exp_rows, axis=0)).T
    gate_ref[...] = jnp.concatenate(gate_rows, axis=0).T


def _peer_route(h, wq_b, keys_b):
    n = h.shape[0]
    TT = PEER_TT
    return pl.pallas_call(
        _peer_route_kernel,
        grid=(n // TT,),
        in_specs=[pl.BlockSpec((TT, D_MODEL), lambda i: (i, 0)),
                  pl.BlockSpec((D_MODEL, PEER_HEADS * PEER_QDIM), lambda i: (0, 0)),
                  pl.BlockSpec((PEER_HEADS * 2, PEER_KEYS, PEER_HALF), lambda i: (0, 0, 0))],
        out_specs=[pl.BlockSpec((TT, PEER_PICKS), lambda i: (i, 0)),
                   pl.BlockSpec((TT, PEER_PICKS), lambda i: (i, 0))],
        out_shape=[jax.ShapeDtypeStruct((n, PEER_PICKS), jnp.int32),
                   jax.ShapeDtypeStruct((n, PEER_PICKS), jnp.float32)],
        compiler_params=pltpu.CompilerParams(dimension_semantics=("arbitrary",),
                                             vmem_limit_bytes=40 * 1024 * 1024),
        name="peer_route",
    )(h, wq_b, keys_b)


def _pack_rows(w):
    b = lax.bitcast_convert_type(w.astype(jnp.bfloat16), jnp.uint16).astype(jnp.uint32)
    b = b.reshape(N_EXPERTS // 2, 2, D_MODEL)
    return ((b[:, 0] << 16) | b[:, 1]).reshape(N_EXPERTS // 2 * ROW_SUBLANES, 128)


def _row_code(e):
    return (e >> 1) * ROW_SUBLANES + (e & 1)


def _load_row(tab_ref, start, parity_word):
    w = tab_ref[pl.ds(pl.multiple_of(start, ROW_SUBLANES), ROW_SUBLANES), :]
    w = (w << ((parity_word & jnp.uint32(1)) << jnp.uint32(4))) & jnp.uint32(0xFFFF0000)
    return lax.bitcast_convert_type(w, jnp.float32)


def _splat_u32(s):
    return lax.bitcast_convert_type(jnp.full((ROW_SUBLANES, 128), s, jnp.int32), jnp.uint32)


def _sublane_sums(tiles):
    sub = lax.broadcasted_iota(jnp.int32, (ROW_SUBLANES, 128), 0)
    step = ROW_SUBLANES // 2
    while step >= 1:
        low = (sub % (2 * step)) < step
        tiles = [jnp.where(low, a + pltpu.roll(a, ROW_SUBLANES - step, 0), pltpu.roll(b, step, 0) + b)
                 for a, b in zip(tiles[:len(tiles) // 2], tiles[len(tiles) // 2:])]
        step //= 2
    return tiles[0]


PEER_GROUP = 16


PEER_GROUPS_PER_TOKEN = PEER_PICKS // PEER_GROUP


def _pipelined_groups(n_groups, produce, consume, stage_a, stage_b):
    def put(stage, tiles):
        for k, tile in enumerate(tiles):
            stage[k] = tile

    def get(stage):
        return [stage[k] for k in range(PEER_GROUP)]

    def body(j, carry):
        i = 2 * j + 1
        put(stage_b, produce(i))
        consume(i - 1, get(stage_a))
        put(stage_a, produce(i + 1))
        consume(i, get(stage_b))
        return carry

    put(stage_a, produce(0))
    lax.fori_loop(0, n_groups // 2 - 1, body, 0)
    put(stage_b, produce(n_groups - 1))
    consume(n_groups - 2, get(stage_a))
    consume(n_groups - 1, get(stage_b))


def _tree_sum(tiles):
    while len(tiles) > 1:
        tiles = [a + b for a, b in zip(tiles[0::2], tiles[1::2])]
    return tiles[0]


def _peer_up_kernel(code_ref, h_ref, gate_ref, tab_ref, coef_ref, dots_ref, stage_a, stage_b):
    GT = PEER_GT

    def products(i):
        t = i // PEER_GROUPS_PER_TOKEN
        h = h_ref[pl.ds(pl.multiple_of(t * ROW_SUBLANES, ROW_SUBLANES), ROW_SUBLANES), :]
        prods = []
        for kk in range(PEER_GROUP):
            code = code_ref[i * PEER_GROUP + kk]
            prods.append(_load_row(tab_ref, code & ~(ROW_SUBLANES - 1), _splat_u32(code)) * h)
        return tuple(prods)

    def reduce_rows(i, prods):
        rows = [_sublane_sums(list(prods[k:k + ROW_SUBLANES])) for k in range(0, PEER_GROUP, ROW_SUBLANES)]
        dots_ref[pl.ds(pl.multiple_of(i * PEER_GROUP, PEER_GROUP), PEER_GROUP), :] = jnp.concatenate(rows, axis=0)

    _pipelined_groups(GT * PEER_GROUPS_PER_TOKEN, products, reduce_rows, stage_a, stage_b)
    rows = [jnp.sum(dots_ref[t * PEER_PICKS:(t + 1) * PEER_PICKS, :].T, axis=0, keepdims=True) for t in range(GT)]
    coef_ref[...] = gate_ref[...] * jax.nn.gelu(jnp.concatenate(rows, axis=0))


def _peer_down_kernel(start_ref, cbits_ref, tab_ref, o_ref, part_ref, stage_a, stage_b):
    GT = PEER_GT

    def products(i):
        prods = []
        for kk in range(PEER_GROUP):
            cb = _splat_u32(cbits_ref[i * PEER_GROUP + kk])
            coef = lax.bitcast_convert_type(cb & jnp.uint32(0xFFFFFFFE), jnp.float32)
            prods.append(_load_row(tab_ref, start_ref[i * PEER_GROUP + kk], cb) * coef)
        return tuple(prods)

    def partial_sum(i, prods):
        part_ref[pl.ds(pl.multiple_of(i * ROW_SUBLANES, ROW_SUBLANES), ROW_SUBLANES), :] = _tree_sum(list(prods))

    _pipelined_groups(GT * PEER_GROUPS_PER_TOKEN, products, partial_sum, stage_a, stage_b)
    parts = part_ref[...].reshape(GT, PEER_GROUPS_PER_TOKEN, ROW_SUBLANES, 128)
    o_ref[...] = jnp.sum(parts, axis=1).reshape(GT * ROW_SUBLANES, 128)


def _peer_gather(h, experts, gates, u_tab, v_tab):
    n = h.shape[0]
    GT = PEER_GT
    tab_rows = N_EXPERTS // 2 * ROW_SUBLANES
    flat_smem = pl.BlockSpec((GT * PEER_PICKS,), lambda i: (i,), memory_space=pltpu.MemorySpace.SMEM)
    tab_spec = pl.BlockSpec((tab_rows, 128), lambda i: (0, 0), pipeline_mode=pl.Buffered(1))
    params = pltpu.CompilerParams(dimension_semantics=("arbitrary",), vmem_limit_bytes=56 * 1024 * 1024)
    codes = experts.reshape(n * PEER_PICKS)
    stage = pltpu.VMEM((PEER_GROUP, ROW_SUBLANES, 128), jnp.float32)
    coef = pl.pallas_call(
        _peer_up_kernel,
        grid=(n // GT,),
        in_specs=[flat_smem,
                  pl.BlockSpec((GT * ROW_SUBLANES, 128), lambda i: (i, 0)),
                  pl.BlockSpec((GT, PEER_PICKS), lambda i: (i, 0)),
                  tab_spec],
        out_specs=pl.BlockSpec((GT, PEER_PICKS), lambda i: (i, 0)),
        out_shape=jax.ShapeDtypeStruct((n, PEER_PICKS), jnp.float32),
        scratch_shapes=[pltpu.VMEM((GT * PEER_PICKS, 128), jnp.float32), stage, stage],
        compiler_params=params,
        name="peer_up",
    )(codes, h.reshape(n * ROW_SUBLANES, 128), gates, u_tab)
    out = pl.pallas_call(
        _peer_down_kernel,
        grid=(n // GT,),
        in_specs=[flat_smem, flat_smem, tab_spec],
        out_specs=pl.BlockSpec((GT * ROW_SUBLANES, 128), lambda i: (i, 0)),
        out_shape=jax.ShapeDtypeStruct((n * ROW_SUBLANES, 128), jnp.float32),
        scratch_shapes=[pltpu.VMEM((GT * PEER_GROUPS_PER_TOKEN * ROW_SUBLANES, 128), jnp.float32), stage, stage],
        compiler_params=params,
        name="peer_down",
    )(codes & ~(ROW_SUBLANES - 1),
      (lax.bitcast_convert_type(coef, jnp.int32).reshape(n * PEER_PICKS) & ~1) | (codes & 1), v_tab)
    return out.reshape(n, D_MODEL)


def _peer_tokens(h, peer_wq, peer_sub_keys, u_tab, v_tab):
    shape = h.shape
    hf = h.reshape(-1, D_MODEL)
    wq_b = peer_wq.astype(jnp.bfloat16)
    keys_b = peer_sub_keys.reshape(PEER_HEADS * 2, PEER_KEYS, PEER_HALF).astype(jnp.bfloat16)
    experts, gates = _peer_route(hf, wq_b, keys_b)
    return _peer_gather(hf, experts, gates, u_tab, v_tab).reshape(shape)


def _layer_weights(w_in, w_ssm_branch, w_attn_branch, w_out, peer_wq, peer_sub_keys, peer_u, peer_v):
    bf16 = jnp.bfloat16
    offs = np.cumsum((0,) + PROJ_SIZES).tolist()
    seg = lambda k: w_in[:, offs[k]:offs[k + 1]]
    pad = jnp.zeros((D_MODEL, SMALL_WIDTH - SSM_HEADS - 3 * ATTN_HEADS), w_in.dtype)
    return dict(
        in_ssm=[seg(0).astype(bf16), seg(1).astype(bf16)],
        in_attn=[seg(3).astype(bf16), jnp.concatenate([seg(4), seg(5), seg(6)], axis=1).astype(bf16),
                 seg(8).astype(bf16), jnp.concatenate([seg(2), seg(7), pad], axis=1).astype(bf16)],
        w_ssm=w_ssm_branch.astype(bf16), w_attn=w_attn_branch.astype(bf16), w_out=w_out.astype(bf16),
        wq=peer_wq.astype(bf16),
        keys=peer_sub_keys.reshape(PEER_HEADS * 2, PEER_KEYS, PEER_HALF).astype(bf16),
        u_tab=_pack_rows(peer_u), v_tab=_pack_rows(peer_v))


def _block(x, conv_prev, ssm_prev, nsa_core, lw, norm_mix_w, conv_w, conv_b, dt_bias, a_log, d_skip, ssm_norm_w,
           norm_ffn_w):
    b, T, _ = x.shape
    n = b * T
    xf = x.reshape(n, D_MODEL)
    z, xbc = _norm_proj(xf, norm_mix_w, lw["in_ssm"], "in_proj_ssm")
    q, kv, br_g, small = _norm_proj(xf, norm_mix_w, lw["in_attn"], "in_proj_attn")
    nsa_g = small[:, SSM_HEADS:SSM_HEADS + 3 * ATTN_HEADS]
    kv_c, kv_s, kv_w = (kv[:, k * KV_WIDTH:(k + 1) * KV_WIDTH].reshape(b, T, KV_WIDTH) for k in range(3))
    y_ssm, conv_new, ssm_new = _mamba_branch(z.reshape(b, T, D_INNER), xbc.reshape(b, T, CONV_DIM),
                                             small.reshape(b, T, SMALL_WIDTH), conv_prev, ssm_prev, conv_w, conv_b,
                                             dt_bias, a_log, d_skip, ssm_norm_w)
    o_attn, win_new = nsa_core(q.reshape(b, T, ATTN_WIDTH), kv_c, kv_s, kv_w, nsa_g.reshape(b, T, 3 * ATTN_HEADS))
    x2, h2 = _merge_proj(xf, o_attn.reshape(n, ATTN_WIDTH), y_ssm.reshape(n, D_INNER), br_g, norm_ffn_w,
                         lw["w_attn"], lw["w_ssm"], lw["w_out"])
    experts, gates = _peer_route(h2, lw["wq"], lw["keys"])
    peer = _peer_gather(h2, experts, gates, lw["u_tab"], lw["v_tab"])
    kv_shape = (b, T, KV_HEADS, 2, ATTN_HEAD_DIM)
    return x2, peer, kv_c.reshape(kv_shape), kv_s.reshape(kv_shape), win_new, conv_new, ssm_new


def kernel(x_prompt, x_sample, cache_cmp_kv, cache_sel_kv, cache_win_kv, state_ssm, state_conv, page_table,
           norm_mix_w, w_in, conv_w, conv_b, dt_bias, a_log, d_skip, ssm_norm_w, w_ssm_branch, w_attn_branch,
           w_out, norm_ffn_w, peer_wq, peer_sub_keys, peer_u, peer_v, final_norm_w):
    xp, xs = x_prompt, x_sample
    cmp_p, cmp_s, sel_p, sel_s, win_p, win_s = [], [], [], [], [], []
    ssm_p, ssm_s, conv_p, conv_s = [], [], [], []
    for layer in range(DEPTH):
        lw = _layer_weights(*(a[layer] for a in (w_in, w_ssm_branch, w_attn_branch, w_out, peer_wq, peer_sub_keys,
                                                   peer_u, peer_v)))
        rest = [a[layer] for a in (norm_mix_w, conv_w, conv_b, dt_bias, a_log, d_skip, ssm_norm_w, norm_ffn_w)]
        last = layer == DEPTH - 1
        bp = xp.shape[0]
        conv0 = jnp.zeros((bp, CONV_WIDTH - 1, CONV_DIM), xp.dtype)
        ssm0 = jnp.zeros((bp, SSM_HEADS, SSM_HEAD_DIM, SSM_STATE), xp.dtype)
        x2, peer, kc, ks, kw, cv, ss = _block(xp, conv0, ssm0, _nsa_prompt, lw, *rest)
        xp = (_add_norm(x2, peer, final_norm_w) if last else x2 + peer).reshape(xp.shape)
        cmp_p.append(kc); sel_p.append(ks); win_p.append(kw); conv_p.append(cv); ssm_p.append(ss)
        core = functools.partial(_nsa_sample, cache_cmp_kv[layer], cache_sel_kv[layer], cache_win_kv[layer], page_table)
        x2, peer, kc, ks, kw, cv, ss = _block(xs, state_conv[layer], state_ssm[layer], core, lw, *rest)
        xs = (_add_norm(x2, peer, final_norm_w) if last else x2 + peer).reshape(xs.shape)
        cmp_s.append(kc); sel_s.append(ks); win_s.append(kw); conv_s.append(cv); ssm_s.append(ss)
    return (xp, xs, jnp.stack(cmp_p), jnp.stack(cmp_s), jnp.stack(sel_p), jnp.stack(sel_s),
            jnp.stack(win_p), jnp.stack(win_s), jnp.stack(ssm_p), jnp.stack(ssm_s), jnp.stack(conv_p), jnp.stack(conv_s))
```

```python
import math, functools
import jax, jax.numpy as jnp
from jax import lax
import numpy as np
from jax.experimental import pallas as pl
from jax.experimental.pallas import tpu as pltpu

D_MODEL = 1024
BATCH = 8
SEQ = 4096
DEPTH = 1
DEC_BATCH = 32
DEC_SEQ = 8
PAST_LEN = 16384
PAGE_SIZE = 128

SSM_EXPAND = 2
D_INNER = SSM_EXPAND * D_MODEL
SSM_HEAD_DIM = 64
SSM_HEADS = D_INNER // SSM_HEAD_DIM
SSM_GROUPS = 4
SSM_HPG = SSM_HEADS // SSM_GROUPS
SSM_STATE = 128
CONV_WIDTH = 4
CONV_DIM = D_INNER + 2 * SSM_GROUPS * SSM_STATE
SSD_CHUNK = 128
ATTN_HEADS = 16
ATTN_HEAD_DIM = 64
ATTN_WIDTH = ATTN_HEADS * ATTN_HEAD_DIM
KV_HEADS = 2
HEADS_PER_KV = ATTN_HEADS // KV_HEADS
KV_WIDTH = 2 * KV_HEADS * ATTN_HEAD_DIM
CMP_BLOCK = 64
SEL_BLOCK = CMP_BLOCK
SEL_TOPK = 16
SEL_BONUS = 1000.0
WINDOW = 512
WIN_QBLOCK = 128
SEL_QBLOCK = 32
ATTN_SCALE = ATTN_HEAD_DIM ** -0.5
PEER_HEADS = 8
PEER_KEYS = 128
N_EXPERTS = PEER_KEYS * PEER_KEYS
PEER_QDIM = 256
PEER_HALF = PEER_QDIM // 2
PEER_TOPK = 16
PEER_TOKEN_CHUNK = 128
EPS = 1e-6
PROJ_SIZES = (D_INNER, CONV_DIM, SSM_HEADS, ATTN_WIDTH, KV_WIDTH, KV_WIDTH, KV_WIDTH, 3 * ATTN_HEADS, 2 * D_MODEL)
PROJ_DIM = sum(PROJ_SIZES)


PROJ_TM = 512
PROJ_VMEM_BYTES = 56 * 1024 * 1024


def _rms(x, w):
    return x * lax.rsqrt(jnp.mean(x * x, axis=-1, keepdims=True) + EPS) * w


def _resident(shape):
    return pl.BlockSpec(shape, lambda i: (0,) * len(shape), pipeline_mode=pl.Buffered(1))


def _norm_proj_kernel(x_ref, nw_ref, *refs):
    n_out = len(refs) // 2
    h = _rms(x_ref[...], nw_ref[...]).astype(jnp.bfloat16)
    for w_ref, o_ref in zip(refs[:n_out], refs[n_out:]):
        o_ref[...] = jnp.dot(h, w_ref[...], preferred_element_type=jnp.float32)


def _norm_proj(x, norm_w, weights, name):
    n = x.shape[0]
    tm = min(PROJ_TM, n)
    return pl.pallas_call(
        _norm_proj_kernel,
        grid=(n // tm,),
        in_specs=[pl.BlockSpec((tm, D_MODEL), lambda i: (i, 0)), _resident((1, D_MODEL))]
                 + [_resident(w.shape) for w in weights],
        out_specs=[pl.BlockSpec((tm, w.shape[1]), lambda i: (i, 0)) for w in weights],
        out_shape=[jax.ShapeDtypeStruct((n, w.shape[1]), jnp.float32) for w in weights],
        compiler_params=pltpu.CompilerParams(dimension_semantics=("arbitrary",), vmem_limit_bytes=PROJ_VMEM_BYTES),
        name=name,
    )(x, norm_w.reshape(1, D_MODEL), *weights)


def _merge_proj_kernel(x_ref, oa_ref, ys_ref, g_ref, nw_ref, wa_ref, ws_ref, wo_ref, x2_ref, h2_ref):
    f32, bf16 = jnp.float32, jnp.bfloat16
    y_attn = jnp.dot(oa_ref[...].astype(bf16), wa_ref[...], preferred_element_type=f32)
    y_ssm = jnp.dot(ys_ref[...].astype(bf16), ws_ref[...], preferred_element_type=f32)
    g = jax.nn.sigmoid(g_ref[...])
    merged = g[:, :D_MODEL] * y_ssm + g[:, D_MODEL:] * y_attn
    x2 = x_ref[...] + jnp.dot(merged.astype(bf16), wo_ref[...], preferred_element_type=f32)
    x2_ref[...] = x2
    h2_ref[...] = _rms(x2, nw_ref[...])


def _merge_proj(x, o_attn, y_ssm, br_g, norm_w, wa_b, ws_b, wo_b):
    n = x.shape[0]
    tm = min(PROJ_TM, n)
    rows = lambda c: pl.BlockSpec((tm, c), lambda i: (i, 0))
    return pl.pallas_call(
        _merge_proj_kernel,
        grid=(n // tm,),
        in_specs=[rows(D_MODEL), rows(ATTN_WIDTH), rows(D_INNER), rows(2 * D_MODEL), _resident((1, D_MODEL)),
                  _resident(wa_b.shape), _resident(ws_b.shape), _resident(wo_b.shape)],
        out_specs=[rows(D_MODEL), rows(D_MODEL)],
        out_shape=[jax.ShapeDtypeStruct((n, D_MODEL), jnp.float32)] * 2,
        compiler_params=pltpu.CompilerParams(dimension_semantics=("arbitrary",), vmem_limit_bytes=PROJ_VMEM_BYTES),
        name="merge_proj",
    )(x, o_attn, y_ssm, br_g, norm_w.reshape(1, D_MODEL), wa_b, ws_b, wo_b)


def _add_norm_kernel(x_ref, y_ref, w_ref, o_ref):
    o_ref[...] = _rms(x_ref[...] + y_ref[...], w_ref[...])


def _add_norm(x, y, w):
    n = x.shape[0]
    tm = min(PROJ_TM, n)
    rows = pl.BlockSpec((tm, D_MODEL), lambda i: (i, 0))
    return pl.pallas_call(
        _add_norm_kernel,
        grid=(n // tm,),
        in_specs=[rows, rows, _resident((1, D_MODEL))],
        out_specs=rows,
        out_shape=jax.ShapeDtypeStruct((n, D_MODEL), jnp.float32),
        name="add_norm",
    )(x, y, w.reshape(1, D_MODEL))


SMALL_WIDTH = 128
CONV_TAIL = 8


def _ssd_kernel(z_ref, xbc_ref, sm_ref, cprev_ref, h0_ref, cw_ref, cbias_ref, dtb_ref, alog_ref, dskip_ref, nw_ref,
                y_ref, hout_ref, state_ref, tail_ref, *, valid):
    f32, bf16 = jnp.float32, jnp.bfloat16
    L, P, N = SSD_CHUNK, SSM_HEAD_DIM, SSM_STATE
    c = pl.program_id(1)

    @pl.when(c == 0)
    def _init():
        state_ref[...] = h0_ref[0]
        tail_ref[...] = cprev_ref[0]

    xbc = xbc_ref[0]
    padded = jnp.concatenate([tail_ref[...], xbc], axis=0)
    acc = jnp.broadcast_to(cbias_ref[...], (L, CONV_DIM))
    for w in range(CONV_WIDTH):
        lo = CONV_TAIL - (CONV_WIDTH - 1) + w
        acc = acc + cw_ref[w:w + 1, :] * padded[lo:lo + L, :]
    xc = acc * jax.nn.sigmoid(acc)
    tail_ref[...] = xbc[L - CONV_TAIL:, :]

    row = lax.broadcasted_iota(jnp.int32, (L, 1), 0)
    live = row < (valid - c * L)
    dt = jnp.where(live, jax.nn.softplus(sm_ref[0] + dtb_ref[...]), 0.0)
    a = dt * (-jnp.exp(alog_ref[...]))
    tril = (row >= lax.broadcasted_iota(jnp.int32, (L, L), 1))
    a_cum = jnp.dot(tril.astype(f32), a, preferred_element_type=f32, precision=lax.Precision.HIGHEST)
    a_cum_t = a_cum.T
    a_last = a_cum[L - 1:L, :]
    grow = jnp.exp(a_cum)
    to_end = jnp.exp(a_last - a_cum)
    carry = jnp.exp(a_last)

    xs = xc[:, :D_INNER]
    ys = []
    for g in range(SSM_GROUPS):
        bg = xc[:, D_INNER + g * N:D_INNER + (g + 1) * N]
        cg = xc[:, D_INNER + SSM_GROUPS * N + g * N:D_INNER + SSM_GROUPS * N + (g + 1) * N]
        bg_b, cg_b = bg.astype(bf16), cg.astype(bf16)
        bg_t = bg.T.astype(bf16)
        cb = lax.dot_general(cg_b, bg_b, (((1,), (1,)), ((), ())), preferred_element_type=f32)
        for j in range(SSM_HPG):
            h = g * SSM_HPG + j
            diff = a_cum[:, h:h + 1] - a_cum_t[h:h + 1, :]
            m = (cb * jnp.exp(jnp.where(tril, diff, -jnp.inf))).astype(bf16)
            x_h = xs[:, h * P:(h + 1) * P]
            xdt = x_h * dt[:, h:h + 1]
            st = state_ref[h]
            y = jnp.dot(m, xdt.astype(bf16), preferred_element_type=f32)
            y = y + jnp.dot(cg_b, st.astype(bf16), preferred_element_type=f32) * grow[:, h:h + 1]
            state_ref[h] = st * carry[:, h:h + 1] + jnp.dot(bg_t, (xdt * to_end[:, h:h + 1]).astype(bf16),
                                                             preferred_element_type=f32)
            ys.append(y + dskip_ref[:, h:h + 1] * x_h)
    z = z_ref[0]
    y_ref[0] = _rms(jnp.concatenate(ys, axis=1) * (z * jax.nn.sigmoid(z)), nw_ref[...])

    @pl.when(c == pl.num_programs(1) - 1)
    def _emit():
        hout_ref[0] = state_ref[...]


def _mamba_branch(z, xbc, small, conv_prev, ssm_prev, conv_w, conv_b, dt_bias, a_log, d_skip, ssm_norm_w):
    b, L, _ = xbc.shape
    Lc = SSD_CHUNK
    Lp = -(-L // Lc) * Lc
    keep = CONV_WIDTH - 1
    conv_new = xbc[:, L - keep:] if L >= keep else jnp.concatenate([conv_prev, xbc], axis=1)[:, -keep:]
    if Lp != L:
        padrows = lambda a: jnp.pad(a, ((0, 0), (0, Lp - L), (0, 0)))
        z, xbc, small = padrows(z), padrows(xbc), padrows(small)
    cprev = jnp.pad(conv_prev, ((0, 0), (CONV_TAIL - (CONV_WIDTH - 1), 0), (0, 0)))
    h0 = jnp.swapaxes(ssm_prev, 2, 3)
    lane = lambda v: jnp.pad(v.reshape(1, SSM_HEADS), ((0, 0), (0, SMALL_WIDTH - SSM_HEADS)))
    full = lambda shape: pl.BlockSpec(shape, lambda bi, c: (0,) * len(shape))
    kern = functools.partial(_ssd_kernel, valid=L)
    y, hout = pl.pallas_call(
        kern,
        grid=(b, Lp // Lc),
        in_specs=[pl.BlockSpec((1, Lc, D_INNER), lambda bi, c: (bi, c, 0)),
                  pl.BlockSpec((1, Lc, CONV_DIM), lambda bi, c: (bi, c, 0)),
                  pl.BlockSpec((1, Lc, SMALL_WIDTH), lambda bi, c: (bi, c, 0)),
                  pl.BlockSpec((1, CONV_TAIL, CONV_DIM), lambda bi, c: (bi, 0, 0)),
                  pl.BlockSpec((1, SSM_HEADS, SSM_STATE, SSM_HEAD_DIM), lambda bi, c: (bi, 0, 0, 0)),
                  full((CONV_WIDTH, CONV_DIM)), full((1, CONV_DIM)), full((1, SMALL_WIDTH)), full((1, SMALL_WIDTH)),
                  full((1, SMALL_WIDTH)), full((1, D_INNER))],
        out_specs=[pl.BlockSpec((1, Lc, D_INNER), lambda bi, c: (bi, c, 0)),
                   pl.BlockSpec((1, SSM_HEADS, SSM_STATE, SSM_HEAD_DIM), lambda bi, c: (bi, 0, 0, 0))],
        out_shape=[jax.ShapeDtypeStruct((b, Lp, D_INNER), jnp.float32),
                   jax.ShapeDtypeStruct((b, SSM_HEADS, SSM_STATE, SSM_HEAD_DIM), jnp.float32)],
        scratch_shapes=[pltpu.VMEM((SSM_HEADS, SSM_STATE, SSM_HEAD_DIM), jnp.float32),
                        pltpu.VMEM((CONV_TAIL, CONV_DIM), jnp.float32)],
        compiler_params=pltpu.CompilerParams(dimension_semantics=("arbitrary", "arbitrary"),
                                             vmem_limit_bytes=48 * 1024 * 1024),
        name="mamba_ssd",
    )(z, xbc, small, cprev, h0, conv_w, conv_b.reshape(1, CONV_DIM), lane(dt_bias), lane(a_log), lane(d_skip),
      ssm_norm_w.reshape(1, D_INNER))
    return y[:, :L], conv_new, jnp.swapaxes(hout, 2, 3)


NSA_TQ = 128
NSA_TK = 512
NSA_WSPAN = WINDOW + NSA_TQ
NEG_BIG = -1e30


def _nsa_prompt_kernel(q_ref, kvc_ref, ks_ref, vst_ref, kw_ref, vwt_ref, gate_ref, o_ref,
                       pk_ref, pvt_ref, m_ref, l_ref, acc_ref, *, seq):
    f32, bf16 = jnp.float32, jnp.bfloat16
    D, J, TQ, TK = ATTN_HEAD_DIM, HEADS_PER_KV, NSA_TQ, NSA_TK
    n_blk = seq // CMP_BLOCK
    i = pl.program_id(2)
    t0 = i * TQ

    @pl.when(i == 0)
    def _pool():
        c = kvc_ref[0]
        pooled = jnp.sum(c.reshape(n_blk, CMP_BLOCK, 2 * D), axis=1) * (1.0 / CMP_BLOCK)
        pk_ref[...] = pooled[:, :D].astype(bf16)
        pvt_ref[...] = pooled[:, D:].T.astype(bf16)

    qt = (q_ref[0] * ATTN_SCALE).T
    qs = jnp.concatenate([qt[j * D:(j + 1) * D, :] for j in range(J)], axis=1).astype(bf16)
    qpos1 = t0 + lax.broadcasted_iota(jnp.int32, (1, TQ), 1)
    qpos = jnp.concatenate([qpos1] * J, axis=1)

    s = jnp.dot(pk_ref[...], qs, preferred_element_type=f32)
    blk = lax.broadcasted_iota(jnp.int32, (n_blk, 1), 0)
    vis = (blk * CMP_BLOCK + (CMP_BLOCK - 1)) <= qpos
    s = jnp.where(vis, s, -jnp.inf)
    m = jnp.max(s, axis=0, keepdims=True)
    m = jnp.where(m == -jnp.inf, 0.0, m)
    p = jnp.exp(s - m)
    p = p / jnp.maximum(jnp.sum(p, axis=0, keepdims=True), 1e-30)
    o_cmp = jnp.dot(pvt_ref[...], p.astype(bf16), preferred_element_type=f32)
    imp = p[:, 0:TQ]
    for j in range(1, J):
        imp = imp + p[:, j * TQ:(j + 1) * TQ]

    cur = qpos1 // CMP_BLOCK
    blk2 = lax.broadcasted_iota(jnp.int32, (n_blk, TQ), 0)
    forced = ((blk2 == 0) | (blk2 == cur) | (blk2 == cur - 1)).astype(f32)
    score = jnp.where(blk2 <= cur, imp + SEL_BONUS * forced, -jnp.inf)
    sel = jnp.zeros((n_blk, TQ), f32)
    for _ in range(min(SEL_TOPK, n_blk)):
        mx = jnp.max(score, axis=0, keepdims=True)
        first = jnp.min(jnp.where(score == mx, blk2, n_blk), axis=0, keepdims=True)
        pick = blk2 == first
        sel = jnp.where(pick, 1.0, sel)
        score = jnp.where(pick, -jnp.inf, score)
    sel_b = sel.astype(bf16)

    m_ref[...] = jnp.full(m_ref.shape, NEG_BIG, f32)
    l_ref[...] = jnp.zeros(l_ref.shape, f32)
    acc_ref[...] = jnp.zeros(acc_ref.shape, f32)
    n_kt = (t0 + TQ + TK - 1) // TK

    def sel_tile(kt, carry):
        k0 = pl.multiple_of(kt * TK, TK)
        kpos = k0 + lax.broadcasted_iota(jnp.int32, (TK, 1), 0)
        expand = ((kpos // CMP_BLOCK) == lax.broadcasted_iota(jnp.int32, (TK, n_blk), 1)).astype(bf16)
        chosen = jnp.dot(expand, sel_b, preferred_element_type=f32)
        ok = (chosen > 0.5) & (kpos <= qpos1)
        bias1 = jnp.where(ok, 0.0, NEG_BIG)
        bias = jnp.concatenate([bias1] * J, axis=1)
        st = jnp.dot(ks_ref[0, 0, pl.ds(k0, TK), :], qs, preferred_element_type=f32) + bias
        m_old = m_ref[...]
        m_new = jnp.maximum(m_old, jnp.max(st, axis=0, keepdims=True))
        alpha = jnp.exp(m_old - m_new)
        pt = jnp.exp(st - m_new)
        l_ref[...] = alpha * l_ref[...] + jnp.sum(pt, axis=0, keepdims=True)
        acc_ref[...] = alpha * acc_ref[...] + jnp.dot(vst_ref[0, 0, :, pl.ds(k0, TK)], pt.astype(bf16),
                                                      preferred_element_type=f32)
        m_ref[...] = m_new
        return carry

    lax.fori_loop(0, n_kt, sel_tile, 0)
    o_sel = acc_ref[...] / l_ref[...]

    w0 = pl.multiple_of(jnp.maximum(t0 - WINDOW, 0), TQ)
    span = min(NSA_WSPAN, seq)
    kposw = w0 + lax.broadcasted_iota(jnp.int32, (span, 1), 0)
    rel = qpos1 - kposw
    biasw1 = jnp.where((rel >= 0) & (rel <= WINDOW), 0.0, NEG_BIG)
    biasw = jnp.concatenate([biasw1] * J, axis=1)
    sw = jnp.dot(kw_ref[0, 0, pl.ds(w0, span), :], qs, preferred_element_type=f32) + biasw
    pw = jnp.exp(sw - jnp.max(sw, axis=0, keepdims=True))
    lw = jnp.sum(pw, axis=0, keepdims=True)
    o_win = jnp.dot(vwt_ref[0, 0, :, pl.ds(w0, span)], pw.astype(bf16), preferred_element_type=f32) / lw

    def gate_row(br):
        g = jax.nn.sigmoid(gate_ref[0, 0, br])
        return jnp.concatenate([g[j:j + 1, :] for j in range(J)], axis=1)

    out = gate_row(0) * o_cmp + gate_row(1) * o_sel + gate_row(2) * o_win
    out = jnp.concatenate([out[:, j * TQ:(j + 1) * TQ] for j in range(J)], axis=0)
    o_ref[0] = out.T


def _nsa_prompt_attention(q, kv_c, kv_s, kv_w, nsa_g):
    b, T, _ = q.shape
    G, J, D, TQ = KV_HEADS, HEADS_PER_KV, ATTN_HEAD_DIM, NSA_TQ
    bf16 = jnp.bfloat16

    def split_kv(kv):
        r = kv.reshape(b, T, G, 2, D).astype(bf16)
        return jnp.transpose(r[:, :, :, 0], (0, 2, 1, 3)), jnp.transpose(r[:, :, :, 1], (0, 2, 3, 1))

    ks, vst = split_kv(kv_s)
    kw, vwt = split_kv(kv_w)
    gates_t = jnp.transpose(nsa_g.reshape(b, T, G, J, 3), (0, 2, 4, 3, 1))
    n_blk = T // CMP_BLOCK
    kern = functools.partial(_nsa_prompt_kernel, seq=T)
    return pl.pallas_call(
        kern,
        grid=(b, G, T // TQ),
        in_specs=[
            pl.BlockSpec((1, TQ, J * D), lambda bi, g, i: (bi, i, g)),
            pl.BlockSpec((1, T, 2 * D), lambda bi, g, i: (bi, 0, g)),
            pl.BlockSpec((1, 1, T, D), lambda bi, g, i: (bi, g, 0, 0)),
            pl.BlockSpec((1, 1, D, T), lambda bi, g, i: (bi, g, 0, 0)),
            pl.BlockSpec((1, 1, T, D), lambda bi, g, i: (bi, g, 0, 0)),
            pl.BlockSpec((1, 1, D, T), lambda bi, g, i: (bi, g, 0, 0)),
            pl.BlockSpec((1, 1, 3, J, TQ), lambda bi, g, i: (bi, g, 0, 0, i)),
        ],
        out_specs=pl.BlockSpec((1, TQ, J * D), lambda bi, g, i: (bi, i, g)),
        out_shape=jax.ShapeDtypeStruct((b, T, G * J * D), jnp.float32),
        scratch_shapes=[
            pltpu.VMEM((n_blk, D), bf16),
            pltpu.VMEM((D, n_blk), bf16),
            pltpu.VMEM((1, J * TQ), jnp.float32),
            pltpu.VMEM((1, J * TQ), jnp.float32),
            pltpu.VMEM((D, J * TQ), jnp.float32),
        ],
        compiler_params=pltpu.CompilerParams(
            dimension_semantics=("arbitrary", "arbitrary", "arbitrary"),
            vmem_limit_bytes=48 * 1024 * 1024),
        name="nsa_prompt",
    )(q, kv_c, ks, vst, kw, vwt, gates_t)


def _nsa_prompt(q, kv_c, kv_s, kv_w, nsa_g):
    b, T = q.shape[:2]
    win_new = kv_w.reshape(b, T, KV_HEADS, 2, ATTN_HEAD_DIM)[:, -min(WINDOW, T):]
    return _nsa_prompt_attention(q, kv_c, kv_s, kv_w, nsa_g), win_new


NSAS_PPS = 8


def _nsa_sample_kernel(pt_ref, qs_ref, gate_ref, *refs, n_pages, t_len, w_eff):
    f32, bf16 = jnp.float32, jnp.bfloat16
    PPS, G, D, J, T = NSAS_PPS, KV_HEADS, ATTN_HEAD_DIM, HEADS_PER_KV, t_len
    cmp_refs, sel_refs = refs[:PPS], refs[PPS:2 * PPS]
    tail_ref, win_ref, o_ref, pooled_ref, ocmp_ref, sel_ref, m_ref, l_ref, acc_ref = refs[2 * PPS:]
    C = J * T
    ph, pg = pl.program_id(1), pl.program_id(2)
    n_pg = n_pages // PPS
    past = n_pages * PAGE_SIZE
    n_blk = past // CMP_BLOCK
    n_sel_pad = sel_ref.shape[1]
    blk_per_page = PAGE_SIZE // CMP_BLOCK
    col_t = lax.broadcasted_iota(jnp.int32, (1, C), 1) % T
    qpos = past + col_t

    def queries(g):
        return (qs_ref[0, g] * ATTN_SCALE).astype(bf16)

    def flash(g, kv, ok):
        k, v = kv[:, :D], kv[:, D:]
        st = jnp.where(ok, jnp.dot(k.astype(bf16), queries(g), preferred_element_type=f32), NEG_BIG)
        m_old = m_ref[g]
        m_new = jnp.maximum(m_old, jnp.max(st, axis=0, keepdims=True))
        alpha = jnp.exp(m_old - m_new)
        p = jnp.exp(st - m_new)
        l_ref[g] = alpha * l_ref[g] + jnp.sum(p, axis=0, keepdims=True)
        acc_ref[g] = alpha * acc_ref[g] + jnp.dot(v.T.astype(bf16), p.astype(bf16), preferred_element_type=f32)
        m_ref[g] = m_new

    @pl.when(ph == 0)
    def _compressed():
        means = []
        for r in range(PPS):
            page = cmp_refs[r][0]
            means.append(jnp.sum(page.reshape(blk_per_page, CMP_BLOCK, G * 2 * D), axis=1) * (1.0 / CMP_BLOCK))
        rows = PPS * blk_per_page
        pooled_ref[pl.ds(pl.multiple_of(pg * rows, rows), rows), :] = jnp.concatenate(means, axis=0)

        @pl.when(pg == n_pg - 1)
        def _select():
            blk = lax.broadcasted_iota(jnp.int32, (n_blk, 1), 0)
            vis = (blk * CMP_BLOCK + (CMP_BLOCK - 1)) <= qpos
            fold = (lax.broadcasted_iota(jnp.int32, (C, T), 0) % T == lax.broadcasted_iota(jnp.int32, (C, T), 1))
            blk2 = lax.broadcasted_iota(jnp.int32, (n_sel_pad, T), 0)
            cur = (past + lax.broadcasted_iota(jnp.int32, (1, T), 1)) // SEL_BLOCK
            forced = ((blk2 == 0) | (blk2 == cur) | (blk2 == cur - 1)).astype(f32)
            for g in range(G):
                pooled = pooled_ref[:, g * 2 * D:(g + 1) * 2 * D]
                s = jnp.dot(pooled[:, :D].astype(bf16), queries(g), preferred_element_type=f32)
                s = jnp.where(vis, s, -jnp.inf)
                m = jnp.max(s, axis=0, keepdims=True)
                m = jnp.where(m == -jnp.inf, 0.0, m)
                p = jnp.exp(s - m)
                p = p / jnp.maximum(jnp.sum(p, axis=0, keepdims=True), 1e-30)
                ocmp_ref[g] = jnp.dot(pooled[:, D:].T.astype(bf16), p.astype(bf16), preferred_element_type=f32)
                imp = jnp.dot(p, fold.astype(f32), preferred_element_type=f32, precision=lax.Precision.HIGHEST)
                imp = jnp.concatenate([imp, jnp.zeros((n_sel_pad - n_blk, T), f32)], axis=0)
                score = jnp.where(blk2 <= cur, imp + SEL_BONUS * forced, -jnp.inf)
                sel = jnp.zeros((n_sel_pad, T), f32)
                for _ in range(SEL_TOPK):
                    mx = jnp.max(score, axis=0, keepdims=True)
                    first = jnp.min(jnp.where(score == mx, blk2, n_sel_pad), axis=0, keepdims=True)
                    pick = blk2 == first
                    sel = jnp.where(pick, 1.0, sel)
                    score = jnp.where(pick, -jnp.inf, score)
                sel_ref[g] = jnp.concatenate([sel] * J, axis=1)

    @pl.when(ph == 1)
    def _selected():
        @pl.when(pg == 0)
        def _init():
            m_ref[...] = jnp.full(m_ref.shape, NEG_BIG, f32)
            l_ref[...] = jnp.zeros(l_ref.shape, f32)
            acc_ref[...] = jnp.zeros(acc_ref.shape, f32)

        key = lax.broadcasted_iota(jnp.int32, (PAGE_SIZE, 1), 0)
        for r in range(PPS):
            page = sel_refs[r][0]
            first_blk = (pg * PPS + r) * blk_per_page
            for g in range(G):
                chosen = jnp.where(key < CMP_BLOCK, sel_ref[g, pl.ds(first_blk, 1), :], sel_ref[g, pl.ds(first_blk + 1, 1), :])
                flash(g, page[:, g * 2 * D:(g + 1) * 2 * D], chosen > 0.5)

        @pl.when(pg == n_pg - 1)
        def _finish():
            tail = tail_ref[0]
            wkeys = win_ref[0]
            span = wkeys.shape[0]
            wi = lax.broadcasted_iota(jnp.int32, (span, 1), 0)
            rel = qpos - (past - w_eff + wi)
            wok = (rel >= 0) & (rel <= WINDOW) & (wi < w_eff + T)
            for g in range(G):
                ok = (sel_ref[g, n_blk:n_blk + 1, :] > 0.5) & (key <= col_t) & (key < T)
                flash(g, tail[:, g * 2 * D:(g + 1) * 2 * D], ok)
                o_sel = acc_ref[g] / l_ref[g]
                kvw = wkeys[:, g * 2 * D:(g + 1) * 2 * D]
                sw = jnp.where(wok, jnp.dot(kvw[:, :D].astype(bf16), queries(g), preferred_element_type=f32), NEG_BIG)
                pw = jnp.exp(sw - jnp.max(sw, axis=0, keepdims=True))
                o_win = jnp.dot(kvw[:, D:].T.astype(bf16), pw.astype(bf16), preferred_element_type=f32)
                o_win = o_win / jnp.sum(pw, axis=0, keepdims=True)
                gates = jax.nn.sigmoid(gate_ref[0, g])
                o_ref[0, g] = gates[0:1] * ocmp_ref[g] + gates[1:2] * o_sel + gates[2:3] * o_win


def _nsa_sample(pool_cmp, pool_sel, win_buf, page_table, q, kv_cmp, kv_sel, kv_win, nsa_g):
    b, T, _ = q.shape
    assert T < CMP_BLOCK and page_table.shape[1] % NSAS_PPS == 0
    G, J, D, PPS = KV_HEADS, HEADS_PER_KV, ATTN_HEAD_DIM, NSAS_PPS
    n_pages = page_table.shape[1]
    n_pg = n_pages // PPS
    past = n_pages * PAGE_SIZE
    n_blk = past // CMP_BLOCK
    n_sel_pad = -(-(n_blk + 1) // 8) * 8
    w_eff = win_buf.shape[1]
    C = J * T
    n_pool = pool_cmp.shape[0]
    pool_cmp = pool_cmp.reshape(n_pool, PAGE_SIZE, KV_WIDTH)
    pool_sel = pool_sel.reshape(n_pool, PAGE_SIZE, KV_WIDTH)
    qs = jnp.transpose(q.reshape(b, T, G, J, D), (0, 2, 4, 3, 1)).reshape(b, G, D, C)
    gates = jnp.transpose(nsa_g.reshape(b, T, G, J, 3), (0, 2, 4, 3, 1)).reshape(b, G, 3, C)
    tail = jnp.pad(kv_sel, ((0, 0), (0, PAGE_SIZE - T), (0, 0)))
    keys = jnp.concatenate([win_buf.reshape(b, w_eff, KV_WIDTH), kv_win], axis=1)
    span = -(-(w_eff + T) // PAGE_SIZE) * PAGE_SIZE
    wkeys = jnp.pad(keys, ((0, 0), (0, span - (w_eff + T)), (0, 0)))

    def cmp_map(r):
        return lambda bi, ph, pg, pt: (pt[bi, jnp.where(ph == 0, pg, n_pg - 1) * PPS + r], 0, 0)

    def sel_map(r):
        return lambda bi, ph, pg, pt: (pt[bi, jnp.where(ph == 1, pg, 0) * PPS + r], 0, 0)

    page_block = (1, PAGE_SIZE, KV_WIDTH)
    per_b = lambda shape: pl.BlockSpec(shape, lambda bi, ph, pg, pt: (bi,) + (0,) * (len(shape) - 1))
    kern = functools.partial(_nsa_sample_kernel, n_pages=n_pages, t_len=T, w_eff=w_eff)
    out = pl.pallas_call(
        kern,
        grid_spec=pltpu.PrefetchScalarGridSpec(
            num_scalar_prefetch=1,
            grid=(b, 2, n_pg),
            in_specs=[per_b((1, G, D, C)), per_b((1, G, 3, C))]
                     + [pl.BlockSpec(page_block, cmp_map(r)) for r in range(PPS)]
                     + [pl.BlockSpec(page_block, sel_map(r)) for r in range(PPS)]
                     + [per_b((1, PAGE_SIZE, KV_WIDTH)), per_b((1, span, KV_WIDTH))],
            out_specs=per_b((1, G, D, C)),
            scratch_shapes=[pltpu.VMEM((n_blk, KV_WIDTH), jnp.float32),
                            pltpu.VMEM((G, D, C), jnp.float32),
                            pltpu.VMEM((G, n_sel_pad, C), jnp.float32),
                            pltpu.VMEM((G, 1, C), jnp.float32),
                            pltpu.VMEM((G, 1, C), jnp.float32),
                            pltpu.VMEM((G, D, C), jnp.float32)]),
        out_shape=jax.ShapeDtypeStruct((b, G, D, C), jnp.float32),
        compiler_params=pltpu.CompilerParams(dimension_semantics=("arbitrary", "arbitrary", "arbitrary"),
                                             vmem_limit_bytes=32 * 1024 * 1024),
        name="nsa_sample",
    )(page_table, qs, gates, *([pool_cmp] * PPS), *([pool_sel] * PPS), tail, wkeys)
    o_attn = jnp.transpose(out.reshape(b, G, D, J, T), (0, 4, 1, 3, 2)).reshape(b, T, ATTN_WIDTH)
    return o_attn, keys[:, -w_eff:].reshape(b, w_eff, KV_HEADS, 2, D)


PEER_PICKS = PEER_HEADS * PEER_TOPK
PEER_TT = 128
PEER_GT = 64
ROW_SUBLANES = D_MODEL // 128


def _top16_cols(s, n_rows, iota_rows):
    vals, rows = [], []
    for _ in range(PEER_TOPK):
        mx = jnp.max(s, axis=0, keepdims=True)
        first = jnp.min(jnp.where(s == mx, iota_rows, n_rows), axis=0, keepdims=True)
        vals.append(mx)
        rows.append(first)
        s = jnp.where(iota_rows == first, -jnp.inf, s)
    return jnp.concatenate(vals, axis=0), jnp.concatenate(rows, axis=0)


def _peer_route_kernel(h_ref, wq_ref, keys_ref, exp_ref, gate_ref):
    f32 = jnp.float32
    TT = PEER_TT
    q = jnp.dot(h_ref[...].astype(jnp.bfloat16), wq_ref[...], preferred_element_type=f32)
    iota_k = lax.broadcasted_iota(jnp.int32, (PEER_KEYS, TT), 0)
    widths = [PEER_TOPK // (a + 1) for a in range(PEER_TOPK)]
    n_cand = -(-sum(widths) // 8) * 8
    sub = lax.broadcasted_iota(jnp.int32, (n_cand, 1), 0)
    iota_c = jnp.full((n_cand, 1), PEER_TOPK * PEER_TOPK, jnp.int32)
    row = 0
    for a in range(PEER_TOPK):
        iota_c = jnp.where((sub >= row) & (sub < row + widths[a]), a * PEER_TOPK + (sub - row), iota_c)
        row += widths[a]
    iota_c = jnp.broadcast_to(iota_c, (n_cand, TT))
    exp_rows, gate_rows = [], []
    for hd in range(PEER_HEADS):
        tops = []
        for c in range(2):
            j = hd * 2 + c
            qj = q[:, j * PEER_HALF:(j + 1) * PEER_HALF].astype(jnp.bfloat16)
            st = lax.dot_general(keys_ref[j], qj, (((1,), (1,)), ((), ())), preferred_element_type=f32)
            tops.append(_top16_cols(st, PEER_KEYS, iota_k))
        (s0, i0), (s1, i1) = tops
        fill = n_cand - sum(widths)
        cand_s = jnp.concatenate([s0[a:a + 1, :] + s1[:widths[a], :] for a in range(PEER_TOPK)]
                                 + [jnp.full((fill, TT), -jnp.inf, f32)], axis=0)
        cand_i = jnp.concatenate([i0[a:a + 1, :] * PEER_KEYS + i1[:widths[a], :] for a in range(PEER_TOPK)]
                                 + [jnp.zeros((fill, TT), jnp.int32)], axis=0)
        best_s, best_e = [], []
        s = cand_s
        for _ in range(PEER_TOPK):
            mx = jnp.max(s, axis=0, keepdims=True)
            first = jnp.min(jnp.where(s == mx, iota_c, PEER_TOPK * PEER_TOPK), axis=0, keepdims=True)
            pick = iota_c == first
            best_s.append(mx)
            best_e.append(jnp.sum(jnp.where(pick, cand_i, 0), axis=0, keepdims=True))
            s = jnp.where(pick, -jnp.inf, s)
        bs = jnp.concatenate(best_s, axis=0)
        be = jnp.concatenate(best_e, axis=0)
        ex = jnp.exp(bs - bs[0:1, :])
        gate_rows.append(ex / jnp.sum(ex, axis=0, keepdims=True))
        exp_rows.append(be)
    exp_ref[...] = _row_code(jnp.concatenate(exp_rows, axis=0)).T
    gate_ref[...] = jnp.concatenate(gate_rows, axis=0).T


def _peer_route(h, wq_b, keys_b):
    n = h.shape[0]
    TT = PEER_TT
    return pl.pallas_call(
        _peer_route_kernel,
        grid=(n // TT,),
        in_specs=[pl.BlockSpec((TT, D_MODEL), lambda i: (i, 0)),
                  pl.BlockSpec((D_MODEL, PEER_HEADS * PEER_QDIM), lambda i: (0, 0)),
                  pl.BlockSpec((PEER_HEADS * 2, PEER_KEYS, PEER_HALF), lambda i: (0, 0, 0))],
        out_specs=[pl.BlockSpec((TT, PEER_PICKS), lambda i: (i, 0)),
                   pl.BlockSpec((TT, PEER_PICKS), lambda i: (i, 0))],
        out_shape=[jax.ShapeDtypeStruct((n, PEER_PICKS), jnp.int32),
                   jax.ShapeDtypeStruct((n, PEER_PICKS), jnp.float32)],
        compiler_params=pltpu.CompilerParams(dimension_semantics=("arbitrary",),
                                             vmem_limit_bytes=40 * 1024 * 1024),
        name="peer_route",
    )(h, wq_b, keys_b)


def _pack_rows(w):
    b = lax.bitcast_convert_type(w.astype(jnp.bfloat16), jnp.uint16).astype(jnp.uint32)
    b = b.reshape(N_EXPERTS // 2, 2, D_MODEL)
    return ((b[:, 0] << 16) | b[:, 1]).reshape(N_EXPERTS // 2 * ROW_SUBLANES, 128)


def _row_code(e):
    return (e >> 1) * ROW_SUBLANES + (e & 1)


def _load_row(tab_ref, start, parity_word):
    w = tab_ref[pl.ds(pl.multiple_of(start, ROW_SUBLANES), ROW_SUBLANES), :]
    w = (w << ((parity_word & jnp.uint32(1)) << jnp.uint32(4))) & jnp.uint32(0xFFFF0000)
    return lax.bitcast_convert_type(w, jnp.float32)


def _splat_u32(s):
    return lax.bitcast_convert_type(jnp.full((ROW_SUBLANES, 128), s, jnp.int32), jnp.uint32)


def _sublane_sums(tiles):
    sub = lax.broadcasted_iota(jnp.int32, (ROW_SUBLANES, 128), 0)
    step = ROW_SUBLANES // 2
    while step >= 1:
        low = (sub % (2 * step)) < step
        tiles = [jnp.where(low, a + pltpu.roll(a, ROW_SUBLANES - step, 0), pltpu.roll(b, step, 0) + b)
                 for a, b in zip(tiles[:len(tiles) // 2], tiles[len(tiles) // 2:])]
        step //= 2
    return tiles[0]


PEER_GROUP = 16


PEER_GROUPS_PER_TOKEN = PEER_PICKS // PEER_GROUP


def _pipelined_groups(n_groups, produce, consume, stage_a, stage_b):
    def put(stage, tiles):
        for k, tile in enumerate(tiles):
            stage[k] = tile

    def get(stage):
        return [stage[k] for k in range(PEER_GROUP)]

    def body(j, carry):
        i = 2 * j + 1
        put(stage_b, produce(i))
        consume(i - 1, get(stage_a))
        put(stage_a, produce(i + 1))
        consume(i, get(stage_b))
        return carry

    put(stage_a, produce(0))
    lax.fori_loop(0, n_groups // 2 - 1, body, 0)
    put(stage_b, produce(n_groups - 1))
    consume(n_groups - 2, get(stage_a))
    consume(n_groups - 1, get(stage_b))


def _tree_sum(tiles):
    while len(tiles) > 1:
        tiles = [a + b for a, b in zip(tiles[0::2], tiles[1::2])]
    return tiles[0]


def _peer_up_kernel(code_ref, h_ref, gate_ref, tab_ref, coef_ref, dots_ref, stage_a, stage_b):
    GT = PEER_GT

    def products(i):
        t = i // PEER_GROUPS_PER_TOKEN
        h = h_ref[pl.ds(pl.multiple_of(t * ROW_SUBLANES, ROW_SUBLANES), ROW_SUBLANES), :]
        prods = []
        for kk in range(PEER_GROUP):
            code = code_ref[i * PEER_GROUP + kk]
            prods.append(_load_row(tab_ref, code & ~(ROW_SUBLANES - 1), _splat_u32(code)) * h)
        return tuple(prods)

    def reduce_rows(i, prods):
        rows = [_sublane_sums(list(prods[k:k + ROW_SUBLANES])) for k in range(0, PEER_GROUP, ROW_SUBLANES)]
        dots_ref[pl.ds(pl.multiple_of(i * PEER_GROUP, PEER_GROUP), PEER_GROUP), :] = jnp.concatenate(rows, axis=0)

    _pipelined_groups(GT * PEER_GROUPS_PER_TOKEN, products, reduce_rows, stage_a, stage_b)
    rows = [jnp.sum(dots_ref[t * PEER_PICKS:(t + 1) * PEER_PICKS, :].T, axis=0, keepdims=True) for t in range(GT)]
    coef_ref[...] = gate_ref[...] * jax.nn.gelu(jnp.concatenate(rows, axis=0))


def _peer_down_kernel(start_ref, cbits_ref, tab_ref, o_ref, part_ref, stage_a, stage_b):
    GT = PEER_GT

    def products(i):
        prods = []
        for kk in range(PEER_GROUP):
            cb = _splat_u32(cbits_ref[i * PEER_GROUP + kk])
            coef = lax.bitcast_convert_type(cb & jnp.uint32(0xFFFFFFFE), jnp.float32)
            prods.append(_load_row(tab_ref, start_ref[i * PEER_GROUP + kk], cb) * coef)
        return tuple(prods)

    def partial_sum(i, prods):
        part_ref[pl.ds(pl.multiple_of(i * ROW_SUBLANES, ROW_SUBLANES), ROW_SUBLANES), :] = _tree_sum(list(prods))

    _pipelined_groups(GT * PEER_GROUPS_PER_TOKEN, products, partial_sum, stage_a, stage_b)
    parts = part_ref[...].reshape(GT, PEER_GROUPS_PER_TOKEN, ROW_SUBLANES, 128)
    o_ref[...] = jnp.sum(parts, axis=1).reshape(GT * ROW_SUBLANES, 128)


def _peer_gather(h, experts, gates, u_tab, v_tab):
    n = h.shape[0]
    GT = PEER_GT
    tab_rows = N_EXPERTS // 2 * ROW_SUBLANES
    flat_smem = pl.BlockSpec((GT * PEER_PICKS,), lambda i: (i,), memory_space=pltpu.MemorySpace.SMEM)
    tab_spec = pl.BlockSpec((tab_rows, 128), lambda i: (0, 0), pipeline_mode=pl.Buffered(1))
    params = pltpu.CompilerParams(dimension_semantics=("arbitrary",), vmem_limit_bytes=56 * 1024 * 1024)
    codes = experts.reshape(n * PEER_PICKS)
    stage = pltpu.VMEM((PEER_GROUP, ROW_SUBLANES, 128), jnp.float32)
    coef = pl.pallas_call(
        _peer_up_kernel,
        grid=(n // GT,),
        in_specs=[flat_smem,
                  pl.BlockSpec((GT * ROW_SUBLANES, 128), lambda i: (i, 0)),
                  pl.BlockSpec((GT, PEER_PICKS), lambda i: (i, 0)),
                  tab_spec],
        out_specs=pl.BlockSpec((GT, PEER_PICKS), lambda i: (i, 0)),
        out_shape=jax.ShapeDtypeStruct((n, PEER_PICKS), jnp.float32),
        scratch_shapes=[pltpu.VMEM((GT * PEER_PICKS, 128), jnp.float32), stage, stage],
        compiler_params=params,
        name="peer_up",
    )(codes, h.reshape(n * ROW_SUBLANES, 128), gates, u_tab)
    out = pl.pallas_call(
        _peer_down_kernel,
        grid=(n // GT,),
        in_specs=[flat_smem, flat_smem, tab_spec],
        out_specs=pl.BlockSpec((GT * ROW_SUBLANES, 128), lambda i: (i, 0)),
        out_shape=jax.ShapeDtypeStruct((n * ROW_SUBLANES, 128), jnp.float32),
        scratch_shapes=[pltpu.VMEM((GT * PEER_GROUPS_PER_TOKEN * ROW_SUBLANES, 128), jnp.float32), stage, stage],
        compiler_params=params,
        name="peer_down",
    )(codes & ~(ROW_SUBLANES - 1),
      (lax.bitcast_convert_type(coef, jnp.int32).reshape(n * PEER_PICKS) & ~1) | (codes & 1), v_tab)
    return out.reshape(n, D_MODEL)


def _layer_weights(w_in, w_ssm_branch, w_attn_branch, w_out, peer_wq, peer_sub_keys, peer_u, peer_v):
    bf16 = jnp.bfloat16
    offs = np.cumsum((0,) + PROJ_SIZES).tolist()
    seg = lambda k: w_in[:, offs[k]:offs[k + 1]]
    pad = jnp.zeros((D_MODEL, SMALL_WIDTH - SSM_HEADS - 3 * ATTN_HEADS), w_in.dtype)
    return dict(
        in_ssm=[seg(0).astype(bf16), seg(1).astype(bf16)],
        in_attn=[seg(3).astype(bf16), jnp.concatenate([seg(4), seg(5), seg(6)], axis=1).astype(bf16),
                 seg(8).astype(bf16), jnp.concatenate([seg(2), seg(7), pad], axis=1).astype(bf16)],
        w_ssm=w_ssm_branch.astype(bf16), w_attn=w_attn_branch.astype(bf16), w_out=w_out.astype(bf16),
        wq=peer_wq.astype(bf16),
        keys=peer_sub_keys.reshape(PEER_HEADS * 2, PEER_KEYS, PEER_HALF).astype(bf16),
        u_tab=_pack_rows(peer_u), v_tab=_pack_rows(peer_v))


def _block(x, conv_prev, ssm_prev, nsa_core, lw, norm_mix_w, conv_w, conv_b, dt_bias, a_log, d_skip, ssm_norm_w,
           norm_ffn_w):
    b, T, _ = x.shape
    n = b * T
    xf = x.reshape(n, D_MODEL)
    z, xbc = _norm_proj(xf, norm_mix_w, lw["in_ssm"], "in_proj_ssm")
    q, kv, br_g, small = _norm_proj(xf, norm_mix_w, lw["in_attn"], "in_proj_attn")
    nsa_g = small[:, SSM_HEADS:SSM_HEADS + 3 * ATTN_HEADS]
    kv_c, kv_s, kv_w = (kv[:, k * KV_WIDTH:(k + 1) * KV_WIDTH].reshape(b, T, KV_WIDTH) for k in range(3))
    y_ssm, conv_new, ssm_new = _mamba_branch(z.reshape(b, T, D_INNER), xbc.reshape(b, T, CONV_DIM),
                                             small.reshape(b, T, SMALL_WIDTH), conv_prev, ssm_prev, conv_w, conv_b,
                                             dt_bias, a_log, d_skip, ssm_norm_w)
    o_attn, win_new = nsa_core(q.reshape(b, T, ATTN_WIDTH), kv_c, kv_s, kv_w, nsa_g.reshape(b, T, 3 * ATTN_HEADS))
    x2, h2 = _merge_proj(xf, o_attn.reshape(n, ATTN_WIDTH), y_ssm.reshape(n, D_INNER), br_g, norm_ffn_w,
                         lw["w_attn"], lw["w_ssm"], lw["w_out"])
    experts, gates = _peer_route(h2, lw["wq"], lw["keys"])
    peer = _peer_gather(h2, experts, gates, lw["u_tab"], lw["v_tab"])
    kv_shape = (b, T, KV_HEADS, 2, ATTN_HEAD_DIM)
    return x2, peer, kv_c.reshape(kv_shape), kv_s.reshape(kv_shape), win_new, conv_new, ssm_new


def kernel(x_prompt, x_sample, cache_cmp_kv, cache_sel_kv, cache_win_kv, state_ssm, state_conv, page_table,
           norm_mix_w, w_in, conv_w, conv_b, dt_bias, a_log, d_skip, ssm_norm_w, w_ssm_branch, w_attn_branch,
           w_out, norm_ffn_w, peer_wq, peer_sub_keys, peer_u, peer_v, final_norm_w):
    xp, xs = x_prompt, x_sample
    cmp_p, cmp_s, sel_p, sel_s, win_p, win_s = [], [], [], [], [], []
    ssm_p, ssm_s, conv_p, conv_s = [], [], [], []
    for layer in range(DEPTH):
        lw = _layer_weights(*(a[layer] for a in (w_in, w_ssm_branch, w_attn_branch, w_out, peer_wq, peer_sub_keys,
                                                   peer_u, peer_v)))
        rest = [a[layer] for a in (norm_mix_w, conv_w, conv_b, dt_bias, a_log, d_skip, ssm_norm_w, norm_ffn_w)]
        last = layer == DEPTH - 1
        bp = xp.shape[0]
        conv0 = jnp.zeros((bp, CONV_WIDTH - 1, CONV_DIM), xp.dtype)
        ssm0 = jnp.zeros((bp, SSM_HEADS, SSM_HEAD_DIM, SSM_STATE), xp.dtype)
        x2, peer, kc, ks, kw, cv, ss = _block(xp, conv0, ssm0, _nsa_prompt, lw, *rest)
        xp = (_add_norm(x2, peer, final_norm_w) if last else x2 + peer).reshape(xp.shape)
        cmp_p.append(kc); sel_p.append(ks); win_p.append(kw); conv_p.append(cv); ssm_p.append(ss)
        core = functools.partial(_nsa_sample, cache_cmp_kv[layer], cache_sel_kv[layer], cache_win_kv[layer], page_table)
        x2, peer, kc, ks, kw, cv, ss = _block(xs, state_conv[layer], state_ssm[layer], core, lw, *rest)
        xs = (_add_norm(x2, peer, final_norm_w) if last else x2 + peer).reshape(xs.shape)
        cmp_s.append(kc); sel_s.append(ks); win_s.append(kw); conv_s.append(cv); ssm_s.append(ss)
    return (xp, xs, jnp.stack(cmp_p), jnp.stack(cmp_s), jnp.stack(sel_p), jnp.stack(sel_s),
            jnp.stack(win_p), jnp.stack(win_s), jnp.stack(ssm_p), jnp.stack(ssm_s), jnp.stack(conv_p), jnp.stack(conv_s))
```

```python
import math, functools
import jax, jax.numpy as jnp
from jax import lax
import numpy as np
from jax.experimental import pallas as pl
from jax.experimental.pallas import tpu as pltpu

D_MODEL = 1024
BATCH = 8
SEQ = 4096
DEPTH = 1
DEC_BATCH = 32
DEC_SEQ = 8
PAST_LEN = 16384
PAGE_SIZE = 128

SSM_EXPAND = 2
D_INNER = SSM_EXPAND * D_MODEL
SSM_HEAD_DIM = 64
SSM_HEADS = D_INNER // SSM_HEAD_DIM
SSM_GROUPS = 4
SSM_HPG = SSM_HEADS // SSM_GROUPS
SSM_STATE = 128
CONV_WIDTH = 4
CONV_DIM = D_INNER + 2 * SSM_GROUPS * SSM_STATE
SSD_CHUNK = 128
ATTN_HEADS = 16
ATTN_HEAD_DIM = 64
ATTN_WIDTH = ATTN_HEADS * ATTN_HEAD_DIM
KV_HEADS = 2
HEADS_PER_KV = ATTN_HEADS // KV_HEADS
KV_WIDTH = 2 * KV_HEADS * ATTN_HEAD_DIM
CMP_BLOCK = 64
SEL_BLOCK = CMP_BLOCK
SEL_TOPK = 16
SEL_BONUS = 1000.0
WINDOW = 512
WIN_QBLOCK = 128
SEL_QBLOCK = 32
ATTN_SCALE = ATTN_HEAD_DIM ** -0.5
PEER_HEADS = 8
PEER_KEYS = 128
N_EXPERTS = PEER_KEYS * PEER_KEYS
PEER_QDIM = 256
PEER_HALF = PEER_QDIM // 2
PEER_TOPK = 16
PEER_TOKEN_CHUNK = 128
EPS = 1e-6
PROJ_SIZES = (D_INNER, CONV_DIM, SSM_HEADS, ATTN_WIDTH, KV_WIDTH, KV_WIDTH, KV_WIDTH, 3 * ATTN_HEADS, 2 * D_MODEL)
PROJ_DIM = sum(PROJ_SIZES)


PROJ_TM = 512
PROJ_VMEM_BYTES = 56 * 1024 * 1024


def _rms(x, w):
    return x * lax.rsqrt(jnp.mean(x * x, axis=-1, keepdims=True) + EPS) * w


def _resident(shape):
    return pl.BlockSpec(shape, lambda i: (0,) * len(shape), pipeline_mode=pl.Buffered(1))


def _norm_proj_kernel(x_ref, nw_ref, *refs):
    n_out = len(refs) // 2
    h = _rms(x_ref[...], nw_ref[...]).astype(jnp.bfloat16)
    for w_ref, o_ref in zip(refs[:n_out], refs[n_out:]):
        o_ref[...] = jnp.dot(h, w_ref[...], preferred_element_type=jnp.float32)


def _norm_proj(x, norm_w, weights, name):
    n = x.shape[0]
    tm = min(PROJ_TM, n)
    return pl.pallas_call(
        _norm_proj_kernel,
        grid=(n // tm,),
        in_specs=[pl.BlockSpec((tm, D_MODEL), lambda i: (i, 0)), _resident((1, D_MODEL))]
                 + [_resident(w.shape) for w in weights],
        out_specs=[pl.BlockSpec((tm, w.shape[1]), lambda i: (i, 0)) for w in weights],
        out_shape=[jax.ShapeDtypeStruct((n, w.shape[1]), jnp.float32) for w in weights],
        compiler_params=pltpu.CompilerParams(dimension_semantics=("arbitrary",), vmem_limit_bytes=PROJ_VMEM_BYTES),
        name=name,
    )(x, norm_w.reshape(1, D_MODEL), *weights)


def _merge_proj_kernel(x_ref, oa_ref, ys_ref, g_ref, nw_ref, wa_ref, ws_ref, wo_ref, x2_ref, h2_ref):
    f32, bf16 = jnp.float32, jnp.bfloat16
    y_attn = jnp.dot(oa_ref[...].astype(bf16), wa_ref[...], preferred_element_type=f32)
    y_ssm = jnp.dot(ys_ref[...].astype(bf16), ws_ref[...], preferred_element_type=f32)
    g = jax.nn.sigmoid(g_ref[...])
    merged = g[:, :D_MODEL] * y_ssm + g[:, D_MODEL:] * y_attn
    x2 = x_ref[...] + jnp.dot(merged.astype(bf16), wo_ref[...], preferred_element_type=f32)
    x2_ref[...] = x2
    h2_ref[...] = _rms(x2, nw_ref[...])


def _merge_proj(x, o_attn, y_ssm, br_g, norm_w, wa_b, ws_b, wo_b):
    n = x.shape[0]
    tm = min(PROJ_TM, n)
    rows = lambda c: pl.BlockSpec((tm, c), lambda i: (i, 0))
    return pl.pallas_call(
        _merge_proj_kernel,
        grid=(n // tm,),
        in_specs=[rows(D_MODEL), rows(ATTN_WIDTH), rows(D_INNER), rows(2 * D_MODEL), _resident((1, D_MODEL)),
                  _resident(wa_b.shape), _resident(ws_b.shape), _resident(wo_b.shape)],
        out_specs=[rows(D_MODEL), rows(D_MODEL)],
        out_shape=[jax.ShapeDtypeStruct((n, D_MODEL), jnp.float32)] * 2,
        compiler_params=pltpu.CompilerParams(dimension_semantics=("arbitrary",), vmem_limit_bytes=PROJ_VMEM_BYTES),
        name="merge_proj",
    )(x, o_attn, y_ssm, br_g, norm_w.reshape(1, D_MODEL), wa_b, ws_b, wo_b)


def _add_norm_kernel(x_ref, y_ref, w_ref, o_ref):
    o_ref[...] = _rms(x_ref[...] + y_ref[...], w_ref[...])


def _add_norm(x, y, w):
    n = x.shape[0]
    tm = min(PROJ_TM, n)
    rows = pl.BlockSpec((tm, D_MODEL), lambda i: (i, 0))
    return pl.pallas_call(
        _add_norm_kernel,
        grid=(n // tm,),
        in_specs=[rows, rows, _resident((1, D_MODEL))],
        out_specs=rows,
        out_shape=jax.ShapeDtypeStruct((n, D_MODEL), jnp.float32),
        name="add_norm",
    )(x, y, w.reshape(1, D_MODEL))


SMALL_WIDTH = 128
CONV_TAIL = 8


def _ssd_kernel(z_ref, xbc_ref, sm_ref, cprev_ref, h0_ref, cw_ref, cbias_ref, dtb_ref, alog_ref, dskip_ref, nw_ref,
                y_ref, hout_ref, state_ref, tail_ref, *, valid):
    f32, bf16 = jnp.float32, jnp.bfloat16
    L, P, N = SSD_CHUNK, SSM_HEAD_DIM, SSM_STATE
    c = pl.program_id(1)

    @pl.when(c == 0)
    def _init():
        state_ref[...] = h0_ref[0]
        tail_ref[...] = cprev_ref[0]

    xbc = xbc_ref[0]
    padded = jnp.concatenate([tail_ref[...], xbc], axis=0)
    acc = jnp.broadcast_to(cbias_ref[...], (L, CONV_DIM))
    for w in range(CONV_WIDTH):
        lo = CONV_TAIL - (CONV_WIDTH - 1) + w
        acc = acc + cw_ref[w:w + 1, :] * padded[lo:lo + L, :]
    xc = acc * jax.nn.sigmoid(acc)
    tail_ref[...] = xbc[L - CONV_TAIL:, :]

    row = lax.broadcasted_iota(jnp.int32, (L, 1), 0)
    live = row < (valid - c * L)
    dt = jnp.where(live, jax.nn.softplus(sm_ref[0] + dtb_ref[...]), 0.0)
    a = dt * (-jnp.exp(alog_ref[...]))
    tril = (row >= lax.broadcasted_iota(jnp.int32, (L, L), 1))
    a_cum = jnp.dot(tril.astype(f32), a, preferred_element_type=f32, precision=lax.Precision.HIGHEST)
    a_cum_t = a_cum.T
    a_last = a_cum[L - 1:L, :]
    grow = jnp.exp(a_cum)
    to_end = jnp.exp(a_last - a_cum)
    carry = jnp.exp(a_last)

    W = SSM_HPG * P

    def per_head(v, g):
        cols = [jnp.broadcast_to(v[:, g * SSM_HPG + j:g * SSM_HPG + j + 1], (v.shape[0], P)) for j in range(SSM_HPG)]
        return jnp.concatenate(cols, axis=1)

    xs = xc[:, :D_INNER]
    ys = []
    for g in range(SSM_GROUPS):
        bg = xc[:, D_INNER + g * N:D_INNER + (g + 1) * N]
        cg = xc[:, D_INNER + SSM_GROUPS * N + g * N:D_INNER + SSM_GROUPS * N + (g + 1) * N]
        bg_b, cg_b = bg.astype(bf16), cg.astype(bf16)
        bg_t = bg.T.astype(bf16)
        cb = lax.dot_general(cg_b, bg_b, (((1,), (1,)), ((), ())), preferred_element_type=f32)
        xs_g = xs[:, g * W:(g + 1) * W]
        xdt = xs_g * per_head(dt, g)
        xdt_b = xdt.astype(bf16)
        st = state_ref[g]
        y = jnp.dot(cg_b, st.astype(bf16), preferred_element_type=f32) * per_head(grow, g)
        state_ref[g] = st * per_head(carry, g) + jnp.dot(bg_t, (xdt * per_head(to_end, g)).astype(bf16),
                                                         preferred_element_type=f32)
        diag = []
        for j in range(SSM_HPG):
            h = g * SSM_HPG + j
            diff = a_cum[:, h:h + 1] - a_cum_t[h:h + 1, :]
            m = (cb * jnp.exp(jnp.where(tril, diff, -jnp.inf))).astype(bf16)
            diag.append(jnp.dot(m, xdt_b[:, j * P:(j + 1) * P], preferred_element_type=f32))
        ys.append(jnp.concatenate(diag, axis=1) + y + per_head(dskip_ref[...], g) * xs_g)
    z = z_ref[0]
    y_ref[0] = _rms(jnp.concatenate(ys, axis=1) * (z * jax.nn.sigmoid(z)), nw_ref[...])

    @pl.when(c == pl.num_programs(1) - 1)
    def _emit():
        hout_ref[0] = state_ref[...]


def _mamba_branch(z, xbc, small, conv_prev, ssm_prev, conv_w, conv_b, dt_bias, a_log, d_skip, ssm_norm_w):
    b, L, _ = xbc.shape
    Lc = SSD_CHUNK
    Lp = -(-L // Lc) * Lc
    keep = CONV_WIDTH - 1
    conv_new = xbc[:, L - keep:] if L >= keep else jnp.concatenate([conv_prev, xbc], axis=1)[:, -keep:]
    if Lp != L:
        padrows = lambda a: jnp.pad(a, ((0, 0), (0, Lp - L), (0, 0)))
        z, xbc, small = padrows(z), padrows(xbc), padrows(small)
    cprev = jnp.pad(conv_prev, ((0, 0), (CONV_TAIL - (CONV_WIDTH - 1), 0), (0, 0)))
    G, J, P, N = SSM_GROUPS, SSM_HPG, SSM_HEAD_DIM, SSM_STATE
    state_block = (1, G, N, J * P)
    h0 = jnp.transpose(ssm_prev.reshape(b, G, J, P, N), (0, 1, 4, 2, 3)).reshape(b, G, N, J * P)
    lane = lambda v: jnp.pad(v.reshape(1, SSM_HEADS), ((0, 0), (0, SMALL_WIDTH - SSM_HEADS)))
    full = lambda shape: pl.BlockSpec(shape, lambda bi, c: (0,) * len(shape))
    kern = functools.partial(_ssd_kernel, valid=L)
    y, hout = pl.pallas_call(
        kern,
        grid=(b, Lp // Lc),
        in_specs=[pl.BlockSpec((1, Lc, D_INNER), lambda bi, c: (bi, c, 0)),
                  pl.BlockSpec((1, Lc, CONV_DIM), lambda bi, c: (bi, c, 0)),
                  pl.BlockSpec((1, Lc, SMALL_WIDTH), lambda bi, c: (bi, c, 0)),
                  pl.BlockSpec((1, CONV_TAIL, CONV_DIM), lambda bi, c: (bi, 0, 0)),
                  pl.BlockSpec(state_block, lambda bi, c: (bi, 0, 0, 0)),
                  full((CONV_WIDTH, CONV_DIM)), full((1, CONV_DIM)), full((1, SMALL_WIDTH)), full((1, SMALL_WIDTH)),
                  full((1, SMALL_WIDTH)), full((1, D_INNER))],
        out_specs=[pl.BlockSpec((1, Lc, D_INNER), lambda bi, c: (bi, c, 0)),
                   pl.BlockSpec(state_block, lambda bi, c: (bi, 0, 0, 0))],
        out_shape=[jax.ShapeDtypeStruct((b, Lp, D_INNER), jnp.float32),
                   jax.ShapeDtypeStruct((b,) + state_block[1:], jnp.float32)],
        scratch_shapes=[pltpu.VMEM(state_block[1:], jnp.float32),
                        pltpu.VMEM((CONV_TAIL, CONV_DIM), jnp.float32)],
        compiler_params=pltpu.CompilerParams(dimension_semantics=("arbitrary", "arbitrary"),
                                             vmem_limit_bytes=48 * 1024 * 1024),
        name="mamba_ssd",
    )(z, xbc, small, cprev, h0, conv_w, conv_b.reshape(1, CONV_DIM), lane(dt_bias), lane(a_log), lane(d_skip),
      ssm_norm_w.reshape(1, D_INNER))
    ssm_new = jnp.transpose(hout.reshape(b, G, N, J, P), (0, 1, 3, 4, 2)).reshape(b, SSM_HEADS, P, N)
    return y[:, :L], conv_new, ssm_new


NSA_TQ = 128
NSA_TK = 512
NSA_WSPAN = WINDOW + NSA_TQ
NEG_BIG = -1e30


def _nsa_prompt_kernel(q_ref, kvc_ref, ks_ref, vst_ref, kw_ref, vwt_ref, gate_ref, o_ref,
                       pk_ref, pvt_ref, m_ref, l_ref, acc_ref, *, seq):
    f32, bf16 = jnp.float32, jnp.bfloat16
    D, J, TQ, TK = ATTN_HEAD_DIM, HEADS_PER_KV, NSA_TQ, NSA_TK
    n_blk = seq // CMP_BLOCK
    i = pl.program_id(2)
    t0 = i * TQ

    @pl.when(i == 0)
    def _pool():
        c = kvc_ref[0]
        pooled = jnp.sum(c.reshape(n_blk, CMP_BLOCK, 2 * D), axis=1) * (1.0 / CMP_BLOCK)
        pk_ref[...] = pooled[:, :D].astype(bf16)
        pvt_ref[...] = pooled[:, D:].T.astype(bf16)

    qt = (q_ref[0] * ATTN_SCALE).T
    qs = jnp.concatenate([qt[j * D:(j + 1) * D, :] for j in range(J)], axis=1).astype(bf16)
    qpos1 = t0 + lax.broadcasted_iota(jnp.int32, (1, TQ), 1)
    qpos = jnp.concatenate([qpos1] * J, axis=1)

    s = jnp.dot(pk_ref[...], qs, preferred_element_type=f32)
    blk = lax.broadcasted_iota(jnp.int32, (n_blk, 1), 0)
    vis = (blk * CMP_BLOCK + (CMP_BLOCK - 1)) <= qpos
    s = jnp.where(vis, s, -jnp.inf)
    m = jnp.max(s, axis=0, keepdims=True)
    m = jnp.where(m == -jnp.inf, 0.0, m)
    p = jnp.exp(s - m)
    p = p / jnp.maximum(jnp.sum(p, axis=0, keepdims=True), 1e-30)
    o_cmp = jnp.dot(pvt_ref[...], p.astype(bf16), preferred_element_type=f32)
    imp = p[:, 0:TQ]
    for j in range(1, J):
        imp = imp + p[:, j * TQ:(j + 1) * TQ]

    cur = qpos1 // CMP_BLOCK
    blk2 = lax.broadcasted_iota(jnp.int32, (n_blk, TQ), 0)
    forced = ((blk2 == 0) | (blk2 == cur) | (blk2 == cur - 1)).astype(f32)
    score = jnp.where(blk2 <= cur, imp + SEL_BONUS * forced, -jnp.inf)
    sel = jnp.zeros((n_blk, TQ), f32)
    for _ in range(min(SEL_TOPK, n_blk)):
        mx = jnp.max(score, axis=0, keepdims=True)
        first = jnp.min(jnp.where(score == mx, blk2, n_blk), axis=0, keepdims=True)
        pick = blk2 == first
        sel = jnp.where(pick, 1.0, sel)
        score = jnp.where(pick, -jnp.inf, score)
    sel_b = sel.astype(bf16)

    m_ref[...] = jnp.full(m_ref.shape, NEG_BIG, f32)
    l_ref[...] = jnp.zeros(l_ref.shape, f32)
    acc_ref[...] = jnp.zeros(acc_ref.shape, f32)
    n_kt = (t0 + TQ + TK - 1) // TK

    def sel_tile(kt, carry):
        k0 = pl.multiple_of(kt * TK, TK)
        kpos = k0 + lax.broadcasted_iota(jnp.int32, (TK, 1), 0)
        expand = ((kpos // CMP_BLOCK) == lax.broadcasted_iota(jnp.int32, (TK, n_blk), 1)).astype(bf16)
        chosen = jnp.dot(expand, sel_b, preferred_element_type=f32)
        ok = (chosen > 0.5) & (kpos <= qpos1)
        bias1 = jnp.where(ok, 0.0, NEG_BIG)
        bias = jnp.concatenate([bias1] * J, axis=1)
        st = jnp.dot(ks_ref[0, 0, pl.ds(k0, TK), :], qs, preferred_element_type=f32) + bias
        m_old = m_ref[...]
        m_new = jnp.maximum(m_old, jnp.max(st, axis=0, keepdims=True))
        alpha = jnp.exp(m_old - m_new)
        pt = jnp.exp(st - m_new)
        l_ref[...] = alpha * l_ref[...] + jnp.sum(pt, axis=0, keepdims=True)
        acc_ref[...] = alpha * acc_ref[...] + jnp.dot(vst_ref[0, 0, :, pl.ds(k0, TK)], pt.astype(bf16),
                                                      preferred_element_type=f32)
        m_ref[...] = m_new
        return carry

    lax.fori_loop(0, n_kt, sel_tile, 0)
    o_sel = acc_ref[...] / l_ref[...]

    w0 = pl.multiple_of(jnp.maximum(t0 - WINDOW, 0), TQ)
    span = min(NSA_WSPAN, seq)
    kposw = w0 + lax.broadcasted_iota(jnp.int32, (span, 1), 0)
    rel = qpos1 - kposw
    biasw1 = jnp.where((rel >= 0) & (rel <= WINDOW), 0.0, NEG_BIG)
    biasw = jnp.concatenate([biasw1] * J, axis=1)
    sw = jnp.dot(kw_ref[0, 0, pl.ds(w0, span), :], qs, preferred_element_type=f32) + biasw
    pw = jnp.exp(sw - jnp.max(sw, axis=0, keepdims=True))
    lw = jnp.sum(pw, axis=0, keepdims=True)
    o_win = jnp.dot(vwt_ref[0, 0, :, pl.ds(w0, span)], pw.astype(bf16), preferred_element_type=f32) / lw

    def gate_row(br):
        g = jax.nn.sigmoid(gate_ref[0, 0, br])
        return jnp.concatenate([g[j:j + 1, :] for j in range(J)], axis=1)

    out = gate_row(0) * o_cmp + gate_row(1) * o_sel + gate_row(2) * o_win
    out = jnp.concatenate([out[:, j * TQ:(j + 1) * TQ] for j in range(J)], axis=0)
    o_ref[0] = out.T


def _nsa_prompt_attention(q, kv_c, kv_s, kv_w, nsa_g):
    b, T, _ = q.shape
    G, J, D, TQ = KV_HEADS, HEADS_PER_KV, ATTN_HEAD_DIM, NSA_TQ
    bf16 = jnp.bfloat16

    def split_kv(kv):
        r = kv.reshape(b, T, G, 2, D).astype(bf16)
        return jnp.transpose(r[:, :, :, 0], (0, 2, 1, 3)), jnp.transpose(r[:, :, :, 1], (0, 2, 3, 1))

    ks, vst = split_kv(kv_s)
    kw, vwt = split_kv(kv_w)
    gates_t = jnp.transpose(nsa_g.reshape(b, T, G, J, 3), (0, 2, 4, 3, 1))
    n_blk = T // CMP_BLOCK
    kern = functools.partial(_nsa_prompt_kernel, seq=T)
    return pl.pallas_call(
        kern,
        grid=(b, G, T // TQ),
        in_specs=[
            pl.BlockSpec((1, TQ, J * D), lambda bi, g, i: (bi, i, g)),
            pl.BlockSpec((1, T, 2 * D), lambda bi, g, i: (bi, 0, g)),
            pl.BlockSpec((1, 1, T, D), lambda bi, g, i: (bi, g, 0, 0)),
            pl.BlockSpec((1, 1, D, T), lambda bi, g, i: (bi, g, 0, 0)),
            pl.BlockSpec((1, 1, T, D), lambda bi, g, i: (bi, g, 0, 0)),
            pl.BlockSpec((1, 1, D, T), lambda bi, g, i: (bi, g, 0, 0)),
            pl.BlockSpec((1, 1, 3, J, TQ), lambda bi, g, i: (bi, g, 0, 0, i)),
        ],
        out_specs=pl.BlockSpec((1, TQ, J * D), lambda bi, g, i: (bi, i, g)),
        out_shape=jax.ShapeDtypeStruct((b, T, G * J * D), jnp.float32),
        scratch_shapes=[
            pltpu.VMEM((n_blk, D), bf16),
            pltpu.VMEM((D, n_blk), bf16),
            pltpu.VMEM((1, J * TQ), jnp.float32),
            pltpu.VMEM((1, J * TQ), jnp.float32),
            pltpu.VMEM((D, J * TQ), jnp.float32),
        ],
        compiler_params=pltpu.CompilerParams(
            dimension_semantics=("arbitrary", "arbitrary", "arbitrary"),
            vmem_limit_bytes=48 * 1024 * 1024),
        name="nsa_prompt",
    )(q, kv_c, ks, vst, kw, vwt, gates_t)


def _nsa_prompt(q, kv_c, kv_s, kv_w, nsa_g):
    b, T = q.shape[:2]
    win_new = kv_w.reshape(b, T, KV_HEADS, 2, ATTN_HEAD_DIM)[:, -min(WINDOW, T):]
    return _nsa_prompt_attention(q, kv_c, kv_s, kv_w, nsa_g), win_new


NSAS_PPS = 8


def _nsa_sample_kernel(pt_ref, qs_ref, gate_ref, *refs, n_pages, t_len, w_eff):
    f32, bf16 = jnp.float32, jnp.bfloat16
    PPS, G, D, J, T = NSAS_PPS, KV_HEADS, ATTN_HEAD_DIM, HEADS_PER_KV, t_len
    cmp_refs, sel_refs = refs[:PPS], refs[PPS:2 * PPS]
    tail_ref, win_ref, o_ref, pooled_ref, ocmp_ref, sel_ref, m_ref, l_ref, acc_ref = refs[2 * PPS:]
    C = J * T
    ph, pg = pl.program_id(1), pl.program_id(2)
    n_pg = n_pages // PPS
    past = n_pages * PAGE_SIZE
    n_blk = past // CMP_BLOCK
    n_sel_pad = sel_ref.shape[1]
    blk_per_page = PAGE_SIZE // CMP_BLOCK
    col_t = lax.broadcasted_iota(jnp.int32, (1, C), 1) % T
    qpos = past + col_t

    def queries(g):
        return (qs_ref[0, g] * ATTN_SCALE).astype(bf16)

    def flash(g, kv, ok):
        k, v = kv[:, :D], kv[:, D:]
        st = jnp.where(ok, jnp.dot(k.astype(bf16), queries(g), preferred_element_type=f32), NEG_BIG)
        m_old = m_ref[g]
        m_new = jnp.maximum(m_old, jnp.max(st, axis=0, keepdims=True))
        alpha = jnp.exp(m_old - m_new)
        p = jnp.exp(st - m_new)
        l_ref[g] = alpha * l_ref[g] + jnp.sum(p, axis=0, keepdims=True)
        acc_ref[g] = alpha * acc_ref[g] + jnp.dot(v.T.astype(bf16), p.astype(bf16), preferred_element_type=f32)
        m_ref[g] = m_new

    @pl.when(ph == 0)
    def _compressed():
        means = []
        for r in range(PPS):
            page = cmp_refs[r][0]
            means.append(jnp.sum(page.reshape(blk_per_page, CMP_BLOCK, G * 2 * D), axis=1) * (1.0 / CMP_BLOCK))
        rows = PPS * blk_per_page
        pooled_ref[pl.ds(pl.multiple_of(pg * rows, rows), rows), :] = jnp.concatenate(means, axis=0)

        @pl.when(pg == n_pg - 1)
        def _select():
            blk = lax.broadcasted_iota(jnp.int32, (n_blk, 1), 0)
            vis = (blk * CMP_BLOCK + (CMP_BLOCK - 1)) <= qpos
            fold = (lax.broadcasted_iota(jnp.int32, (C, T), 0) % T == lax.broadcasted_iota(jnp.int32, (C, T), 1))
            blk2 = lax.broadcasted_iota(jnp.int32, (n_sel_pad, T), 0)
            cur = (past + lax.broadcasted_iota(jnp.int32, (1, T), 1)) // SEL_BLOCK
            forced = ((blk2 == 0) | (blk2 == cur) | (blk2 == cur - 1)).astype(f32)
            for g in range(G):
                pooled = pooled_ref[:, g * 2 * D:(g + 1) * 2 * D]
                s = jnp.dot(pooled[:, :D].astype(bf16), queries(g), preferred_element_type=f32)
                s = jnp.where(vis, s, -jnp.inf)
                m = jnp.max(s, axis=0, keepdims=True)
                m = jnp.where(m == -jnp.inf, 0.0, m)
                p = jnp.exp(s - m)
                p = p / jnp.maximum(jnp.sum(p, axis=0, keepdims=True), 1e-30)
                ocmp_ref[g] = jnp.dot(pooled[:, D:].T.astype(bf16), p.astype(bf16), preferred_element_type=f32)
                imp = jnp.dot(p, fold.astype(f32), preferred_element_type=f32, precision=lax.Precision.HIGHEST)
                imp = jnp.concatenate([imp, jnp.zeros((n_sel_pad - n_blk, T), f32)], axis=0)
                score = jnp.where(blk2 <= cur, imp + SEL_BONUS * forced, -jnp.inf)
                sel = jnp.zeros((n_sel_pad, T), f32)
                for _ in range(SEL_TOPK):
                    mx = jnp.max(score, axis=0, keepdims=True)
                    first = jnp.min(jnp.where(score == mx, blk2, n_sel_pad), axis=0, keepdims=True)
                    pick = blk2 == first
                    sel = jnp.where(pick, 1.0, sel)
                    score = jnp.where(pick, -jnp.inf, score)
                sel_ref[g] = jnp.concatenate([sel] * J, axis=1)

    @pl.when(ph == 1)
    def _selected():
        @pl.when(pg == 0)
        def _init():
            m_ref[...] = jnp.full(m_ref.shape, NEG_BIG, f32)
            l_ref[...] = jnp.zeros(l_ref.shape, f32)
            acc_ref[...] = jnp.zeros(acc_ref.shape, f32)

        key = lax.broadcasted_iota(jnp.int32, (PAGE_SIZE, 1), 0)
        for r in range(PPS):
            page = sel_refs[r][0]
            first_blk = (pg * PPS + r) * blk_per_page
            for g in range(G):
                chosen = jnp.where(key < CMP_BLOCK, sel_ref[g, pl.ds(first_blk, 1), :], sel_ref[g, pl.ds(first_blk + 1, 1), :])
                flash(g, page[:, g * 2 * D:(g + 1) * 2 * D], chosen > 0.5)

        @pl.when(pg == n_pg - 1)
        def _finish():
            tail = tail_ref[0]
            wkeys = win_ref[0]
            span = wkeys.shape[0]
            wi = lax.broadcasted_iota(jnp.int32, (span, 1), 0)
            rel = qpos - (past - w_eff + wi)
            wok = (rel >= 0) & (rel <= WINDOW) & (wi < w_eff + T)
            for g in range(G):
                ok = (sel_ref[g, n_blk:n_blk + 1, :] > 0.5) & (key <= col_t) & (key < T)
                flash(g, tail[:, g * 2 * D:(g + 1) * 2 * D], ok)
                o_sel = acc_ref[g] / l_ref[g]
                kvw = wkeys[:, g * 2 * D:(g + 1) * 2 * D]
                sw = jnp.where(wok, jnp.dot(kvw[:, :D].astype(bf16), queries(g), preferred_element_type=f32), NEG_BIG)
                pw = jnp.exp(sw - jnp.max(sw, axis=0, keepdims=True))
                o_win = jnp.dot(kvw[:, D:].T.astype(bf16), pw.astype(bf16), preferred_element_type=f32)
                o_win = o_win / jnp.sum(pw, axis=0, keepdims=True)
                gates = jax.nn.sigmoid(gate_ref[0, g])
                o_ref[0, g] = gates[0:1] * ocmp_ref[g] + gates[1:2] * o_sel + gates[2:3] * o_win


def _nsa_sample(pool_cmp, pool_sel, win_buf, page_table, q, kv_cmp, kv_sel, kv_win, nsa_g):
    b, T, _ = q.shape
    assert T < CMP_BLOCK and page_table.shape[1] % NSAS_PPS == 0
    G, J, D, PPS = KV_HEADS, HEADS_PER_KV, ATTN_HEAD_DIM, NSAS_PPS
    n_pages = page_table.shape[1]
    n_pg = n_pages // PPS
    past = n_pages * PAGE_SIZE
    n_blk = past // CMP_BLOCK
    n_sel_pad = -(-(n_blk + 1) // 8) * 8
    w_eff = win_buf.shape[1]
    C = J * T
    n_pool = pool_cmp.shape[0]
    pool_cmp = pool_cmp.reshape(n_pool, PAGE_SIZE, KV_WIDTH)
    pool_sel = pool_sel.reshape(n_pool, PAGE_SIZE, KV_WIDTH)
    qs = jnp.transpose(q.reshape(b, T, G, J, D), (0, 2, 4, 3, 1)).reshape(b, G, D, C)
    gates = jnp.transpose(nsa_g.reshape(b, T, G, J, 3), (0, 2, 4, 3, 1)).reshape(b, G, 3, C)
    tail = jnp.pad(kv_sel, ((0, 0), (0, PAGE_SIZE - T), (0, 0)))
    keys = jnp.concatenate([win_buf.reshape(b, w_eff, KV_WIDTH), kv_win], axis=1)
    span = -(-(w_eff + T) // PAGE_SIZE) * PAGE_SIZE
    wkeys = jnp.pad(keys, ((0, 0), (0, span - (w_eff + T)), (0, 0)))

    def cmp_map(r):
        return lambda bi, ph, pg, pt: (pt[bi, jnp.where(ph == 0, pg, n_pg - 1) * PPS + r], 0, 0)

    def sel_map(r):
        return lambda bi, ph, pg, pt: (pt[bi, jnp.where(ph == 1, pg, 0) * PPS + r], 0, 0)

    page_block = (1, PAGE_SIZE, KV_WIDTH)
    per_b = lambda shape: pl.BlockSpec(shape, lambda bi, ph, pg, pt: (bi,) + (0,) * (len(shape) - 1))
    kern = functools.partial(_nsa_sample_kernel, n_pages=n_pages, t_len=T, w_eff=w_eff)
    out = pl.pallas_call(
        kern,
        grid_spec=pltpu.PrefetchScalarGridSpec(
            num_scalar_prefetch=1,
            grid=(b, 2, n_pg),
            in_specs=[per_b((1, G, D, C)), per_b((1, G, 3, C))]
                     + [pl.BlockSpec(page_block, cmp_map(r)) for r in range(PPS)]
                     + [pl.BlockSpec(page_block, sel_map(r)) for r in range(PPS)]
                     + [per_b((1, PAGE_SIZE, KV_WIDTH)), per_b((1, span, KV_WIDTH))],
            out_specs=per_b((1, G, D, C)),
            scratch_shapes=[pltpu.VMEM((n_blk, KV_WIDTH), jnp.float32),
                            pltpu.VMEM((G, D, C), jnp.float32),
                            pltpu.VMEM((G, n_sel_pad, C), jnp.float32),
                            pltpu.VMEM((G, 1, C), jnp.float32),
                            pltpu.VMEM((G, 1, C), jnp.float32),
                            pltpu.VMEM((G, D, C), jnp.float32)]),
        out_shape=jax.ShapeDtypeStruct((b, G, D, C), jnp.float32),
        compiler_params=pltpu.CompilerParams(dimension_semantics=("arbitrary", "arbitrary", "arbitrary"),
                                             vmem_limit_bytes=32 * 1024 * 1024),
        name="nsa_sample",
    )(page_table, qs, gates, *([pool_cmp] * PPS), *([pool_sel] * PPS), tail, wkeys)
    o_attn = jnp.transpose(out.reshape(b, G, D, J, T), (0, 4, 1, 3, 2)).reshape(b, T, ATTN_WIDTH)
    return o_attn, keys[:, -w_eff:].reshape(b, w_eff, KV_HEADS, 2, D)


PEER_PICKS = PEER_HEADS * PEER_TOPK
PEER_TT = 128
PEER_GT = 64
ROW_SUBLANES = D_MODEL // 128


def _top16_cols(s, n_rows, iota_rows):
    vals, rows = [], []
    for _ in range(PEER_TOPK):
        mx = jnp.max(s, axis=0, keepdims=True)
        first = jnp.min(jnp.where(s == mx, iota_rows, n_rows), axis=0, keepdims=True)
        vals.append(mx)
        rows.append(first)
        s = jnp.where(iota_rows == first, -jnp.inf, s)
    return jnp.concatenate(vals, axis=0), jnp.concatenate(rows, axis=0)


def _peer_route_kernel(h_ref, wq_ref, keys_ref, exp_ref, gate_ref):
    f32 = jnp.float32
    TT = PEER_TT
    q = jnp.dot(h_ref[...].astype(jnp.bfloat16), wq_ref[...], preferred_element_type=f32)
    iota_k = lax.broadcasted_iota(jnp.int32, (PEER_KEYS, TT), 0)
    widths = [PEER_TOPK // (a + 1) for a in range(PEER_TOPK)]
    n_cand = -(-sum(widths) // 8) * 8
    sub = lax.broadcasted_iota(jnp.int32, (n_cand, 1), 0)
    iota_c = jnp.full((n_cand, 1), PEER_TOPK * PEER_TOPK, jnp.int32)
    row = 0
    for a in range(PEER_TOPK):
        iota_c = jnp.where((sub >= row) & (sub < row + widths[a]), a * PEER_TOPK + (sub - row), iota_c)
        row += widths[a]
    iota_c = jnp.broadcast_to(iota_c, (n_cand, TT))
    exp_rows, gate_rows = [], []
    for hd in range(PEER_HEADS):
        tops = []
        for c in range(2):
            j = hd * 2 + c
            qj = q[:, j * PEER_HALF:(j + 1) * PEER_HALF].astype(jnp.bfloat16)
            st = lax.dot_general(keys_ref[j], qj, (((1,), (1,)), ((), ())), preferred_element_type=f32)
            tops.append(_top16_cols(st, PEER_KEYS, iota_k))
        (s0, i0), (s1, i1) = tops
        fill = n_cand - sum(widths)
        cand_s = jnp.concatenate([s0[a:a + 1, :] + s1[:widths[a], :] for a in range(PEER_TOPK)]
                                 + [jnp.full((fill, TT), -jnp.inf, f32)], axis=0)
        cand_i = jnp.concatenate([i0[a:a + 1, :] * PEER_KEYS + i1[:widths[a], :] for a in range(PEER_TOPK)]
                                 + [jnp.zeros((fill, TT), jnp.int32)], axis=0)
        best_s, best_e = [], []
        s = cand_s
        for _ in range(PEER_TOPK):
            mx = jnp.max(s, axis=0, keepdims=True)
            first = jnp.min(jnp.where(s == mx, iota_c, PEER_TOPK * PEER_TOPK), axis=0, keepdims=True)
            pick = iota_c == first
            best_s.append(mx)
            best_e.append(jnp.sum(jnp.where(pick, cand_i, 0), axis=0, keepdims=True))
            s = jnp.where(pick, -jnp.inf, s)
        bs = jnp.concatenate(best_s, axis=0)
        be = jnp.concatenate(best_e, axis=0)
        ex = jnp.exp(bs - bs[0:1, :])
        gate_rows.append(ex / jnp.sum(ex, axis=0, keepdims=True))
        exp_rows.append(be)
    exp_ref[...] = _row_code(jnp.concatenate(exp_rows, axis=0)).T
    gate_ref[...] = jnp.concatenate(gate_rows, axis=0).T


def _peer_route(h, wq_b, keys_b):
    n = h.shape[0]
    TT = PEER_TT
    return pl.pallas_call(
        _peer_route_kernel,
        grid=(n // TT,),
        in_specs=[pl.BlockSpec((TT, D_MODEL), lambda i: (i, 0)),
                  pl.BlockSpec((D_MODEL, PEER_HEADS * PEER_QDIM), lambda i: (0, 0)),
                  pl.BlockSpec((PEER_HEADS * 2, PEER_KEYS, PEER_HALF), lambda i: (0, 0, 0))],
        out_specs=[pl.BlockSpec((TT, PEER_PICKS), lambda i: (i, 0)),
                   pl.BlockSpec((TT, PEER_PICKS), lambda i: (i, 0))],
        out_shape=[jax.ShapeDtypeStruct((n, PEER_PICKS), jnp.int32),
                   jax.ShapeDtypeStruct((n, PEER_PICKS), jnp.float32)],
        compiler_params=pltpu.CompilerParams(dimension_semantics=("arbitrary",),
                                             vmem_limit_bytes=40 * 1024 * 1024),
        name="peer_route",
    )(h, wq_b, keys_b)


def _pack_rows(w):
    b = lax.bitcast_convert_type(w.astype(jnp.bfloat16), jnp.uint16).astype(jnp.uint32)
    b = b.reshape(N_EXPERTS // 2, 2, D_MODEL)
    return ((b[:, 0] << 16) | b[:, 1]).reshape(N_EXPERTS // 2 * ROW_SUBLANES, 128)


def _row_code(e):
    return (e >> 1) * ROW_SUBLANES + (e & 1)


PARITY_SHIFT_BIT = 4


def _load_row(tab_ref, start, shift_word):
    w = tab_ref[pl.ds(pl.multiple_of(start, ROW_SUBLANES), ROW_SUBLANES), :]
    w = (w << (shift_word & jnp.uint32(1 << PARITY_SHIFT_BIT))) & jnp.uint32(0xFFFF0000)
    return lax.bitcast_convert_type(w, jnp.float32)


def _group_shift_words(codes):
    par = (codes & 1).reshape(-1, PEER_GROUP)
    return jnp.sum(par << (PARITY_SHIFT_BIT + jnp.arange(PEER_GROUP, dtype=jnp.int32)), axis=1, dtype=jnp.int32)


def _splat_u32(s):
    return lax.bitcast_convert_type(jnp.full((ROW_SUBLANES, 128), s, jnp.int32), jnp.uint32)


def _sublane_sums(tiles):
    sub = lax.broadcasted_iota(jnp.int32, (ROW_SUBLANES, 128), 0)
    step = ROW_SUBLANES // 2
    while step >= 1:
        low = (sub % (2 * step)) < step
        tiles = [jnp.where(low, a + pltpu.roll(a, ROW_SUBLANES - step, 0), pltpu.roll(b, step, 0) + b)
                 for a, b in zip(tiles[:len(tiles) // 2], tiles[len(tiles) // 2:])]
        step //= 2
    return tiles[0]


PEER_GROUP = 16


PEER_GROUPS_PER_TOKEN = PEER_PICKS // PEER_GROUP


def _pipelined_groups(n_groups, produce, consume, stage_a, stage_b):
    def put(stage, tiles):
        for k, tile in enumerate(tiles):
            stage[k] = tile

    def get(stage):
        return [stage[k] for k in range(PEER_GROUP)]

    def body(j, carry):
        i = 2 * j + 1
        put(stage_b, produce(i))
        consume(i - 1, get(stage_a))
        put(stage_a, produce(i + 1))
        consume(i, get(stage_b))
        return carry

    put(stage_a, produce(0))
    lax.fori_loop(0, n_groups // 2 - 1, body, 0)
    put(stage_b, produce(n_groups - 1))
    consume(n_groups - 2, get(stage_a))
    consume(n_groups - 1, get(stage_b))


def _tree_sum(tiles):
    while len(tiles) > 1:
        tiles = [a + b for a, b in zip(tiles[0::2], tiles[1::2])]
    return tiles[0]


def _peer_up_kernel(start_ref, shift_ref, h_ref, gate_ref, tab_ref, coef_ref, dots_ref, stage_a, stage_b):
    GT = PEER_GT

    def products(i):
        t = i // PEER_GROUPS_PER_TOKEN
        h = h_ref[pl.ds(pl.multiple_of(t * ROW_SUBLANES, ROW_SUBLANES), ROW_SUBLANES), :]
        shifts = _splat_u32(shift_ref[i])
        return tuple(_load_row(tab_ref, start_ref[i * PEER_GROUP + kk], shifts >> jnp.uint32(kk)) * h
                     for kk in range(PEER_GROUP))

    def reduce_rows(i, prods):
        rows = [_sublane_sums(list(prods[k:k + ROW_SUBLANES])) for k in range(0, PEER_GROUP, ROW_SUBLANES)]
        dots_ref[pl.ds(pl.multiple_of(i * PEER_GROUP, PEER_GROUP), PEER_GROUP), :] = jnp.concatenate(rows, axis=0)

    _pipelined_groups(GT * PEER_GROUPS_PER_TOKEN, products, reduce_rows, stage_a, stage_b)
    rows = [jnp.sum(dots_ref[t * PEER_PICKS:(t + 1) * PEER_PICKS, :].T, axis=0, keepdims=True) for t in range(GT)]
    coef_ref[...] = gate_ref[...] * jax.nn.gelu(jnp.concatenate(rows, axis=0))


def _peer_down_kernel(start_ref, shift_ref, coef_ref, tab_ref, o_ref, part_ref, stage_a, stage_b):
    GT = PEER_GT

    def products(i):
        shifts = _splat_u32(shift_ref[i])
        return tuple(_load_row(tab_ref, start_ref[i * PEER_GROUP + kk], shifts >> jnp.uint32(kk))
                     * jnp.full((ROW_SUBLANES, 128), coef_ref[i * PEER_GROUP + kk], jnp.float32)
                     for kk in range(PEER_GROUP))

    def partial_sum(i, prods):
        part_ref[pl.ds(pl.multiple_of(i * ROW_SUBLANES, ROW_SUBLANES), ROW_SUBLANES), :] = _tree_sum(list(prods))

    _pipelined_groups(GT * PEER_GROUPS_PER_TOKEN, products, partial_sum, stage_a, stage_b)
    parts = part_ref[...].reshape(GT, PEER_GROUPS_PER_TOKEN, ROW_SUBLANES, 128)
    o_ref[...] = jnp.sum(parts, axis=1).reshape(GT * ROW_SUBLANES, 128)


def _peer_gather(h, experts, gates, u_tab, v_tab):
    n = h.shape[0]
    GT = PEER_GT
    tab_rows = N_EXPERTS // 2 * ROW_SUBLANES
    flat_smem = pl.BlockSpec((GT * PEER_PICKS,), lambda i: (i,), memory_space=pltpu.MemorySpace.SMEM)
    tab_spec = pl.BlockSpec((tab_rows, 128), lambda i: (0, 0), pipeline_mode=pl.Buffered(1))
    params = pltpu.CompilerParams(dimension_semantics=("arbitrary",), vmem_limit_bytes=56 * 1024 * 1024)
    group_smem = pl.BlockSpec((GT * PEER_GROUPS_PER_TOKEN,), lambda i: (i,), memory_space=pltpu.MemorySpace.SMEM)
    starts = (experts & ~(ROW_SUBLANES - 1)).reshape(n * PEER_PICKS)
    shifts = _group_shift_words(experts)
    stage = pltpu.VMEM((PEER_GROUP, ROW_SUBLANES, 128), jnp.float32)
    coef = pl.pallas_call(
        _peer_up_kernel,
        grid=(n // GT,),
        in_specs=[flat_smem, group_smem,
                  pl.BlockSpec((GT * ROW_SUBLANES, 128), lambda i: (i, 0)),
                  pl.BlockSpec((GT, PEER_PICKS), lambda i: (i, 0)),
                  tab_spec],
        out_specs=pl.BlockSpec((GT, PEER_PICKS), lambda i: (i, 0)),
        out_shape=jax.ShapeDtypeStruct((n, PEER_PICKS), jnp.float32),
        scratch_shapes=[pltpu.VMEM((GT * PEER_PICKS, 128), jnp.float32), stage, stage],
        compiler_params=params,
        name="peer_up",
    )(starts, shifts, h.reshape(n * ROW_SUBLANES, 128), gates, u_tab)
    out = pl.pallas_call(
        _peer_down_kernel,
        grid=(n // GT,),
        in_specs=[flat_smem, group_smem, flat_smem, tab_spec],
        out_specs=pl.BlockSpec((GT * ROW_SUBLANES, 128), lambda i: (i, 0)),
        out_shape=jax.ShapeDtypeStruct((n * ROW_SUBLANES, 128), jnp.float32),
        scratch_shapes=[pltpu.VMEM((GT * PEER_GROUPS_PER_TOKEN * ROW_SUBLANES, 128), jnp.float32), stage, stage],
        compiler_params=params,
        name="peer_down",
    )(starts, shifts, coef.reshape(n * PEER_PICKS), v_tab)
    return out.reshape(n, D_MODEL)


def _layer_weights(w_in, w_ssm_branch, w_attn_branch, w_out, peer_wq, peer_sub_keys, peer_u, peer_v):
    bf16 = jnp.bfloat16
    offs = np.cumsum((0,) + PROJ_SIZES).tolist()
    seg = lambda k: w_in[:, offs[k]:offs[k + 1]]
    pad = jnp.zeros((D_MODEL, SMALL_WIDTH - SSM_HEADS - 3 * ATTN_HEADS), w_in.dtype)
    return dict(
        in_ssm=[seg(0).astype(bf16), seg(1).astype(bf16)],
        in_attn=[seg(3).astype(bf16), jnp.concatenate([seg(4), seg(5), seg(6)], axis=1).astype(bf16),
                 seg(8).astype(bf16), jnp.concatenate([seg(2), seg(7), pad], axis=1).astype(bf16)],
        w_ssm=w_ssm_branch.astype(bf16), w_attn=w_attn_branch.astype(bf16), w_out=w_out.astype(bf16),
        wq=peer_wq.astype(bf16),
        keys=peer_sub_keys.reshape(PEER_HEADS * 2, PEER_KEYS, PEER_HALF).astype(bf16),
        u_tab=_pack_rows(peer_u), v_tab=_pack_rows(peer_v))


def _block(x, conv_prev, ssm_prev, nsa_core, lw, norm_mix_w, conv_w, conv_b, dt_bias, a_log, d_skip, ssm_norm_w,
           norm_ffn_w):
    b, T, _ = x.shape
    n = b * T
    xf = x.reshape(n, D_MODEL)
    z, xbc = _norm_proj(xf, norm_mix_w, lw["in_ssm"], "in_proj_ssm")
    q, kv, br_g, small = _norm_proj(xf, norm_mix_w, lw["in_attn"], "in_proj_attn")
    nsa_g = small[:, SSM_HEADS:SSM_HEADS + 3 * ATTN_HEADS]
    kv_c, kv_s, kv_w = (kv[:, k * KV_WIDTH:(k + 1) * KV_WIDTH].reshape(b, T, KV_WIDTH) for k in range(3))
    y_ssm, conv_new, ssm_new = _mamba_branch(z.reshape(b, T, D_INNER), xbc.reshape(b, T, CONV_DIM),
                                             small.reshape(b, T, SMALL_WIDTH), conv_prev, ssm_prev, conv_w, conv_b,
                                             dt_bias, a_log, d_skip, ssm_norm_w)
    o_attn, win_new = nsa_core(q.reshape(b, T, ATTN_WIDTH), kv_c, kv_s, kv_w, nsa_g.reshape(b, T, 3 * ATTN_HEADS))
    x2, h2 = _merge_proj(xf, o_attn.reshape(n, ATTN_WIDTH), y_ssm.reshape(n, D_INNER), br_g, norm_ffn_w,
                         lw["w_attn"], lw["w_ssm"], lw["w_out"])
    experts, gates = _peer_route(h2, lw["wq"], lw["keys"])
    peer = _peer_gather(h2, experts, gates, lw["u_tab"], lw["v_tab"])
    kv_shape = (b, T, KV_HEADS, 2, ATTN_HEAD_DIM)
    return x2, peer, kv_c.reshape(kv_shape), kv_s.reshape(kv_shape), win_new, conv_new, ssm_new


def kernel(x_prompt, x_sample, cache_cmp_kv, cache_sel_kv, cache_win_kv, state_ssm, state_conv, page_table,
           norm_mix_w, w_in, conv_w, conv_b, dt_bias, a_log, d_skip, ssm_norm_w, w_ssm_branch, w_attn_branch,
           w_out, norm_ffn_w, peer_wq, peer_sub_keys, peer_u, peer_v, final_norm_w):
    xp, xs = x_prompt, x_sample
    cmp_p, cmp_s, sel_p, sel_s, win_p, win_s = [], [], [], [], [], []
    ssm_p, ssm_s, conv_p, conv_s = [], [], [], []
    for layer in range(DEPTH):
        lw = _layer_weights(*(a[layer] for a in (w_in, w_ssm_branch, w_attn_branch, w_out, peer_wq, peer_sub_keys,
                                                   peer_u, peer_v)))
        rest = [a[layer] for a in (norm_mix_w, conv_w, conv_b, dt_bias, a_log, d_skip, ssm_norm_w, norm_ffn_w)]
        last = layer == DEPTH - 1
        bp = xp.shape[0]
        conv0 = jnp.zeros((bp, CONV_WIDTH - 1, CONV_DIM), xp.dtype)
        ssm0 = jnp.zeros((bp, SSM_HEADS, SSM_HEAD_DIM, SSM_STATE), xp.dtype)
        x2, peer, kc, ks, kw, cv, ss = _block(xp, conv0, ssm0, _nsa_prompt, lw, *rest)
        xp = (_add_norm(x2, peer, final_norm_w) if last else x2 + peer).reshape(xp.shape)
        cmp_p.append(kc); sel_p.append(ks); win_p.append(kw); conv_p.append(cv); ssm_p.append(ss)
        core = functools.partial(_nsa_sample, cache_cmp_kv[layer], cache_sel_kv[layer], cache_win_kv[layer], page_table)
        x2, peer, kc, ks, kw, cv, ss = _block(xs, state_conv[layer], state_ssm[layer], core, lw, *rest)
        xs = (_add_norm(x2, peer, final_norm_w) if last else x2 + peer).reshape(xs.shape)
        cmp_s.append(kc); sel_s.append(ks); win_s.append(kw); conv_s.append(cv); ssm_s.append(ss)
    return (xp, xs, jnp.stack(cmp_p), jnp.stack(cmp_s), jnp.stack(sel_p), jnp.stack(sel_s),
            jnp.stack(win_p), jnp.stack(win_s), jnp.stack(ssm_p), jnp.stack(ssm_s), jnp.stack(conv_p), jnp.stack(conv_s))
```

```python
import math, functools
import jax, jax.numpy as jnp
from jax import lax
import numpy as np
from jax.experimental import pallas as pl
from jax.experimental.pallas import tpu as pltpu

D_MODEL = 1024
BATCH = 8
SEQ = 4096
DEPTH = 1
DEC_BATCH = 32
DEC_SEQ = 8
PAST_LEN = 16384
PAGE_SIZE = 128

SSM_EXPAND = 2
D_INNER = SSM_EXPAND * D_MODEL
SSM_HEAD_DIM = 64
SSM_HEADS = D_INNER // SSM_HEAD_DIM
SSM_GROUPS = 4
SSM_HPG = SSM_HEADS // SSM_GROUPS
SSM_STATE = 128
CONV_WIDTH = 4
CONV_DIM = D_INNER + 2 * SSM_GROUPS * SSM_STATE
SSD_CHUNK = 128
ATTN_HEADS = 16
ATTN_HEAD_DIM = 64
ATTN_WIDTH = ATTN_HEADS * ATTN_HEAD_DIM
KV_HEADS = 2
HEADS_PER_KV = ATTN_HEADS // KV_HEADS
KV_WIDTH = 2 * KV_HEADS * ATTN_HEAD_DIM
CMP_BLOCK = 64
SEL_BLOCK = CMP_BLOCK
SEL_TOPK = 16
SEL_BONUS = 1000.0
WINDOW = 512
WIN_QBLOCK = 128
SEL_QBLOCK = 32
ATTN_SCALE = ATTN_HEAD_DIM ** -0.5
PEER_HEADS = 8
PEER_KEYS = 128
N_EXPERTS = PEER_KEYS * PEER_KEYS
PEER_QDIM = 256
PEER_HALF = PEER_QDIM // 2
PEER_TOPK = 16
PEER_TOKEN_CHUNK = 128
EPS = 1e-6
PROJ_SIZES = (D_INNER, CONV_DIM, SSM_HEADS, ATTN_WIDTH, KV_WIDTH, KV_WIDTH, KV_WIDTH, 3 * ATTN_HEADS, 2 * D_MODEL)
PROJ_DIM = sum(PROJ_SIZES)


PROJ_TM = 512
PROJ_VMEM_BYTES = 56 * 1024 * 1024


def _rms(x, w):
    return x * lax.rsqrt(jnp.mean(x * x, axis=-1, keepdims=True) + EPS) * w


def _resident(shape):
    return pl.BlockSpec(shape, lambda i: (0,) * len(shape), pipeline_mode=pl.Buffered(1))


def _norm_proj_kernel(x_ref, nw_ref, *refs):
    n_out = len(refs) // 2
    h = _rms(x_ref[...], nw_ref[...]).astype(jnp.bfloat16)
    for w_ref, o_ref in zip(refs[:n_out], refs[n_out:]):
        o_ref[...] = jnp.dot(h, w_ref[...], preferred_element_type=jnp.float32)


def _norm_proj(x, norm_w, weights, name):
    n = x.shape[0]
    tm = min(PROJ_TM, n)
    return pl.pallas_call(
        _norm_proj_kernel,
        grid=(n // tm,),
        in_specs=[pl.BlockSpec((tm, D_MODEL), lambda i: (i, 0)), _resident((1, D_MODEL))]
                 + [_resident(w.shape) for w in weights],
        out_specs=[pl.BlockSpec((tm, w.shape[1]), lambda i: (i, 0)) for w in weights],
        out_shape=[jax.ShapeDtypeStruct((n, w.shape[1]), jnp.float32) for w in weights],
        compiler_params=pltpu.CompilerParams(dimension_semantics=("arbitrary",), vmem_limit_bytes=PROJ_VMEM_BYTES),
        name=name,
    )(x, norm_w.reshape(1, D_MODEL), *weights)


def _merge_proj_kernel(x_ref, oa_ref, ys_ref, g_ref, nw_ref, wa_ref, ws_ref, wo_ref, x2_ref, h2_ref):
    f32, bf16 = jnp.float32, jnp.bfloat16
    y_attn = jnp.dot(oa_ref[...].astype(bf16), wa_ref[...], preferred_element_type=f32)
    y_ssm = jnp.dot(ys_ref[...].astype(bf16), ws_ref[...], preferred_element_type=f32)
    g = jax.nn.sigmoid(g_ref[...])
    merged = g[:, :D_MODEL] * y_ssm + g[:, D_MODEL:] * y_attn
    x2 = x_ref[...] + jnp.dot(merged.astype(bf16), wo_ref[...], preferred_element_type=f32)
    x2_ref[...] = x2
    h2_ref[...] = _rms(x2, nw_ref[...])


def _merge_proj(x, o_attn, y_ssm, br_g, norm_w, wa_b, ws_b, wo_b):
    n = x.shape[0]
    tm = min(PROJ_TM, n)
    rows = lambda c: pl.BlockSpec((tm, c), lambda i: (i, 0))
    return pl.pallas_call(
        _merge_proj_kernel,
        grid=(n // tm,),
        in_specs=[rows(D_MODEL), rows(ATTN_WIDTH), rows(D_INNER), rows(2 * D_MODEL), _resident((1, D_MODEL)),
                  _resident(wa_b.shape), _resident(ws_b.shape), _resident(wo_b.shape)],
        out_specs=[rows(D_MODEL), rows(D_MODEL)],
        out_shape=[jax.ShapeDtypeStruct((n, D_MODEL), jnp.float32)] * 2,
        compiler_params=pltpu.CompilerParams(dimension_semantics=("arbitrary",), vmem_limit_bytes=PROJ_VMEM_BYTES),
        name="merge_proj",
    )(x, o_attn, y_ssm, br_g, norm_w.reshape(1, D_MODEL), wa_b, ws_b, wo_b)


def _add_norm_kernel(x_ref, y_ref, w_ref, o_ref):
    o_ref[...] = _rms(x_ref[...] + y_ref[...], w_ref[...])


def _add_norm(x, y, w):
    n = x.shape[0]
    tm = min(PROJ_TM, n)
    rows = pl.BlockSpec((tm, D_MODEL), lambda i: (i, 0))
    return pl.pallas_call(
        _add_norm_kernel,
        grid=(n // tm,),
        in_specs=[rows, rows, _resident((1, D_MODEL))],
        out_specs=rows,
        out_shape=jax.ShapeDtypeStruct((n, D_MODEL), jnp.float32),
        name="add_norm",
    )(x, y, w.reshape(1, D_MODEL))


SMALL_WIDTH = 128
CONV_TAIL = 8


def _ssd_kernel(z_ref, xbc_ref, sm_ref, cprev_ref, h0_ref, cw_ref, cbias_ref, dtb_ref, alog_ref, dskip_ref, nw_ref,
                y_ref, hout_ref, state_ref, tail_ref, *, valid):
    f32, bf16 = jnp.float32, jnp.bfloat16
    L, P, N = SSD_CHUNK, SSM_HEAD_DIM, SSM_STATE
    c = pl.program_id(1)

    @pl.when(c == 0)
    def _init():
        state_ref[...] = h0_ref[0]
        tail_ref[...] = cprev_ref[0]

    xbc = xbc_ref[0]
    padded = jnp.concatenate([tail_ref[...], xbc], axis=0)
    acc = jnp.broadcast_to(cbias_ref[...], (L, CONV_DIM))
    for w in range(CONV_WIDTH):
        lo = CONV_TAIL - (CONV_WIDTH - 1) + w
        acc = acc + cw_ref[w:w + 1, :] * padded[lo:lo + L, :]
    xc = acc * jax.nn.sigmoid(acc)
    tail_ref[...] = xbc[L - CONV_TAIL:, :]

    row = lax.broadcasted_iota(jnp.int32, (L, 1), 0)
    live = row < (valid - c * L)
    dt = jnp.where(live, jax.nn.softplus(sm_ref[0] + dtb_ref[...]), 0.0)
    a = dt * (-jnp.exp(alog_ref[...]))
    tril = (row >= lax.broadcasted_iota(jnp.int32, (L, L), 1))
    a_cum = jnp.dot(tril.astype(f32), a, preferred_element_type=f32, precision=lax.Precision.HIGHEST)
    a_cum_t = a_cum.T
    a_last = a_cum[L - 1:L, :]
    grow = jnp.exp(a_cum)
    to_end = jnp.exp(a_last - a_cum)
    carry = jnp.exp(a_last)

    W = SSM_HPG * P

    def per_head(v, g):
        cols = [jnp.broadcast_to(v[:, g * SSM_HPG + j:g * SSM_HPG + j + 1], (v.shape[0], P)) for j in range(SSM_HPG)]
        return jnp.concatenate(cols, axis=1)

    xs = xc[:, :D_INNER]
    ys = []
    for g in range(SSM_GROUPS):
        bg = xc[:, D_INNER + g * N:D_INNER + (g + 1) * N]
        cg = xc[:, D_INNER + SSM_GROUPS * N + g * N:D_INNER + SSM_GROUPS * N + (g + 1) * N]
        bg_b, cg_b = bg.astype(bf16), cg.astype(bf16)
        bg_t = bg.T.astype(bf16)
        cb = lax.dot_general(cg_b, bg_b, (((1,), (1,)), ((), ())), preferred_element_type=f32)
        xs_g = xs[:, g * W:(g + 1) * W]
        xdt = xs_g * per_head(dt, g)
        xdt_b = xdt.astype(bf16)
        st = state_ref[g]
        y = jnp.dot(cg_b, st.astype(bf16), preferred_element_type=f32) * per_head(grow, g)
        state_ref[g] = st * per_head(carry, g) + jnp.dot(bg_t, (xdt * per_head(to_end, g)).astype(bf16),
                                                         preferred_element_type=f32)
        diag = []
        for j in range(SSM_HPG):
            h = g * SSM_HPG + j
            diff = a_cum[:, h:h + 1] - a_cum_t[h:h + 1, :]
            m = (cb * jnp.exp(jnp.where(tril, diff, -jnp.inf))).astype(bf16)
            diag.append(jnp.dot(m, xdt_b[:, j * P:(j + 1) * P], preferred_element_type=f32))
        ys.append(jnp.concatenate(diag, axis=1) + y + per_head(dskip_ref[...], g) * xs_g)
    z = z_ref[0]
    y_ref[0] = _rms(jnp.concatenate(ys, axis=1) * (z * jax.nn.sigmoid(z)), nw_ref[...])

    @pl.when(c == pl.num_programs(1) - 1)
    def _emit():
        hout_ref[0] = state_ref[...]


def _mamba_branch(z, xbc, small, conv_prev, ssm_prev, conv_w, conv_b, dt_bias, a_log, d_skip, ssm_norm_w):
    b, L, _ = xbc.shape
    Lc = SSD_CHUNK
    Lp = -(-L // Lc) * Lc
    keep = CONV_WIDTH - 1
    conv_new = xbc[:, L - keep:] if L >= keep else jnp.concatenate([conv_prev, xbc], axis=1)[:, -keep:]
    if Lp != L:
        padrows = lambda a: jnp.pad(a, ((0, 0), (0, Lp - L), (0, 0)))
        z, xbc, small = padrows(z), padrows(xbc), padrows(small)
    cprev = jnp.pad(conv_prev, ((0, 0), (CONV_TAIL - (CONV_WIDTH - 1), 0), (0, 0)))
    G, J, P, N = SSM_GROUPS, SSM_HPG, SSM_HEAD_DIM, SSM_STATE
    state_block = (1, G, N, J * P)
    h0 = jnp.transpose(ssm_prev.reshape(b, G, J, P, N), (0, 1, 4, 2, 3)).reshape(b, G, N, J * P)
    lane = lambda v: jnp.pad(v.reshape(1, SSM_HEADS), ((0, 0), (0, SMALL_WIDTH - SSM_HEADS)))
    full = lambda shape: pl.BlockSpec(shape, lambda bi, c: (0,) * len(shape))
    kern = functools.partial(_ssd_kernel, valid=L)
    y, hout = pl.pallas_call(
        kern,
        grid=(b, Lp // Lc),
        in_specs=[pl.BlockSpec((1, Lc, D_INNER), lambda bi, c: (bi, c, 0)),
                  pl.BlockSpec((1, Lc, CONV_DIM), lambda bi, c: (bi, c, 0)),
                  pl.BlockSpec((1, Lc, SMALL_WIDTH), lambda bi, c: (bi, c, 0)),
                  pl.BlockSpec((1, CONV_TAIL, CONV_DIM), lambda bi, c: (bi, 0, 0)),
                  pl.BlockSpec(state_block, lambda bi, c: (bi, 0, 0, 0)),
                  full((CONV_WIDTH, CONV_DIM)), full((1, CONV_DIM)), full((1, SMALL_WIDTH)), full((1, SMALL_WIDTH)),
                  full((1, SMALL_WIDTH)), full((1, D_INNER))],
        out_specs=[pl.BlockSpec((1, Lc, D_INNER), lambda bi, c: (bi, c, 0)),
                   pl.BlockSpec(state_block, lambda bi, c: (bi, 0, 0, 0))],
        out_shape=[jax.ShapeDtypeStruct((b, Lp, D_INNER), jnp.float32),
                   jax.ShapeDtypeStruct((b,) + state_block[1:], jnp.float32)],
        scratch_shapes=[pltpu.VMEM(state_block[1:], jnp.float32),
                        pltpu.VMEM((CONV_TAIL, CONV_DIM), jnp.float32)],
        compiler_params=pltpu.CompilerParams(dimension_semantics=("arbitrary", "arbitrary"),
                                             vmem_limit_bytes=48 * 1024 * 1024),
        name="mamba_ssd",
    )(z, xbc, small, cprev, h0, conv_w, conv_b.reshape(1, CONV_DIM), lane(dt_bias), lane(a_log), lane(d_skip),
      ssm_norm_w.reshape(1, D_INNER))
    ssm_new = jnp.transpose(hout.reshape(b, G, N, J, P), (0, 1, 3, 4, 2)).reshape(b, SSM_HEADS, P, N)
    return y[:, :L], conv_new, ssm_new


NSA_TQ = 128
NSA_TK = 512
NSA_WSPAN = WINDOW + NSA_TQ
NEG_BIG = -1e30


def _nsa_prompt_kernel(q_ref, kvc_ref, ks_ref, vst_ref, kw_ref, vwt_ref, gate_ref, o_ref,
                       pk_ref, pvt_ref, m_ref, l_ref, acc_ref, *, seq):
    f32, bf16 = jnp.float32, jnp.bfloat16
    D, J, TQ, TK = ATTN_HEAD_DIM, HEADS_PER_KV, NSA_TQ, NSA_TK
    n_blk = seq // CMP_BLOCK
    i = pl.program_id(2)
    t0 = i * TQ

    @pl.when(i == 0)
    def _pool():
        c = kvc_ref[0]
        pooled = jnp.sum(c.reshape(n_blk, CMP_BLOCK, 2 * D), axis=1) * (1.0 / CMP_BLOCK)
        pk_ref[...] = pooled[:, :D].astype(bf16)
        pvt_ref[...] = pooled[:, D:].T.astype(bf16)

    qt = (q_ref[0] * ATTN_SCALE).T
    qs = jnp.concatenate([qt[j * D:(j + 1) * D, :] for j in range(J)], axis=1).astype(bf16)
    qpos1 = t0 + lax.broadcasted_iota(jnp.int32, (1, TQ), 1)
    qpos = jnp.concatenate([qpos1] * J, axis=1)

    s = jnp.dot(pk_ref[...], qs, preferred_element_type=f32)
    blk = lax.broadcasted_iota(jnp.int32, (n_blk, 1), 0)
    vis = (blk * CMP_BLOCK + (CMP_BLOCK - 1)) <= qpos
    s = jnp.where(vis, s, -jnp.inf)
    m = jnp.max(s, axis=0, keepdims=True)
    m = jnp.where(m == -jnp.inf, 0.0, m)
    p = jnp.exp(s - m)
    p = p / jnp.maximum(jnp.sum(p, axis=0, keepdims=True), 1e-30)
    o_cmp = jnp.dot(pvt_ref[...], p.astype(bf16), preferred_element_type=f32)
    imp = p[:, 0:TQ]
    for j in range(1, J):
        imp = imp + p[:, j * TQ:(j + 1) * TQ]

    cur = qpos1 // CMP_BLOCK
    blk2 = lax.broadcasted_iota(jnp.int32, (n_blk, TQ), 0)
    forced = ((blk2 == 0) | (blk2 == cur) | (blk2 == cur - 1)).astype(f32)
    score = jnp.where(blk2 <= cur, imp + SEL_BONUS * forced, -jnp.inf)
    sel = jnp.zeros((n_blk, TQ), f32)
    for _ in range(min(SEL_TOPK, n_blk)):
        mx = jnp.max(score, axis=0, keepdims=True)
        first = jnp.min(jnp.where(score == mx, blk2, n_blk), axis=0, keepdims=True)
        pick = blk2 == first
        sel = jnp.where(pick, 1.0, sel)
        score = jnp.where(pick, -jnp.inf, score)
    sel_bias = ((sel - 1.0) * -NEG_BIG).astype(bf16)
    qs_sel = jnp.concatenate([qs, jnp.concatenate([sel_bias] * J, axis=1)], axis=0)

    m_ref[...] = jnp.full(m_ref.shape, NEG_BIG, f32)
    l_ref[...] = jnp.zeros(l_ref.shape, f32)
    acc_ref[...] = jnp.zeros(acc_ref.shape, f32)
    n_kt = (t0 + TQ + TK - 1) // TK

    def sel_tile(kt, causal):
        k0 = pl.multiple_of(kt * TK, TK)
        st = jnp.dot(ks_ref[0, 0, pl.ds(k0, TK), :], qs_sel, preferred_element_type=f32)
        if causal:
            kpos = k0 + lax.broadcasted_iota(jnp.int32, (TK, 1), 0)
            st = st + jnp.concatenate([jnp.where(kpos <= qpos1, 0.0, NEG_BIG)] * J, axis=1)
        m_old = m_ref[...]
        m_new = jnp.maximum(m_old, jnp.max(st, axis=0, keepdims=True))
        alpha = jnp.exp(m_old - m_new)
        pt = jnp.exp(st - m_new)
        l_ref[...] = alpha * l_ref[...] + jnp.sum(pt, axis=0, keepdims=True)
        acc_ref[...] = alpha * acc_ref[...] + jnp.dot(vst_ref[0, 0, :, pl.ds(k0, TK)], pt.astype(bf16),
                                                      preferred_element_type=f32)
        m_ref[...] = m_new

    def past_tile(kt, carry):
        sel_tile(kt, causal=False)
        return carry

    lax.fori_loop(0, n_kt - 1, past_tile, 0)
    sel_tile(n_kt - 1, causal=True)
    o_sel = acc_ref[...] / l_ref[...]

    w0 = pl.multiple_of(jnp.maximum(t0 - WINDOW, 0), TQ)
    span = min(NSA_WSPAN, seq)
    kposw = w0 + lax.broadcasted_iota(jnp.int32, (span, 1), 0)
    rel = qpos1 - kposw
    biasw1 = jnp.where((rel >= 0) & (rel <= WINDOW), 0.0, NEG_BIG)
    biasw = jnp.concatenate([biasw1] * J, axis=1)
    sw = jnp.dot(kw_ref[0, 0, pl.ds(w0, span), :], qs, preferred_element_type=f32) + biasw
    pw = jnp.exp(sw - jnp.max(sw, axis=0, keepdims=True))
    lw = jnp.sum(pw, axis=0, keepdims=True)
    o_win = jnp.dot(vwt_ref[0, 0, :, pl.ds(w0, span)], pw.astype(bf16), preferred_element_type=f32) / lw

    def gate_row(br):
        g = jax.nn.sigmoid(gate_ref[0, 0, br])
        return jnp.concatenate([g[j:j + 1, :] for j in range(J)], axis=1)

    out = gate_row(0) * o_cmp + gate_row(1) * o_sel + gate_row(2) * o_win
    out = jnp.concatenate([out[:, j * TQ:(j + 1) * TQ] for j in range(J)], axis=0)
    o_ref[0] = out.T


def _nsa_prompt_attention(q, kv_c, kv_s, kv_w, nsa_g):
    b, T, _ = q.shape
    G, J, D, TQ = KV_HEADS, HEADS_PER_KV, ATTN_HEAD_DIM, NSA_TQ
    bf16 = jnp.bfloat16

    def split_kv(kv):
        r = kv.reshape(b, T, G, 2, D).astype(bf16)
        return jnp.transpose(r[:, :, :, 0], (0, 2, 1, 3)), jnp.transpose(r[:, :, :, 1], (0, 2, 3, 1))

    ks, vst = split_kv(kv_s)
    kw, vwt = split_kv(kv_w)
    gates_t = jnp.transpose(nsa_g.reshape(b, T, G, J, 3), (0, 2, 4, 3, 1))
    n_blk = T // CMP_BLOCK
    member = (jnp.arange(T)[:, None] // SEL_BLOCK == jnp.arange(n_blk)[None, :]).astype(bf16)
    ks = jnp.concatenate([ks, jnp.broadcast_to(member, (b, G, T, n_blk))], axis=-1)
    kern = functools.partial(_nsa_prompt_kernel, seq=T)
    return pl.pallas_call(
        kern,
        grid=(b, G, T // TQ),
        in_specs=[
            pl.BlockSpec((1, TQ, J * D), lambda bi, g, i: (bi, i, g)),
            pl.BlockSpec((1, T, 2 * D), lambda bi, g, i: (bi, 0, g)),
            pl.BlockSpec((1, 1, T, D + n_blk), lambda bi, g, i: (bi, g, 0, 0)),
            pl.BlockSpec((1, 1, D, T), lambda bi, g, i: (bi, g, 0, 0)),
            pl.BlockSpec((1, 1, T, D), lambda bi, g, i: (bi, g, 0, 0)),
            pl.BlockSpec((1, 1, D, T), lambda bi, g, i: (bi, g, 0, 0)),
            pl.BlockSpec((1, 1, 3, J, TQ), lambda bi, g, i: (bi, g, 0, 0, i)),
        ],
        out_specs=pl.BlockSpec((1, TQ, J * D), lambda bi, g, i: (bi, i, g)),
        out_shape=jax.ShapeDtypeStruct((b, T, G * J * D), jnp.float32),
        scratch_shapes=[
            pltpu.VMEM((n_blk, D), bf16),
            pltpu.VMEM((D, n_blk), bf16),
            pltpu.VMEM((1, J * TQ), jnp.float32),
            pltpu.VMEM((1, J * TQ), jnp.float32),
            pltpu.VMEM((D, J * TQ), jnp.float32),
        ],
        compiler_params=pltpu.CompilerParams(
            dimension_semantics=("arbitrary", "arbitrary", "arbitrary"),
            vmem_limit_bytes=48 * 1024 * 1024),
        name="nsa_prompt",
    )(q, kv_c, ks, vst, kw, vwt, gates_t)


def _nsa_prompt(q, kv_c, kv_s, kv_w, nsa_g):
    b, T = q.shape[:2]
    win_new = kv_w.reshape(b, T, KV_HEADS, 2, ATTN_HEAD_DIM)[:, -min(WINDOW, T):]
    return _nsa_prompt_attention(q, kv_c, kv_s, kv_w, nsa_g), win_new


NSAS_PPS = 8


def _nsa_sample_kernel(pt_ref, qs_ref, gate_ref, *refs, n_pages, t_len, w_eff):
    f32, bf16 = jnp.float32, jnp.bfloat16
    PPS, G, D, J, T = NSAS_PPS, KV_HEADS, ATTN_HEAD_DIM, HEADS_PER_KV, t_len
    cmp_refs, sel_refs = refs[:PPS], refs[PPS:2 * PPS]
    tail_ref, win_ref, o_ref, pooled_ref, ocmp_ref, sel_ref, m_ref, l_ref, acc_ref = refs[2 * PPS:]
    C = J * T
    ph, pg = pl.program_id(1), pl.program_id(2)
    n_pg = n_pages // PPS
    past = n_pages * PAGE_SIZE
    n_blk = past // CMP_BLOCK
    n_sel_pad = sel_ref.shape[1]
    blk_per_page = PAGE_SIZE // CMP_BLOCK
    col_t = lax.broadcasted_iota(jnp.int32, (1, C), 1) % T
    qpos = past + col_t

    def queries(g):
        return (qs_ref[0, g] * ATTN_SCALE).astype(bf16)

    def flash(g, kv, ok):
        k, v = kv[:, :D], kv[:, D:]
        st = jnp.where(ok, jnp.dot(k.astype(bf16), queries(g), preferred_element_type=f32), NEG_BIG)
        m_old = m_ref[g]
        m_new = jnp.maximum(m_old, jnp.max(st, axis=0, keepdims=True))
        alpha = jnp.exp(m_old - m_new)
        p = jnp.exp(st - m_new)
        l_ref[g] = alpha * l_ref[g] + jnp.sum(p, axis=0, keepdims=True)
        acc_ref[g] = alpha * acc_ref[g] + jnp.dot(v.T.astype(bf16), p.astype(bf16), preferred_element_type=f32)
        m_ref[g] = m_new

    @pl.when(ph == 0)
    def _compressed():
        means = []
        for r in range(PPS):
            page = cmp_refs[r][0]
            means.append(jnp.sum(page.reshape(blk_per_page, CMP_BLOCK, G * 2 * D), axis=1) * (1.0 / CMP_BLOCK))
        rows = PPS * blk_per_page
        pooled_ref[pl.ds(pl.multiple_of(pg * rows, rows), rows), :] = jnp.concatenate(means, axis=0)

        @pl.when(pg == n_pg - 1)
        def _select():
            blk = lax.broadcasted_iota(jnp.int32, (n_blk, 1), 0)
            vis = (blk * CMP_BLOCK + (CMP_BLOCK - 1)) <= qpos
            fold = (lax.broadcasted_iota(jnp.int32, (C, T), 0) % T == lax.broadcasted_iota(jnp.int32, (C, T), 1))
            blk2 = lax.broadcasted_iota(jnp.int32, (n_sel_pad, T), 0)
            cur = (past + lax.broadcasted_iota(jnp.int32, (1, T), 1)) // SEL_BLOCK
            forced = ((blk2 == 0) | (blk2 == cur) | (blk2 == cur - 1)).astype(f32)
            for g in range(G):
                pooled = pooled_ref[:, g * 2 * D:(g + 1) * 2 * D]
                s = jnp.dot(pooled[:, :D].astype(bf16), queries(g), preferred_element_type=f32)
                s = jnp.where(vis, s, -jnp.inf)
                m = jnp.max(s, axis=0, keepdims=True)
                m = jnp.where(m == -jnp.inf, 0.0, m)
                p = jnp.exp(s - m)
                p = p / jnp.maximum(jnp.sum(p, axis=0, keepdims=True), 1e-30)
                ocmp_ref[g] = jnp.dot(pooled[:, D:].T.astype(bf16), p.astype(bf16), preferred_element_type=f32)
                imp = jnp.dot(p, fold.astype(f32), preferred_element_type=f32, precision=lax.Precision.HIGHEST)
                imp = jnp.concatenate([imp, jnp.zeros((n_sel_pad - n_blk, T), f32)], axis=0)
                score = jnp.where(blk2 <= cur, imp + SEL_BONUS * forced, -jnp.inf)
                sel = jnp.zeros((n_sel_pad, T), f32)
                for _ in range(SEL_TOPK):
                    mx = jnp.max(score, axis=0, keepdims=True)
                    first = jnp.min(jnp.where(score == mx, blk2, n_sel_pad), axis=0, keepdims=True)
                    pick = blk2 == first
                    sel = jnp.where(pick, 1.0, sel)
                    score = jnp.where(pick, -jnp.inf, score)
                sel_ref[g] = jnp.concatenate([sel] * J, axis=1)

    @pl.when(ph == 1)
    def _selected():
        @pl.when(pg == 0)
        def _init():
            m_ref[...] = jnp.full(m_ref.shape, NEG_BIG, f32)
            l_ref[...] = jnp.zeros(l_ref.shape, f32)
            acc_ref[...] = jnp.zeros(acc_ref.shape, f32)

        key = lax.broadcasted_iota(jnp.int32, (PAGE_SIZE, 1), 0)
        for r in range(PPS):
            page = sel_refs[r][0]
            first_blk = (pg * PPS + r) * blk_per_page
            for g in range(G):
                chosen = jnp.where(key < CMP_BLOCK, sel_ref[g, pl.ds(first_blk, 1), :], sel_ref[g, pl.ds(first_blk + 1, 1), :])
                flash(g, page[:, g * 2 * D:(g + 1) * 2 * D], chosen > 0.5)

        @pl.when(pg == n_pg - 1)
        def _finish():
            tail = tail_ref[0]
            wkeys = win_ref[0]
            span = wkeys.shape[0]
            wi = lax.broadcasted_iota(jnp.int32, (span, 1), 0)
            rel = qpos - (past - w_eff + wi)
            wok = (rel >= 0) & (rel <= WINDOW) & (wi < w_eff + T)
            for g in range(G):
                ok = (sel_ref[g, n_blk:n_blk + 1, :] > 0.5) & (key <= col_t) & (key < T)
                flash(g, tail[:, g * 2 * D:(g + 1) * 2 * D], ok)
                o_sel = acc_ref[g] / l_ref[g]
                kvw = wkeys[:, g * 2 * D:(g + 1) * 2 * D]
                sw = jnp.where(wok, jnp.dot(kvw[:, :D].astype(bf16), queries(g), preferred_element_type=f32), NEG_BIG)
                pw = jnp.exp(sw - jnp.max(sw, axis=0, keepdims=True))
                o_win = jnp.dot(kvw[:, D:].T.astype(bf16), pw.astype(bf16), preferred_element_type=f32)
                o_win = o_win / jnp.sum(pw, axis=0, keepdims=True)
                gates = jax.nn.sigmoid(gate_ref[0, g])
                o_ref[0, g] = gates[0:1] * ocmp_ref[g] + gates[1:2] * o_sel + gates[2:3] * o_win


def _nsa_sample(pool_cmp, pool_sel, win_buf, page_table, q, kv_cmp, kv_sel, kv_win, nsa_g):
    b, T, _ = q.shape
    assert T < CMP_BLOCK and page_table.shape[1] % NSAS_PPS == 0
    G, J, D, PPS = KV_HEADS, HEADS_PER_KV, ATTN_HEAD_DIM, NSAS_PPS
    n_pages = page_table.shape[1]
    n_pg = n_pages // PPS
    past = n_pages * PAGE_SIZE
    n_blk = past // CMP_BLOCK
    n_sel_pad = -(-(n_blk + 1) // 8) * 8
    w_eff = win_buf.shape[1]
    C = J * T
    n_pool = pool_cmp.shape[0]
    pool_cmp = pool_cmp.reshape(n_pool, PAGE_SIZE, KV_WIDTH)
    pool_sel = pool_sel.reshape(n_pool, PAGE_SIZE, KV_WIDTH)
    qs = jnp.transpose(q.reshape(b, T, G, J, D), (0, 2, 4, 3, 1)).reshape(b, G, D, C)
    gates = jnp.transpose(nsa_g.reshape(b, T, G, J, 3), (0, 2, 4, 3, 1)).reshape(b, G, 3, C)
    tail = jnp.pad(kv_sel, ((0, 0), (0, PAGE_SIZE - T), (0, 0)))
    keys = jnp.concatenate([win_buf.reshape(b, w_eff, KV_WIDTH), kv_win], axis=1)
    span = -(-(w_eff + T) // PAGE_SIZE) * PAGE_SIZE
    wkeys = jnp.pad(keys, ((0, 0), (0, span - (w_eff + T)), (0, 0)))

    def cmp_map(r):
        return lambda bi, ph, pg, pt: (pt[bi, jnp.where(ph == 0, pg, n_pg - 1) * PPS + r], 0, 0)

    def sel_map(r):
        return lambda bi, ph, pg, pt: (pt[bi, jnp.where(ph == 1, pg, 0) * PPS + r], 0, 0)

    page_block = (1, PAGE_SIZE, KV_WIDTH)
    per_b = lambda shape: pl.BlockSpec(shape, lambda bi, ph, pg, pt: (bi,) + (0,) * (len(shape) - 1))
    kern = functools.partial(_nsa_sample_kernel, n_pages=n_pages, t_len=T, w_eff=w_eff)
    out = pl.pallas_call(
        kern,
        grid_spec=pltpu.PrefetchScalarGridSpec(
            num_scalar_prefetch=1,
            grid=(b, 2, n_pg),
            in_specs=[per_b((1, G, D, C)), per_b((1, G, 3, C))]
                     + [pl.BlockSpec(page_block, cmp_map(r)) for r in range(PPS)]
                     + [pl.BlockSpec(page_block, sel_map(r)) for r in range(PPS)]
                     + [per_b((1, PAGE_SIZE, KV_WIDTH)), per_b((1, span, KV_WIDTH))],
            out_specs=per_b((1, G, D, C)),
            scratch_shapes=[pltpu.VMEM((n_blk, KV_WIDTH), jnp.float32),
                            pltpu.VMEM((G, D, C), jnp.float32),
                            pltpu.VMEM((G, n_sel_pad, C), jnp.float32),
                            pltpu.VMEM((G, 1, C), jnp.float32),
                            pltpu.VMEM((G, 1, C), jnp.float32),
                            pltpu.VMEM((G, D, C), jnp.float32)]),
        out_shape=jax.ShapeDtypeStruct((b, G, D, C), jnp.float32),
        compiler_params=pltpu.CompilerParams(dimension_semantics=("arbitrary", "arbitrary", "arbitrary"),
                                             vmem_limit_bytes=32 * 1024 * 1024),
        name="nsa_sample",
    )(page_table, qs, gates, *([pool_cmp] * PPS), *([pool_sel] * PPS), tail, wkeys)
    o_attn = jnp.transpose(out.reshape(b, G, D, J, T), (0, 4, 1, 3, 2)).reshape(b, T, ATTN_WIDTH)
    return o_attn, keys[:, -w_eff:].reshape(b, w_eff, KV_HEADS, 2, D)


PEER_PICKS = PEER_HEADS * PEER_TOPK
PEER_TT = 128
PEER_GT = 64
ROW_SUBLANES = D_MODEL // 128


def _top16_cols(s, n_rows, iota_rows):
    vals, rows = [], []
    for _ in range(PEER_TOPK):
        mx = jnp.max(s, axis=0, keepdims=True)
        first = jnp.min(jnp.where(s == mx, iota_rows, n_rows), axis=0, keepdims=True)
        vals.append(mx)
        rows.append(first)
        s = jnp.where(iota_rows == first, -jnp.inf, s)
    return jnp.concatenate(vals, axis=0), jnp.concatenate(rows, axis=0)


def _peer_route_kernel(h_ref, wq_ref, keys_ref, exp_ref, gate_ref):
    f32 = jnp.float32
    TT = PEER_TT
    q = jnp.dot(h_ref[...].astype(jnp.bfloat16), wq_ref[...], preferred_element_type=f32)
    iota_k = lax.broadcasted_iota(jnp.int32, (PEER_KEYS, TT), 0)
    widths = [PEER_TOPK // (a + 1) for a in range(PEER_TOPK)]
    n_cand = -(-sum(widths) // 8) * 8
    sub = lax.broadcasted_iota(jnp.int32, (n_cand, 1), 0)
    iota_c = jnp.full((n_cand, 1), PEER_TOPK * PEER_TOPK, jnp.int32)
    row = 0
    for a in range(PEER_TOPK):
        iota_c = jnp.where((sub >= row) & (sub < row + widths[a]), a * PEER_TOPK + (sub - row), iota_c)
        row += widths[a]
    iota_c = jnp.broadcast_to(iota_c, (n_cand, TT))
    exp_rows, gate_rows = [], []
    for hd in range(PEER_HEADS):
        tops = []
        for c in range(2):
            j = hd * 2 + c
            qj = q[:, j * PEER_HALF:(j + 1) * PEER_HALF].astype(jnp.bfloat16)
            st = lax.dot_general(keys_ref[j], qj, (((1,), (1,)), ((), ())), preferred_element_type=f32)
            tops.append(_top16_cols(st, PEER_KEYS, iota_k))
        (s0, i0), (s1, i1) = tops
        fill = n_cand - sum(widths)
        cand_s = jnp.concatenate([s0[a:a + 1, :] + s1[:widths[a], :] for a in range(PEER_TOPK)]
                                 + [jnp.full((fill, TT), -jnp.inf, f32)], axis=0)
        cand_i = jnp.concatenate([i0[a:a + 1, :] * PEER_KEYS + i1[:widths[a], :] for a in range(PEER_TOPK)]
                                 + [jnp.zeros((fill, TT), jnp.int32)], axis=0)
        best_s, best_e = [], []
        s = cand_s
        for _ in range(PEER_TOPK):
            mx = jnp.max(s, axis=0, keepdims=True)
            first = jnp.min(jnp.where(s == mx, iota_c, PEER_TOPK * PEER_TOPK), axis=0, keepdims=True)
            pick = iota_c == first
            best_s.append(mx)
            best_e.append(jnp.sum(jnp.where(pick, cand_i, 0), axis=0, keepdims=True))
            s = jnp.where(pick, -jnp.inf, s)
        bs = jnp.concatenate(best_s, axis=0)
        be = jnp.concatenate(best_e, axis=0)
        ex = jnp.exp(bs - bs[0:1, :])
        gate_rows.append(ex / jnp.sum(ex, axis=0, keepdims=True))
        exp_rows.append(be)
    exp_ref[...] = _row_code(jnp.concatenate(exp_rows, axis=0)).T
    gate_ref[...] = jnp.concatenate(gate_rows, axis=0).T


def _peer_route(h, wq_b, keys_b):
    n = h.shape[0]
    TT = PEER_TT
    return pl.pallas_call(
        _peer_route_kernel,
        grid=(n // TT,),
        in_specs=[pl.BlockSpec((TT, D_MODEL), lambda i: (i, 0)),
                  pl.BlockSpec((D_MODEL, PEER_HEADS * PEER_QDIM), lambda i: (0, 0)),
                  pl.BlockSpec((PEER_HEADS * 2, PEER_KEYS, PEER_HALF), lambda i: (0, 0, 0))],
        out_specs=[pl.BlockSpec((TT, PEER_PICKS), lambda i: (i, 0)),
                   pl.BlockSpec((TT, PEER_PICKS), lambda i: (i, 0))],
        out_shape=[jax.ShapeDtypeStruct((n, PEER_PICKS), jnp.int32),
                   jax.ShapeDtypeStruct((n, PEER_PICKS), jnp.float32)],
        compiler_params=pltpu.CompilerParams(dimension_semantics=("arbitrary",),
                                             vmem_limit_bytes=40 * 1024 * 1024),
        name="peer_route",
    )(h, wq_b, keys_b)


def _pack_rows(w):
    b = lax.bitcast_convert_type(w.astype(jnp.bfloat16), jnp.uint16).astype(jnp.uint32)
    b = b.reshape(N_EXPERTS // 2, 2, D_MODEL)
    return ((b[:, 0] << 16) | b[:, 1]).reshape(N_EXPERTS // 2 * ROW_SUBLANES, 128)


def _row_code(e):
    return (e >> 1) * ROW_SUBLANES + (e & 1)


PARITY_SHIFT_BIT = 4


def _load_row(tab_ref, start, shift_word):
    w = tab_ref[pl.ds(pl.multiple_of(start, ROW_SUBLANES), ROW_SUBLANES), :]
    w = (w << (shift_word & jnp.uint32(1 << PARITY_SHIFT_BIT))) & jnp.uint32(0xFFFF0000)
    return lax.bitcast_convert_type(w, jnp.float32)


def _group_shift_words(codes):
    par = (codes & 1).reshape(-1, PEER_GROUP)
    return jnp.sum(par << (PARITY_SHIFT_BIT + jnp.arange(PEER_GROUP, dtype=jnp.int32)), axis=1, dtype=jnp.int32)


def _splat_u32(s):
    return lax.bitcast_convert_type(jnp.full((ROW_SUBLANES, 128), s, jnp.int32), jnp.uint32)


def _sublane_sums(tiles):
    sub = lax.broadcasted_iota(jnp.int32, (ROW_SUBLANES, 128), 0)
    step = ROW_SUBLANES // 2
    while step >= 1:
        low = (sub % (2 * step)) < step
        tiles = [jnp.where(low, a + pltpu.roll(a, ROW_SUBLANES - step, 0), pltpu.roll(b, step, 0) + b)
                 for a, b in zip(tiles[:len(tiles) // 2], tiles[len(tiles) // 2:])]
        step //= 2
    return tiles[0]


PEER_GROUP = 16


PEER_GROUPS_PER_TOKEN = PEER_PICKS // PEER_GROUP


def _pipelined_groups(n_groups, produce, consume, stage_a, stage_b):
    def put(stage, tiles):
        for k, tile in enumerate(tiles):
            stage[k] = tile

    def get(stage):
        return [stage[k] for k in range(PEER_GROUP)]

    def body(j, carry):
        i = 2 * j + 1
        put(stage_b, produce(i))
        consume(i - 1, get(stage_a))
        put(stage_a, produce(i + 1))
        consume(i, get(stage_b))
        return carry

    put(stage_a, produce(0))
    lax.fori_loop(0, n_groups // 2 - 1, body, 0)
    put(stage_b, produce(n_groups - 1))
    consume(n_groups - 2, get(stage_a))
    consume(n_groups - 1, get(stage_b))


def _tree_sum(tiles):
    while len(tiles) > 1:
        tiles = [a + b for a, b in zip(tiles[0::2], tiles[1::2])]
    return tiles[0]


def _peer_up_kernel(start_ref, shift_ref, h_ref, gate_ref, tab_ref, coef_ref, dots_ref, stage_a, stage_b):
    GT = PEER_GT

    def products(i):
        t = i // PEER_GROUPS_PER_TOKEN
        h = h_ref[pl.ds(pl.multiple_of(t * ROW_SUBLANES, ROW_SUBLANES), ROW_SUBLANES), :]
        shifts = _splat_u32(shift_ref[i])
        return tuple(_load_row(tab_ref, start_ref[i * PEER_GROUP + kk], shifts >> jnp.uint32(kk)) * h
                     for kk in range(PEER_GROUP))

    def reduce_rows(i, prods):
        rows = [_sublane_sums(list(prods[k:k + ROW_SUBLANES])) for k in range(0, PEER_GROUP, ROW_SUBLANES)]
        dots_ref[pl.ds(pl.multiple_of(i * PEER_GROUP, PEER_GROUP), PEER_GROUP), :] = jnp.concatenate(rows, axis=0)

    _pipelined_groups(GT * PEER_GROUPS_PER_TOKEN, products, reduce_rows, stage_a, stage_b)
    rows = [jnp.sum(dots_ref[t * PEER_PICKS:(t + 1) * PEER_PICKS, :].T, axis=0, keepdims=True) for t in range(GT)]
    coef_ref[...] = gate_ref[...] * jax.nn.gelu(jnp.concatenate(rows, axis=0))


def _peer_down_kernel(start_ref, shift_ref, coef_ref, tab_ref, o_ref, part_ref, stage_a, stage_b):
    GT = PEER_GT

    def products(i):
        shifts = _splat_u32(shift_ref[i])
        return tuple(_load_row(tab_ref, start_ref[i * PEER_GROUP + kk], shifts >> jnp.uint32(kk))
                     * jnp.full((ROW_SUBLANES, 128), coef_ref[i * PEER_GROUP + kk], jnp.float32)
                     for kk in range(PEER_GROUP))

    def partial_sum(i, prods):
        part_ref[pl.ds(pl.multiple_of(i * ROW_SUBLANES, ROW_SUBLANES), ROW_SUBLANES), :] = _tree_sum(list(prods))

    _pipelined_groups(GT * PEER_GROUPS_PER_TOKEN, products, partial_sum, stage_a, stage_b)
    parts = part_ref[...].reshape(GT, PEER_GROUPS_PER_TOKEN, ROW_SUBLANES, 128)
    o_ref[...] = jnp.sum(parts, axis=1).reshape(GT * ROW_SUBLANES, 128)


def _peer_gather(h, experts, gates, u_tab, v_tab):
    n = h.shape[0]
    GT = PEER_GT
    tab_rows = N_EXPERTS // 2 * ROW_SUBLANES
    flat_smem = pl.BlockSpec((GT * PEER_PICKS,), lambda i: (i,), memory_space=pltpu.MemorySpace.SMEM)
    tab_spec = pl.BlockSpec((tab_rows, 128), lambda i: (0, 0), pipeline_mode=pl.Buffered(1))
    params = pltpu.CompilerParams(dimension_semantics=("arbitrary",), vmem_limit_bytes=56 * 1024 * 1024)
    group_smem = pl.BlockSpec((GT * PEER_GROUPS_PER_TOKEN,), lambda i: (i,), memory_space=pltpu.MemorySpace.SMEM)
    starts = (experts & ~(ROW_SUBLANES - 1)).reshape(n * PEER_PICKS)
    shifts = _group_shift_words(experts)
    stage = pltpu.VMEM((PEER_GROUP, ROW_SUBLANES, 128), jnp.float32)
    coef = pl.pallas_call(
        _peer_up_kernel,
        grid=(n // GT,),
        in_specs=[flat_smem, group_smem,
                  pl.BlockSpec((GT * ROW_SUBLANES, 128), lambda i: (i, 0)),
                  pl.BlockSpec((GT, PEER_PICKS), lambda i: (i, 0)),
                  tab_spec],
        out_specs=pl.BlockSpec((GT, PEER_PICKS), lambda i: (i, 0)),
        out_shape=jax.ShapeDtypeStruct((n, PEER_PICKS), jnp.float32),
        scratch_shapes=[pltpu.VMEM((GT * PEER_PICKS, 128), jnp.float32), stage, stage],
        compiler_params=params,
        name="peer_up",
    )(starts, shifts, h.reshape(n * ROW_SUBLANES, 128), gates, u_tab)
    out = pl.pallas_call(
        _peer_down_kernel,
        grid=(n // GT,),
        in_specs=[flat_smem, group_smem, flat_smem, tab_spec],
        out_specs=pl.BlockSpec((GT * ROW_SUBLANES, 128), lambda i: (i, 0)),
        out_shape=jax.ShapeDtypeStruct((n * ROW_SUBLANES, 128), jnp.float32),
        scratch_shapes=[pltpu.VMEM((GT * PEER_GROUPS_PER_TOKEN * ROW_SUBLANES, 128), jnp.float32), stage, stage],
        compiler_params=params,
        name="peer_down",
    )(starts, shifts, coef.reshape(n * PEER_PICKS), v_tab)
    return out.reshape(n, D_MODEL)


def _layer_weights(w_in, w_ssm_branch, w_attn_branch, w_out, peer_wq, peer_sub_keys, peer_u, peer_v):
    bf16 = jnp.bfloat16
    offs = np.cumsum((0,) + PROJ_SIZES).tolist()
    seg = lambda k: w_in[:, offs[k]:offs[k + 1]]
    pad = jnp.zeros((D_MODEL, SMALL_WIDTH - SSM_HEADS - 3 * ATTN_HEADS), w_in.dtype)
    return dict(
        in_ssm=[seg(0).astype(bf16), seg(1).astype(bf16)],
        in_attn=[seg(3).astype(bf16), jnp.concatenate([seg(4), seg(5), seg(6)], axis=1).astype(bf16),
                 seg(8).astype(bf16), jnp.concatenate([seg(2), seg(7), pad], axis=1).astype(bf16)],
        w_ssm=w_ssm_branch.astype(bf16), w_attn=w_attn_branch.astype(bf16), w_out=w_out.astype(bf16),
        wq=peer_wq.astype(bf16),
        keys=peer_sub_keys.reshape(PEER_HEADS * 2, PEER_KEYS, PEER_HALF).astype(bf16),
        u_tab=_pack_rows(peer_u), v_tab=_pack_rows(peer_v))


def _block(x, conv_prev, ssm_prev, nsa_core, lw, norm_mix_w, conv_w, conv_b, dt_bias, a_log, d_skip, ssm_norm_w,
           norm_ffn_w):
    b, T, _ = x.shape
    n = b * T
    xf = x.reshape(n, D_MODEL)
    z, xbc = _norm_proj(xf, norm_mix_w, lw["in_ssm"], "in_proj_ssm")
    q, kv, br_g, small = _norm_proj(xf, norm_mix_w, lw["in_attn"], "in_proj_attn")
    nsa_g = small[:, SSM_HEADS:SSM_HEADS + 3 * ATTN_HEADS]
    kv_c, kv_s, kv_w = (kv[:, k * KV_WIDTH:(k + 1) * KV_WIDTH].reshape(b, T, KV_WIDTH) for k in range(3))
    y_ssm, conv_new, ssm_new = _mamba_branch(z.reshape(b, T, D_INNER), xbc.reshape(b, T, CONV_DIM),
                                             small.reshape(b, T, SMALL_WIDTH), conv_prev, ssm_prev, conv_w, conv_b,
                                             dt_bias, a_log, d_skip, ssm_norm_w)
    o_attn, win_new = nsa_core(q.reshape(b, T, ATTN_WIDTH), kv_c, kv_s, kv_w, nsa_g.reshape(b, T, 3 * ATTN_HEADS))
    x2, h2 = _merge_proj(xf, o_attn.reshape(n, ATTN_WIDTH), y_ssm.reshape(n, D_INNER), br_g, norm_ffn_w,
                         lw["w_attn"], lw["w_ssm"], lw["w_out"])
    experts, gates = _peer_route(h2, lw["wq"], lw["keys"])
    peer = _peer_gather(h2, experts, gates, lw["u_tab"], lw["v_tab"])
    kv_shape = (b, T, KV_HEADS, 2, ATTN_HEAD_DIM)
    return x2, peer, kv_c.reshape(kv_shape), kv_s.reshape(kv_shape), win_new, conv_new, ssm_new


def kernel(x_prompt, x_sample, cache_cmp_kv, cache_sel_kv, cache_win_kv, state_ssm, state_conv, page_table,
           norm_mix_w, w_in, conv_w, conv_b, dt_bias, a_log, d_skip, ssm_norm_w, w_ssm_branch, w_attn_branch,
           w_out, norm_ffn_w, peer_wq, peer_sub_keys, peer_u, peer_v, final_norm_w):
    xp, xs = x_prompt, x_sample
    cmp_p, cmp_s, sel_p, sel_s, win_p, win_s = [], [], [], [], [], []
    ssm_p, ssm_s, conv_p, conv_s = [], [], [], []
    for layer in range(DEPTH):
        lw = _layer_weights(*(a[layer] for a in (w_in, w_ssm_branch, w_attn_branch, w_out, peer_wq, peer_sub_keys,
                                                   peer_u, peer_v)))
        rest = [a[layer] for a in (norm_mix_w, conv_w, conv_b, dt_bias, a_log, d_skip, ssm_norm_w, norm_ffn_w)]
        last = layer == DEPTH - 1
        bp = xp.shape[0]
        conv0 = jnp.zeros((bp, CONV_WIDTH - 1, CONV_DIM), xp.dtype)
        ssm0 = jnp.zeros((bp, SSM_HEADS, SSM_HEAD_DIM, SSM_STATE), xp.dtype)
        x2, peer, kc, ks, kw, cv, ss = _block(xp, conv0, ssm0, _nsa_prompt, lw, *rest)
        xp = (_add_norm(x2, peer, final_norm_w) if last else x2 + peer).reshape(xp.shape)
        cmp_p.append(kc); sel_p.append(ks); win_p.append(kw); conv_p.append(cv); ssm_p.append(ss)
        core = functools.partial(_nsa_sample, cache_cmp_kv[layer], cache_sel_kv[layer], cache_win_kv[layer], page_table)
        x2, peer, kc, ks, kw, cv, ss = _block(xs, state_conv[layer], state_ssm[layer], core, lw, *rest)
        xs = (_add_norm(x2, peer, final_norm_w) if last else x2 + peer).reshape(xs.shape)
        cmp_s.append(kc); sel_s.append(ks); win_s.append(kw); conv_s.append(cv); ssm_s.append(ss)
    return (xp, xs, jnp.stack(cmp_p), jnp.stack(cmp_s), jnp.stack(sel_p), jnp.stack(sel_s),
            jnp.stack(win_p), jnp.stack(win_s), jnp.stack(ssm_p), jnp.stack(ssm_s), jnp.stack(conv_p), jnp.stack(conv_s))
```

```python
import math, functools
import jax, jax.numpy as jnp
from jax import lax
import numpy as np
from jax.experimental import pallas as pl
from jax.experimental.pallas import tpu as pltpu

D_MODEL = 1024
BATCH = 8
SEQ = 4096
DEPTH = 1
DEC_BATCH = 32
DEC_SEQ = 8
PAST_LEN = 16384
PAGE_SIZE = 128

SSM_EXPAND = 2
D_INNER = SSM_EXPAND * D_MODEL
SSM_HEAD_DIM = 64
SSM_HEADS = D_INNER // SSM_HEAD_DIM
SSM_GROUPS = 4
SSM_HPG = SSM_HEADS // SSM_GROUPS
SSM_STATE = 128
CONV_WIDTH = 4
CONV_DIM = D_INNER + 2 * SSM_GROUPS * SSM_STATE
SSD_CHUNK = 128
ATTN_HEADS = 16
ATTN_HEAD_DIM = 64
ATTN_WIDTH = ATTN_HEADS * ATTN_HEAD_DIM
KV_HEADS = 2
HEADS_PER_KV = ATTN_HEADS // KV_HEADS
KV_WIDTH = 2 * KV_HEADS * ATTN_HEAD_DIM
CMP_BLOCK = 64
SEL_BLOCK = CMP_BLOCK
SEL_TOPK = 16
SEL_BONUS = 1000.0
WINDOW = 512
WIN_QBLOCK = 128
SEL_QBLOCK = 32
ATTN_SCALE = ATTN_HEAD_DIM ** -0.5
PEER_HEADS = 8
PEER_KEYS = 128
N_EXPERTS = PEER_KEYS * PEER_KEYS
PEER_QDIM = 256
PEER_HALF = PEER_QDIM // 2
PEER_TOPK = 16
PEER_TOKEN_CHUNK = 128
EPS = 1e-6
PROJ_SIZES = (D_INNER, CONV_DIM, SSM_HEADS, ATTN_WIDTH, KV_WIDTH, KV_WIDTH, KV_WIDTH, 3 * ATTN_HEADS, 2 * D_MODEL)
PROJ_DIM = sum(PROJ_SIZES)


PROJ_TM = 512
PROJ_VMEM_BYTES = 56 * 1024 * 1024


def _rms(x, w):
    return x * lax.rsqrt(jnp.mean(x * x, axis=-1, keepdims=True) + EPS) * w


def _resident(shape):
    return pl.BlockSpec(shape, lambda i: (0,) * len(shape), pipeline_mode=pl.Buffered(1))


def _norm_proj_kernel(x_ref, nw_ref, *refs):
    n_out = len(refs) // 2
    h = _rms(x_ref[...], nw_ref[...]).astype(jnp.bfloat16)
    for w_ref, o_ref in zip(refs[:n_out], refs[n_out:]):
        o_ref[...] = jnp.dot(h, w_ref[...], preferred_element_type=jnp.float32)


def _norm_proj(x, norm_w, weights, name):
    n = x.shape[0]
    tm = min(PROJ_TM, n)
    return pl.pallas_call(
        _norm_proj_kernel,
        grid=(n // tm,),
        in_specs=[pl.BlockSpec((tm, D_MODEL), lambda i: (i, 0)), _resident((1, D_MODEL))]
                 + [_resident(w.shape) for w in weights],
        out_specs=[pl.BlockSpec((tm, w.shape[1]), lambda i: (i, 0)) for w in weights],
        out_shape=[jax.ShapeDtypeStruct((n, w.shape[1]), jnp.float32) for w in weights],
        compiler_params=pltpu.CompilerParams(dimension_semantics=("arbitrary",), vmem_limit_bytes=PROJ_VMEM_BYTES),
        name=name,
    )(x, norm_w.reshape(1, D_MODEL), *weights)


def _merge_proj_kernel(x_ref, oa_ref, ys_ref, g_ref, nw_ref, wa_ref, ws_ref, wo_ref, x2_ref, h2_ref):
    f32, bf16 = jnp.float32, jnp.bfloat16
    y_attn = jnp.dot(oa_ref[...].astype(bf16), wa_ref[...], preferred_element_type=f32)
    y_ssm = jnp.dot(ys_ref[...].astype(bf16), ws_ref[...], preferred_element_type=f32)
    g = jax.nn.sigmoid(g_ref[...])
    merged = g[:, :D_MODEL] * y_ssm + g[:, D_MODEL:] * y_attn
    x2 = x_ref[...] + jnp.dot(merged.astype(bf16), wo_ref[...], preferred_element_type=f32)
    x2_ref[...] = x2
    h2_ref[...] = _rms(x2, nw_ref[...])


def _merge_proj(x, o_attn, y_ssm, br_g, norm_w, wa_b, ws_b, wo_b):
    n = x.shape[0]
    tm = min(PROJ_TM, n)
    rows = lambda c: pl.BlockSpec((tm, c), lambda i: (i, 0))
    return pl.pallas_call(
        _merge_proj_kernel,
        grid=(n // tm,),
        in_specs=[rows(D_MODEL), rows(ATTN_WIDTH), rows(D_INNER), rows(2 * D_MODEL), _resident((1, D_MODEL)),
                  _resident(wa_b.shape), _resident(ws_b.shape), _resident(wo_b.shape)],
        out_specs=[rows(D_MODEL), rows(D_MODEL)],
        out_shape=[jax.ShapeDtypeStruct((n, D_MODEL), jnp.float32)] * 2,
        compiler_params=pltpu.CompilerParams(dimension_semantics=("arbitrary",), vmem_limit_bytes=PROJ_VMEM_BYTES),
        name="merge_proj",
    )(x, o_attn, y_ssm, br_g, norm_w.reshape(1, D_MODEL), wa_b, ws_b, wo_b)


def _add_norm_kernel(x_ref, y_ref, w_ref, o_ref):
    o_ref[...] = _rms(x_ref[...] + y_ref[...], w_ref[...])


def _add_norm(x, y, w):
    n = x.shape[0]
    tm = min(PROJ_TM, n)
    rows = pl.BlockSpec((tm, D_MODEL), lambda i: (i, 0))
    return pl.pallas_call(
        _add_norm_kernel,
        grid=(n // tm,),
        in_specs=[rows, rows, _resident((1, D_MODEL))],
        out_specs=rows,
        out_shape=jax.ShapeDtypeStruct((n, D_MODEL), jnp.float32),
        name="add_norm",
    )(x, y, w.reshape(1, D_MODEL))


SMALL_WIDTH = 128
CONV_TAIL = 8


def _ssd_kernel(z_ref, xbc_ref, sm_ref, cprev_ref, h0_ref, cw_ref, cbias_ref, dtb_ref, alog_ref, dskip_ref, nw_ref,
                y_ref, hout_ref, state_ref, tail_ref, *, valid):
    f32, bf16 = jnp.float32, jnp.bfloat16
    L, P, N = SSD_CHUNK, SSM_HEAD_DIM, SSM_STATE
    c = pl.program_id(1)

    @pl.when(c == 0)
    def _init():
        state_ref[...] = h0_ref[0]
        tail_ref[...] = cprev_ref[0]

    xbc = xbc_ref[0]
    padded = jnp.concatenate([tail_ref[...], xbc], axis=0)
    acc = jnp.broadcast_to(cbias_ref[...], (L, CONV_DIM))
    for w in range(CONV_WIDTH):
        lo = CONV_TAIL - (CONV_WIDTH - 1) + w
        acc = acc + cw_ref[w:w + 1, :] * padded[lo:lo + L, :]
    xc = acc * jax.nn.sigmoid(acc)
    tail_ref[...] = xbc[L - CONV_TAIL:, :]

    row = lax.broadcasted_iota(jnp.int32, (L, 1), 0)
    live = row < (valid - c * L)
    dt = jnp.where(live, jax.nn.softplus(sm_ref[0] + dtb_ref[...]), 0.0)
    a = dt * (-jnp.exp(alog_ref[...]))
    tril = (row >= lax.broadcasted_iota(jnp.int32, (L, L), 1))
    a_cum = jnp.dot(tril.astype(f32), a, preferred_element_type=f32, precision=lax.Precision.HIGHEST)
    a_cum_t = a_cum.T
    a_last = a_cum[L - 1:L, :]
    grow = jnp.exp(a_cum)
    to_end = jnp.exp(a_last - a_cum)
    carry = jnp.exp(a_last)

    W = SSM_HPG * P

    def per_head(v, g):
        cols = [jnp.broadcast_to(v[:, g * SSM_HPG + j:g * SSM_HPG + j + 1], (v.shape[0], P)) for j in range(SSM_HPG)]
        return jnp.concatenate(cols, axis=1)

    xs = xc[:, :D_INNER]
    ys = []
    for g in range(SSM_GROUPS):
        bg = xc[:, D_INNER + g * N:D_INNER + (g + 1) * N]
        cg = xc[:, D_INNER + SSM_GROUPS * N + g * N:D_INNER + SSM_GROUPS * N + (g + 1) * N]
        bg_b, cg_b = bg.astype(bf16), cg.astype(bf16)
        bg_t = bg.T.astype(bf16)
        cb = lax.dot_general(cg_b, bg_b, (((1,), (1,)), ((), ())), preferred_element_type=f32)
        xs_g = xs[:, g * W:(g + 1) * W]
        xdt = xs_g * per_head(dt, g)
        xdt_b = xdt.astype(bf16)
        st = state_ref[g]
        y = jnp.dot(cg_b, st.astype(bf16), preferred_element_type=f32) * per_head(grow, g)
        state_ref[g] = st * per_head(carry, g) + jnp.dot(bg_t, (xdt * per_head(to_end, g)).astype(bf16),
                                                         preferred_element_type=f32)
        diag = []
        for j in range(SSM_HPG):
            h = g * SSM_HPG + j
            diff = a_cum[:, h:h + 1] - a_cum_t[h:h + 1, :]
            m = (cb * jnp.exp(jnp.where(tril, diff, -jnp.inf))).astype(bf16)
            diag.append(jnp.dot(m, xdt_b[:, j * P:(j + 1) * P], preferred_element_type=f32))
        ys.append(jnp.concatenate(diag, axis=1) + y + per_head(dskip_ref[...], g) * xs_g)
    z = z_ref[0]
    y_ref[0] = _rms(jnp.concatenate(ys, axis=1) * (z * jax.nn.sigmoid(z)), nw_ref[...])

    @pl.when(c == pl.num_programs(1) - 1)
    def _emit():
        hout_ref[0] = state_ref[...]


def _mamba_branch(z, xbc, small, conv_prev, ssm_prev, conv_w, conv_b, dt_bias, a_log, d_skip, ssm_norm_w):
    b, L, _ = xbc.shape
    Lc = SSD_CHUNK
    Lp = -(-L // Lc) * Lc
    keep = CONV_WIDTH - 1
    conv_new = xbc[:, L - keep:] if L >= keep else jnp.concatenate([conv_prev, xbc], axis=1)[:, -keep:]
    if Lp != L:
        padrows = lambda a: jnp.pad(a, ((0, 0), (0, Lp - L), (0, 0)))
        z, xbc, small = padrows(z), padrows(xbc), padrows(small)
    cprev = jnp.pad(conv_prev, ((0, 0), (CONV_TAIL - (CONV_WIDTH - 1), 0), (0, 0)))
    G, J, P, N = SSM_GROUPS, SSM_HPG, SSM_HEAD_DIM, SSM_STATE
    state_block = (1, G, N, J * P)
    h0 = jnp.transpose(ssm_prev.reshape(b, G, J, P, N), (0, 1, 4, 2, 3)).reshape(b, G, N, J * P)
    lane = lambda v: jnp.pad(v.reshape(1, SSM_HEADS), ((0, 0), (0, SMALL_WIDTH - SSM_HEADS)))
    full = lambda shape: pl.BlockSpec(shape, lambda bi, c: (0,) * len(shape))
    kern = functools.partial(_ssd_kernel, valid=L)
    y, hout = pl.pallas_call(
        kern,
        grid=(b, Lp // Lc),
        in_specs=[pl.BlockSpec((1, Lc, D_INNER), lambda bi, c: (bi, c, 0)),
                  pl.BlockSpec((1, Lc, CONV_DIM), lambda bi, c: (bi, c, 0)),
                  pl.BlockSpec((1, Lc, SMALL_WIDTH), lambda bi, c: (bi, c, 0)),
                  pl.BlockSpec((1, CONV_TAIL, CONV_DIM), lambda bi, c: (bi, 0, 0)),
                  pl.BlockSpec(state_block, lambda bi, c: (bi, 0, 0, 0)),
                  full((CONV_WIDTH, CONV_DIM)), full((1, CONV_DIM)), full((1, SMALL_WIDTH)), full((1, SMALL_WIDTH)),
                  full((1, SMALL_WIDTH)), full((1, D_INNER))],
        out_specs=[pl.BlockSpec((1, Lc, D_INNER), lambda bi, c: (bi, c, 0)),
                   pl.BlockSpec(state_block, lambda bi, c: (bi, 0, 0, 0))],
        out_shape=[jax.ShapeDtypeStruct((b, Lp, D_INNER), jnp.float32),
                   jax.ShapeDtypeStruct((b,) + state_block[1:], jnp.float32)],
        scratch_shapes=[pltpu.VMEM(state_block[1:], jnp.float32),
                        pltpu.VMEM((CONV_TAIL, CONV_DIM), jnp.float32)],
        compiler_params=pltpu.CompilerParams(dimension_semantics=("arbitrary", "arbitrary"),
                                             vmem_limit_bytes=48 * 1024 * 1024),
        name="mamba_ssd",
    )(z, xbc, small, cprev, h0, conv_w, conv_b.reshape(1, CONV_DIM), lane(dt_bias), lane(a_log), lane(d_skip),
      ssm_norm_w.reshape(1, D_INNER))
    ssm_new = jnp.transpose(hout.reshape(b, G, N, J, P), (0, 1, 3, 4, 2)).reshape(b, SSM_HEADS, P, N)
    return y[:, :L], conv_new, ssm_new


NSA_TQ = 128
NSA_TK = 512
NSA_WSPAN = WINDOW + NSA_TQ
NEG_BIG = -1e30


def _nsa_prompt_kernel(q_ref, kvc_ref, ks_ref, vst_ref, kw_ref, vwt_ref, gate_ref, o_ref,
                       pk_ref, pvt_ref, m_ref, l_ref, acc_ref, *, seq):
    f32, bf16 = jnp.float32, jnp.bfloat16
    D, J, TQ, TK = ATTN_HEAD_DIM, HEADS_PER_KV, NSA_TQ, NSA_TK
    n_blk = seq // CMP_BLOCK
    i = pl.program_id(2)
    t0 = i * TQ

    @pl.when(i == 0)
    def _pool():
        c = kvc_ref[0]
        pooled = jnp.sum(c.reshape(n_blk, CMP_BLOCK, 2 * D), axis=1) * (1.0 / CMP_BLOCK)
        pk_ref[...] = pooled[:, :D].astype(bf16)
        pvt_ref[...] = pooled[:, D:].T.astype(bf16)

    qt = (q_ref[0] * ATTN_SCALE).T
    qs = jnp.concatenate([qt[j * D:(j + 1) * D, :] for j in range(J)], axis=1).astype(bf16)
    qpos1 = t0 + lax.broadcasted_iota(jnp.int32, (1, TQ), 1)
    qpos = jnp.concatenate([qpos1] * J, axis=1)

    s = jnp.dot(pk_ref[...], qs, preferred_element_type=f32)
    blk = lax.broadcasted_iota(jnp.int32, (n_blk, 1), 0)
    vis = (blk * CMP_BLOCK + (CMP_BLOCK - 1)) <= qpos
    s = jnp.where(vis, s, -jnp.inf)
    m = jnp.max(s, axis=0, keepdims=True)
    m = jnp.where(m == -jnp.inf, 0.0, m)
    p = jnp.exp(s - m)
    p = p / jnp.maximum(jnp.sum(p, axis=0, keepdims=True), 1e-30)
    o_cmp = jnp.dot(pvt_ref[...], p.astype(bf16), preferred_element_type=f32)
    imp = p[:, 0:TQ]
    for j in range(1, J):
        imp = imp + p[:, j * TQ:(j + 1) * TQ]

    cur = qpos1 // CMP_BLOCK
    blk2 = lax.broadcasted_iota(jnp.int32, (n_blk, TQ), 0)
    forced = ((blk2 == 0) | (blk2 == cur) | (blk2 == cur - 1)).astype(f32)
    score = jnp.where(blk2 <= cur, imp + SEL_BONUS * forced, -jnp.inf)
    sel = jnp.zeros((n_blk, TQ), f32)
    for _ in range(min(SEL_TOPK, n_blk)):
        mx = jnp.max(score, axis=0, keepdims=True)
        first = jnp.min(jnp.where(score == mx, blk2, n_blk), axis=0, keepdims=True)
        pick = blk2 == first
        sel = jnp.where(pick, 1.0, sel)
        score = jnp.where(pick, -jnp.inf, score)
    sel_bias = ((sel - 1.0) * -NEG_BIG).astype(bf16)
    qs_sel = jnp.concatenate([qs, jnp.concatenate([sel_bias] * J, axis=1)], axis=0)

    m_ref[...] = jnp.full(m_ref.shape, NEG_BIG, f32)
    l_ref[...] = jnp.zeros(l_ref.shape, f32)
    acc_ref[...] = jnp.zeros(acc_ref.shape, f32)
    n_kt = (t0 + TQ + TK - 1) // TK

    def sel_tile(kt, causal):
        k0 = pl.multiple_of(kt * TK, TK)
        st = jnp.dot(ks_ref[0, 0, pl.ds(k0, TK), :], qs_sel, preferred_element_type=f32)
        if causal:
            kpos = k0 + lax.broadcasted_iota(jnp.int32, (TK, 1), 0)
            st = st + jnp.concatenate([jnp.where(kpos <= qpos1, 0.0, NEG_BIG)] * J, axis=1)
        m_old = m_ref[...]
        m_new = jnp.maximum(m_old, jnp.max(st, axis=0, keepdims=True))
        alpha = jnp.exp(m_old - m_new)
        pt = jnp.exp(st - m_new)
        l_ref[...] = alpha * l_ref[...] + jnp.sum(pt, axis=0, keepdims=True)
        acc_ref[...] = alpha * acc_ref[...] + jnp.dot(vst_ref[0, 0, :, pl.ds(k0, TK)], pt.astype(bf16),
                                                      preferred_element_type=f32)
        m_ref[...] = m_new

    def past_tile(kt, carry):
        sel_tile(kt, causal=False)
        return carry

    lax.fori_loop(0, n_kt - 1, past_tile, 0)
    sel_tile(n_kt - 1, causal=True)
    o_sel = acc_ref[...] / l_ref[...]

    w0 = pl.multiple_of(jnp.maximum(t0 - WINDOW, 0), TQ)
    span = min(NSA_WSPAN, seq)
    kposw = w0 + lax.broadcasted_iota(jnp.int32, (span, 1), 0)
    rel = qpos1 - kposw
    biasw1 = jnp.where((rel >= 0) & (rel <= WINDOW), 0.0, NEG_BIG)
    biasw = jnp.concatenate([biasw1] * J, axis=1)
    sw = jnp.dot(kw_ref[0, 0, pl.ds(w0, span), :], qs, preferred_element_type=f32) + biasw
    pw = jnp.exp(sw - jnp.max(sw, axis=0, keepdims=True))
    lw = jnp.sum(pw, axis=0, keepdims=True)
    o_win = jnp.dot(vwt_ref[0, 0, :, pl.ds(w0, span)], pw.astype(bf16), preferred_element_type=f32) / lw

    def gate_row(br):
        g = jax.nn.sigmoid(gate_ref[0, 0, br])
        return jnp.concatenate([g[j:j + 1, :] for j in range(J)], axis=1)

    out = gate_row(0) * o_cmp + gate_row(1) * o_sel + gate_row(2) * o_win
    out = jnp.concatenate([out[:, j * TQ:(j + 1) * TQ] for j in range(J)], axis=0)
    o_ref[0] = out.T


def _nsa_prompt_attention(q, kv_c, kv_s, kv_w, nsa_g):
    b, T, _ = q.shape
    G, J, D, TQ = KV_HEADS, HEADS_PER_KV, ATTN_HEAD_DIM, NSA_TQ
    bf16 = jnp.bfloat16

    def split_kv(kv):
        r = kv.reshape(b, T, G, 2, D).astype(bf16)
        return jnp.transpose(r[:, :, :, 0], (0, 2, 1, 3)), jnp.transpose(r[:, :, :, 1], (0, 2, 3, 1))

    ks, vst = split_kv(kv_s)
    kw, vwt = split_kv(kv_w)
    gates_t = jnp.transpose(nsa_g.reshape(b, T, G, J, 3), (0, 2, 4, 3, 1))
    n_blk = T // CMP_BLOCK
    member = (jnp.arange(T)[:, None] // SEL_BLOCK == jnp.arange(n_blk)[None, :]).astype(bf16)
    ks = jnp.concatenate([ks, jnp.broadcast_to(member, (b, G, T, n_blk))], axis=-1)
    kern = functools.partial(_nsa_prompt_kernel, seq=T)
    return pl.pallas_call(
        kern,
        grid=(b, G, T // TQ),
        in_specs=[
            pl.BlockSpec((1, TQ, J * D), lambda bi, g, i: (bi, i, g)),
            pl.BlockSpec((1, T, 2 * D), lambda bi, g, i: (bi, 0, g)),
            pl.BlockSpec((1, 1, T, D + n_blk), lambda bi, g, i: (bi, g, 0, 0)),
            pl.BlockSpec((1, 1, D, T), lambda bi, g, i: (bi, g, 0, 0)),
            pl.BlockSpec((1, 1, T, D), lambda bi, g, i: (bi, g, 0, 0)),
            pl.BlockSpec((1, 1, D, T), lambda bi, g, i: (bi, g, 0, 0)),
            pl.BlockSpec((1, 1, 3, J, TQ), lambda bi, g, i: (bi, g, 0, 0, i)),
        ],
        out_specs=pl.BlockSpec((1, TQ, J * D), lambda bi, g, i: (bi, i, g)),
        out_shape=jax.ShapeDtypeStruct((b, T, G * J * D), jnp.float32),
        scratch_shapes=[
            pltpu.VMEM((n_blk, D), bf16),
            pltpu.VMEM((D, n_blk), bf16),
            pltpu.VMEM((1, J * TQ), jnp.float32),
            pltpu.VMEM((1, J * TQ), jnp.float32),
            pltpu.VMEM((D, J * TQ), jnp.float32),
        ],
        compiler_params=pltpu.CompilerParams(
            dimension_semantics=("arbitrary", "arbitrary", "arbitrary"),
            vmem_limit_bytes=48 * 1024 * 1024),
        name="nsa_prompt",
    )(q, kv_c, ks, vst, kw, vwt, gates_t)


def _nsa_prompt(q, kv_c, kv_s, kv_w, nsa_g):
    b, T = q.shape[:2]
    win_new = kv_w.reshape(b, T, KV_HEADS, 2, ATTN_HEAD_DIM)[:, -min(WINDOW, T):]
    return _nsa_prompt_attention(q, kv_c, kv_s, kv_w, nsa_g), win_new


NSAS_PPS = 16


def _nsa_sample_kernel(pt_ref, qs_ref, gate_ref, *refs, n_pages, t_len, w_eff):
    f32, bf16 = jnp.float32, jnp.bfloat16
    PPS, G, D, J, T = NSAS_PPS, KV_HEADS, ATTN_HEAD_DIM, HEADS_PER_KV, t_len
    cmp_refs, sel_refs = refs[:PPS], refs[PPS:2 * PPS]
    tail_ref, win_ref, o_ref, pooled_ref, ocmp_ref, sel_ref, m_ref, l_ref, acc_ref = refs[2 * PPS:]
    C = J * T
    ph, pg = pl.program_id(1), pl.program_id(2)
    n_pg = n_pages // PPS
    past = n_pages * PAGE_SIZE
    n_blk = past // CMP_BLOCK
    n_sel_pad = sel_ref.shape[1]
    blk_per_page = PAGE_SIZE // CMP_BLOCK
    col_t = lax.broadcasted_iota(jnp.int32, (1, C), 1) % T
    qpos = past + col_t

    def queries(g):
        return (qs_ref[0, g] * ATTN_SCALE).astype(bf16)

    def flash(g, kv, ok):
        k, v = kv[:, :D], kv[:, D:]
        st = jnp.where(ok, jnp.dot(k.astype(bf16), queries(g), preferred_element_type=f32), NEG_BIG)
        m_old = m_ref[g]
        m_new = jnp.maximum(m_old, jnp.max(st, axis=0, keepdims=True))
        alpha = jnp.exp(m_old - m_new)
        p = jnp.exp(st - m_new)
        l_ref[g] = alpha * l_ref[g] + jnp.sum(p, axis=0, keepdims=True)
        acc_ref[g] = alpha * acc_ref[g] + jnp.dot(v.T.astype(bf16), p.astype(bf16), preferred_element_type=f32)
        m_ref[g] = m_new

    @pl.when(ph == 0)
    def _compressed():
        means = []
        for r in range(PPS):
            page = cmp_refs[r][0]
            means.append(jnp.sum(page.reshape(blk_per_page, CMP_BLOCK, G * 2 * D), axis=1) * (1.0 / CMP_BLOCK))
        rows = PPS * blk_per_page
        pooled_ref[pl.ds(pl.multiple_of(pg * rows, rows), rows), :] = jnp.concatenate(means, axis=0)

        @pl.when(pg == n_pg - 1)
        def _select():
            blk = lax.broadcasted_iota(jnp.int32, (n_blk, 1), 0)
            vis = (blk * CMP_BLOCK + (CMP_BLOCK - 1)) <= qpos
            fold = (lax.broadcasted_iota(jnp.int32, (C, T), 0) % T == lax.broadcasted_iota(jnp.int32, (C, T), 1))
            blk2 = lax.broadcasted_iota(jnp.int32, (n_sel_pad, T), 0)
            cur = (past + lax.broadcasted_iota(jnp.int32, (1, T), 1)) // SEL_BLOCK
            forced = ((blk2 == 0) | (blk2 == cur) | (blk2 == cur - 1)).astype(f32)
            for g in range(G):
                pooled = pooled_ref[:, g * 2 * D:(g + 1) * 2 * D]
                s = jnp.dot(pooled[:, :D].astype(bf16), queries(g), preferred_element_type=f32)
                s = jnp.where(vis, s, -jnp.inf)
                m = jnp.max(s, axis=0, keepdims=True)
                m = jnp.where(m == -jnp.inf, 0.0, m)
                p = jnp.exp(s - m)
                p = p / jnp.maximum(jnp.sum(p, axis=0, keepdims=True), 1e-30)
                ocmp_ref[g] = jnp.dot(pooled[:, D:].T.astype(bf16), p.astype(bf16), preferred_element_type=f32)
                imp = jnp.dot(p, fold.astype(f32), preferred_element_type=f32, precision=lax.Precision.HIGHEST)
                imp = jnp.concatenate([imp, jnp.zeros((n_sel_pad - n_blk, T), f32)], axis=0)
                score = jnp.where(blk2 <= cur, imp + SEL_BONUS * forced, -jnp.inf)
                sel = jnp.zeros((n_sel_pad, T), f32)
                for _ in range(SEL_TOPK):
                    mx = jnp.max(score, axis=0, keepdims=True)
                    first = jnp.min(jnp.where(score == mx, blk2, n_sel_pad), axis=0, keepdims=True)
                    pick = blk2 == first
                    sel = jnp.where(pick, 1.0, sel)
                    score = jnp.where(pick, -jnp.inf, score)
                sel_ref[g] = jnp.concatenate([sel] * J, axis=1)

    @pl.when(ph == 1)
    def _selected():
        @pl.when(pg == 0)
        def _init():
            m_ref[...] = jnp.full(m_ref.shape, NEG_BIG, f32)
            l_ref[...] = jnp.zeros(l_ref.shape, f32)
            acc_ref[...] = jnp.zeros(acc_ref.shape, f32)

        key = lax.broadcasted_iota(jnp.int32, (PAGE_SIZE, 1), 0)
        for r in range(PPS):
            page = sel_refs[r][0]
            first_blk = (pg * PPS + r) * blk_per_page
            for g in range(G):
                chosen = jnp.where(key < CMP_BLOCK, sel_ref[g, pl.ds(first_blk, 1), :], sel_ref[g, pl.ds(first_blk + 1, 1), :])
                flash(g, page[:, g * 2 * D:(g + 1) * 2 * D], chosen > 0.5)

        @pl.when(pg == n_pg - 1)
        def _finish():
            tail = tail_ref[0]
            wkeys = win_ref[0]
            span = wkeys.shape[0]
            wi = lax.broadcasted_iota(jnp.int32, (span, 1), 0)
            rel = qpos - (past - w_eff + wi)
            wok = (rel >= 0) & (rel <= WINDOW) & (wi < w_eff + T)
            for g in range(G):
                ok = (sel_ref[g, n_blk:n_blk + 1, :] > 0.5) & (key <= col_t) & (key < T)
                flash(g, tail[:, g * 2 * D:(g + 1) * 2 * D], ok)
                o_sel = acc_ref[g] / l_ref[g]
                kvw = wkeys[:, g * 2 * D:(g + 1) * 2 * D]
                sw = jnp.where(wok, jnp.dot(kvw[:, :D].astype(bf16), queries(g), preferred_element_type=f32), NEG_BIG)
                pw = jnp.exp(sw - jnp.max(sw, axis=0, keepdims=True))
                o_win = jnp.dot(kvw[:, D:].T.astype(bf16), pw.astype(bf16), preferred_element_type=f32)
                o_win = o_win / jnp.sum(pw, axis=0, keepdims=True)
                gates = jax.nn.sigmoid(gate_ref[0, g])
                o_ref[0, g] = gates[0:1] * ocmp_ref[g] + gates[1:2] * o_sel + gates[2:3] * o_win


def _nsa_sample(pool_cmp, pool_sel, win_buf, page_table, q, kv_cmp, kv_sel, kv_win, nsa_g):
    b, T, _ = q.shape
    assert T < CMP_BLOCK and page_table.shape[1] % NSAS_PPS == 0
    G, J, D, PPS = KV_HEADS, HEADS_PER_KV, ATTN_HEAD_DIM, NSAS_PPS
    n_pages = page_table.shape[1]
    n_pg = n_pages // PPS
    past = n_pages * PAGE_SIZE
    n_blk = past // CMP_BLOCK
    n_sel_pad = -(-(n_blk + 1) // 8) * 8
    w_eff = win_buf.shape[1]
    C = J * T
    n_pool = pool_cmp.shape[0]
    pool_cmp = pool_cmp.reshape(n_pool, PAGE_SIZE, KV_WIDTH)
    pool_sel = pool_sel.reshape(n_pool, PAGE_SIZE, KV_WIDTH)
    qs = jnp.transpose(q.reshape(b, T, G, J, D), (0, 2, 4, 3, 1)).reshape(b, G, D, C)
    gates = jnp.transpose(nsa_g.reshape(b, T, G, J, 3), (0, 2, 4, 3, 1)).reshape(b, G, 3, C)
    tail = jnp.pad(kv_sel, ((0, 0), (0, PAGE_SIZE - T), (0, 0)))
    keys = jnp.concatenate([win_buf.reshape(b, w_eff, KV_WIDTH), kv_win], axis=1)
    span = -(-(w_eff + T) // PAGE_SIZE) * PAGE_SIZE
    wkeys = jnp.pad(keys, ((0, 0), (0, span - (w_eff + T)), (0, 0)))

    def cmp_map(r):
        return lambda bi, ph, pg, pt: (pt[bi, jnp.where(ph == 0, pg, n_pg - 1) * PPS + r], 0, 0)

    def sel_map(r):
        return lambda bi, ph, pg, pt: (pt[bi, jnp.where(ph == 1, pg, 0) * PPS + r], 0, 0)

    page_block = (1, PAGE_SIZE, KV_WIDTH)
    per_b = lambda shape: pl.BlockSpec(shape, lambda bi, ph, pg, pt: (bi,) + (0,) * (len(shape) - 1))
    kern = functools.partial(_nsa_sample_kernel, n_pages=n_pages, t_len=T, w_eff=w_eff)
    out = pl.pallas_call(
        kern,
        grid_spec=pltpu.PrefetchScalarGridSpec(
            num_scalar_prefetch=1,
            grid=(b, 2, n_pg),
            in_specs=[per_b((1, G, D, C)), per_b((1, G, 3, C))]
                     + [pl.BlockSpec(page_block, cmp_map(r)) for r in range(PPS)]
                     + [pl.BlockSpec(page_block, sel_map(r)) for r in range(PPS)]
                     + [per_b((1, PAGE_SIZE, KV_WIDTH)), per_b((1, span, KV_WIDTH))],
            out_specs=per_b((1, G, D, C)),
            scratch_shapes=[pltpu.VMEM((n_blk, KV_WIDTH), jnp.float32),
                            pltpu.VMEM((G, D, C), jnp.float32),
                            pltpu.VMEM((G, n_sel_pad, C), jnp.float32),
                            pltpu.VMEM((G, 1, C), jnp.float32),
                            pltpu.VMEM((G, 1, C), jnp.float32),
                            pltpu.VMEM((G, D, C), jnp.float32)]),
        out_shape=jax.ShapeDtypeStruct((b, G, D, C), jnp.float32),
        compiler_params=pltpu.CompilerParams(dimension_semantics=("arbitrary", "arbitrary", "arbitrary"),
                                             vmem_limit_bytes=32 * 1024 * 1024),
        name="nsa_sample",
    )(page_table, qs, gates, *([pool_cmp] * PPS), *([pool_sel] * PPS), tail, wkeys)
    o_attn = jnp.transpose(out.reshape(b, G, D, J, T), (0, 4, 1, 3, 2)).reshape(b, T, ATTN_WIDTH)
    return o_attn, keys[:, -w_eff:].reshape(b, w_eff, KV_HEADS, 2, D)


PEER_PICKS = PEER_HEADS * PEER_TOPK
PEER_TT = 128
PEER_GT = 128
ROW_SUBLANES = D_MODEL // 128


def _top16_cols(s, n_rows, iota_rows):
    vals, rows = [], []
    for _ in range(PEER_TOPK):
        mx = jnp.max(s, axis=0, keepdims=True)
        first = jnp.min(jnp.where(s == mx, iota_rows, n_rows), axis=0, keepdims=True)
        vals.append(mx)
        rows.append(first)
        s = jnp.where(iota_rows == first, -jnp.inf, s)
    return jnp.concatenate(vals, axis=0), jnp.concatenate(rows, axis=0)


def _peer_route_kernel(h_ref, wq_ref, keys_ref, exp_ref, gate_ref):
    f32 = jnp.float32
    TT = PEER_TT
    q = jnp.dot(h_ref[...].astype(jnp.bfloat16), wq_ref[...], preferred_element_type=f32)
    iota_k = lax.broadcasted_iota(jnp.int32, (PEER_KEYS, TT), 0)
    widths = [PEER_TOPK // (a + 1) for a in range(PEER_TOPK)]
    n_cand = -(-sum(widths) // 8) * 8
    sub = lax.broadcasted_iota(jnp.int32, (n_cand, 1), 0)
    iota_c = jnp.full((n_cand, 1), PEER_TOPK * PEER_TOPK, jnp.int32)
    row = 0
    for a in range(PEER_TOPK):
        iota_c = jnp.where((sub >= row) & (sub < row + widths[a]), a * PEER_TOPK + (sub - row), iota_c)
        row += widths[a]
    iota_c = jnp.broadcast_to(iota_c, (n_cand, TT))
    exp_rows, gate_rows = [], []
    for hd in range(PEER_HEADS):
        tops = []
        for c in range(2):
            j = hd * 2 + c
            qj = q[:, j * PEER_HALF:(j + 1) * PEER_HALF].astype(jnp.bfloat16)
            st = lax.dot_general(keys_ref[j], qj, (((1,), (1,)), ((), ())), preferred_element_type=f32)
            tops.append(_top16_cols(st, PEER_KEYS, iota_k))
        (s0, i0), (s1, i1) = tops
        fill = n_cand - sum(widths)
        cand_s = jnp.concatenate([s0[a:a + 1, :] + s1[:widths[a], :] for a in range(PEER_TOPK)]
                                 + [jnp.full((fill, TT), -jnp.inf, f32)], axis=0)
        cand_i = jnp.concatenate([i0[a:a + 1, :] * PEER_KEYS + i1[:widths[a], :] for a in range(PEER_TOPK)]
                                 + [jnp.zeros((fill, TT), jnp.int32)], axis=0)
        best_s, best_e = [], []
        s = cand_s
        for _ in range(PEER_TOPK):
            mx = jnp.max(s, axis=0, keepdims=True)
            first = jnp.min(jnp.where(s == mx, iota_c, PEER_TOPK * PEER_TOPK), axis=0, keepdims=True)
            pick = iota_c == first
            best_s.append(mx)
            best_e.append(jnp.sum(jnp.where(pick, cand_i, 0), axis=0, keepdims=True))
            s = jnp.where(pick, -jnp.inf, s)
        bs = jnp.concatenate(best_s, axis=0)
        be = jnp.concatenate(best_e, axis=0)
        ex = jnp.exp(bs - bs[0:1, :])
        gate_rows.append(ex / jnp.sum(ex, axis=0, keepdims=True))
        exp_rows.append(be)
    exp_ref[...] = _row_code(jnp.concatenate(exp_rows, axis=0)).T
    gate_ref[...] = jnp.concatenate(gate_rows, axis=0).T


def _peer_route(h, wq_b, keys_b):
    n = h.shape[0]
    TT = PEER_TT
    return pl.pallas_call(
        _peer_route_kernel,
        grid=(n // TT,),
        in_specs=[pl.BlockSpec((TT, D_MODEL), lambda i: (i, 0)),
                  pl.BlockSpec((D_MODEL, PEER_HEADS * PEER_QDIM), lambda i: (0, 0)),
                  pl.BlockSpec((PEER_HEADS * 2, PEER_KEYS, PEER_HALF), lambda i: (0, 0, 0))],
        out_specs=[pl.BlockSpec((TT, PEER_PICKS), lambda i: (i, 0)),
                   pl.BlockSpec((TT, PEER_PICKS), lambda i: (i, 0))],
        out_shape=[jax.ShapeDtypeStruct((n, PEER_PICKS), jnp.int32),
                   jax.ShapeDtypeStruct((n, PEER_PICKS), jnp.float32)],
        compiler_params=pltpu.CompilerParams(dimension_semantics=("arbitrary",),
                                             vmem_limit_bytes=40 * 1024 * 1024),
        name="peer_route",
    )(h, wq_b, keys_b)


def _pack_rows(w):
    b = lax.bitcast_convert_type(w.astype(jnp.bfloat16), jnp.uint16).astype(jnp.uint32)
    b = b.reshape(N_EXPERTS // 2, 2, D_MODEL)
    return ((b[:, 0] << 16) | b[:, 1]).reshape(N_EXPERTS // 2 * ROW_SUBLANES, 128)


def _row_code(e):
    return (e >> 1) * ROW_SUBLANES + (e & 1)


PARITY_SHIFT_BIT = 4


def _load_row(tab_ref, start, shift_word):
    w = tab_ref[pl.ds(pl.multiple_of(start, ROW_SUBLANES), ROW_SUBLANES), :]
    w = (w << (shift_word & jnp.uint32(1 << PARITY_SHIFT_BIT))) & jnp.uint32(0xFFFF0000)
    return lax.bitcast_convert_type(w, jnp.float32)


def _group_shift_words(codes):
    par = (codes & 1).reshape(-1, PEER_GROUP)
    return jnp.sum(par << (PARITY_SHIFT_BIT + jnp.arange(PEER_GROUP, dtype=jnp.int32)), axis=1, dtype=jnp.int32)


def _splat_u32(s):
    return lax.bitcast_convert_type(jnp.full((ROW_SUBLANES, 128), s, jnp.int32), jnp.uint32)


def _sublane_sums(tiles):
    sub = lax.broadcasted_iota(jnp.int32, (ROW_SUBLANES, 128), 0)
    step = ROW_SUBLANES // 2
    while step >= 1:
        low = (sub % (2 * step)) < step
        tiles = [jnp.where(low, a + pltpu.roll(a, ROW_SUBLANES - step, 0), pltpu.roll(b, step, 0) + b)
                 for a, b in zip(tiles[:len(tiles) // 2], tiles[len(tiles) // 2:])]
        step //= 2
    return tiles[0]


PEER_GROUP = 16


PEER_GROUPS_PER_TOKEN = PEER_PICKS // PEER_GROUP


def _pipelined_groups(n_groups, produce, consume, stage_a, stage_b):
    def put(stage, tiles):
        for k, tile in enumerate(tiles):
            stage[k] = tile

    def get(stage):
        return [stage[k] for k in range(PEER_GROUP)]

    def body(j, carry):
        i = 2 * j + 1
        put(stage_b, produce(i))
        consume(i - 1, get(stage_a))
        put(stage_a, produce(i + 1))
        consume(i, get(stage_b))
        return carry

    put(stage_a, produce(0))
    lax.fori_loop(0, n_groups // 2 - 1, body, 0)
    put(stage_b, produce(n_groups - 1))
    consume(n_groups - 2, get(stage_a))
    consume(n_groups - 1, get(stage_b))


def _tree_sum(tiles):
    while len(tiles) > 1:
        tiles = [a + b for a, b in zip(tiles[0::2], tiles[1::2])]
    return tiles[0]


def _peer_up_kernel(start_ref, shift_ref, h_ref, gate_ref, tab_ref, coef_ref, dots_ref, stage_a, stage_b):
    GT = PEER_GT

    def products(i):
        t = i // PEER_GROUPS_PER_TOKEN
        h = h_ref[pl.ds(pl.multiple_of(t * ROW_SUBLANES, ROW_SUBLANES), ROW_SUBLANES), :]
        shifts = _splat_u32(shift_ref[i])
        return tuple(_load_row(tab_ref, start_ref[i * PEER_GROUP + kk], shifts >> jnp.uint32(kk)) * h
                     for kk in range(PEER_GROUP))

    def reduce_rows(i, prods):
        rows = [_sublane_sums(list(prods[k:k + ROW_SUBLANES])) for k in range(0, PEER_GROUP, ROW_SUBLANES)]
        dots_ref[pl.ds(pl.multiple_of(i * PEER_GROUP, PEER_GROUP), PEER_GROUP), :] = jnp.concatenate(rows, axis=0)

    _pipelined_groups(GT * PEER_GROUPS_PER_TOKEN, products, reduce_rows, stage_a, stage_b)
    rows = [jnp.sum(dots_ref[t * PEER_PICKS:(t + 1) * PEER_PICKS, :].T, axis=0, keepdims=True) for t in range(GT)]
    coef_ref[...] = gate_ref[...] * jax.nn.gelu(jnp.concatenate(rows, axis=0))


def _peer_down_kernel(start_ref, shift_ref, coef_ref, tab_ref, o_ref, part_ref, stage_a, stage_b):
    GT = PEER_GT

    def products(i):
        shifts = _splat_u32(shift_ref[i])
        return tuple(_load_row(tab_ref, start_ref[i * PEER_GROUP + kk], shifts >> jnp.uint32(kk))
                     * jnp.full((ROW_SUBLANES, 128), coef_ref[i * PEER_GROUP + kk], jnp.float32)
                     for kk in range(PEER_GROUP))

    def partial_sum(i, prods):
        part_ref[pl.ds(pl.multiple_of(i * ROW_SUBLANES, ROW_SUBLANES), ROW_SUBLANES), :] = _tree_sum(list(prods))

    _pipelined_groups(GT * PEER_GROUPS_PER_TOKEN, products, partial_sum, stage_a, stage_b)
    parts = part_ref[...].reshape(GT, PEER_GROUPS_PER_TOKEN, ROW_SUBLANES, 128)
    o_ref[...] = jnp.sum(parts, axis=1).reshape(GT * ROW_SUBLANES, 128)


def _peer_gather(h, experts, gates, u_tab, v_tab):
    n = h.shape[0]
    GT = PEER_GT
    tab_rows = N_EXPERTS // 2 * ROW_SUBLANES
    flat_smem = pl.BlockSpec((GT * PEER_PICKS,), lambda i: (i,), memory_space=pltpu.MemorySpace.SMEM)
    tab_spec = pl.BlockSpec((tab_rows, 128), lambda i: (0, 0), pipeline_mode=pl.Buffered(1))
    params = pltpu.CompilerParams(dimension_semantics=("arbitrary",), vmem_limit_bytes=56 * 1024 * 1024)
    group_smem = pl.BlockSpec((GT * PEER_GROUPS_PER_TOKEN,), lambda i: (i,), memory_space=pltpu.MemorySpace.SMEM)
    starts = (experts & ~(ROW_SUBLANES - 1)).reshape(n * PEER_PICKS)
    shifts = _group_shift_words(experts)
    stage = pltpu.VMEM((PEER_GROUP, ROW_SUBLANES, 128), jnp.float32)
    coef = pl.pallas_call(
        _peer_up_kernel,
        grid=(n // GT,),
        in_specs=[flat_smem, group_smem,
                  pl.BlockSpec((GT * ROW_SUBLANES, 128), lambda i: (i, 0)),
                  pl.BlockSpec((GT, PEER_PICKS), lambda i: (i, 0)),
                  tab_spec],
        out_specs=pl.BlockSpec((GT, PEER_PICKS), lambda i: (i, 0)),
        out_shape=jax.ShapeDtypeStruct((n, PEER_PICKS), jnp.float32),
        scratch_shapes=[pltpu.VMEM((GT * PEER_PICKS, 128), jnp.float32), stage, stage],
        compiler_params=params,
        name="peer_up",
    )(starts, shifts, h.reshape(n * ROW_SUBLANES, 128), gates, u_tab)
    out = pl.pallas_call(
        _peer_down_kernel,
        grid=(n // GT,),
        in_specs=[flat_smem, group_smem, flat_smem, tab_spec],
        out_specs=pl.BlockSpec((GT * ROW_SUBLANES, 128), lambda i: (i, 0)),
        out_shape=jax.ShapeDtypeStruct((n * ROW_SUBLANES, 128), jnp.float32),
        scratch_shapes=[pltpu.VMEM((GT * PEER_GROUPS_PER_TOKEN * ROW_SUBLANES, 128), jnp.float32), stage, stage],
        compiler_params=params,
        name="peer_down",
    )(starts, shifts, coef.reshape(n * PEER_PICKS), v_tab)
    return out.reshape(n, D_MODEL)


def _layer_weights(w_in, w_ssm_branch, w_attn_branch, w_out, peer_wq, peer_sub_keys, peer_u, peer_v):
    bf16 = jnp.bfloat16
    offs = np.cumsum((0,) + PROJ_SIZES).tolist()
    seg = lambda k: w_in[:, offs[k]:offs[k + 1]]
    pad = jnp.zeros((D_MODEL, SMALL_WIDTH - SSM_HEADS - 3 * ATTN_HEADS), w_in.dtype)
    return dict(
        in_ssm=[seg(0).astype(bf16), seg(1).astype(bf16)],
        in_attn=[seg(3).astype(bf16), jnp.concatenate([seg(4), seg(5), seg(6)], axis=1).astype(bf16),
                 seg(8).astype(bf16), jnp.concatenate([seg(2), seg(7), pad], axis=1).astype(bf16)],
        w_ssm=w_ssm_branch.astype(bf16), w_attn=w_attn_branch.astype(bf16), w_out=w_out.astype(bf16),
        wq=peer_wq.astype(bf16),
        keys=peer_sub_keys.reshape(PEER_HEADS * 2, PEER_KEYS, PEER_HALF).astype(bf16),
        u_tab=_pack_rows(peer_u), v_tab=_pack_rows(peer_v))


def _block(x, conv_prev, ssm_prev, nsa_core, lw, norm_mix_w, conv_w, conv_b, dt_bias, a_log, d_skip, ssm_norm_w,
           norm_ffn_w):
    b, T, _ = x.shape
    n = b * T
    xf = x.reshape(n, D_MODEL)
    z, xbc = _norm_proj(xf, norm_mix_w, lw["in_ssm"], "in_proj_ssm")
    q, kv, br_g, small = _norm_proj(xf, norm_mix_w, lw["in_attn"], "in_proj_attn")
    nsa_g = small[:, SSM_HEADS:SSM_HEADS + 3 * ATTN_HEADS]
    kv_c, kv_s, kv_w = (kv[:, k * KV_WIDTH:(k + 1) * KV_WIDTH].reshape(b, T, KV_WIDTH) for k in range(3))
    y_ssm, conv_new, ssm_new = _mamba_branch(z.reshape(b, T, D_INNER), xbc.reshape(b, T, CONV_DIM),
                                             small.reshape(b, T, SMALL_WIDTH), conv_prev, ssm_prev, conv_w, conv_b,
                                             dt_bias, a_log, d_skip, ssm_norm_w)
    o_attn, win_new = nsa_core(q.reshape(b, T, ATTN_WIDTH), kv_c, kv_s, kv_w, nsa_g.reshape(b, T, 3 * ATTN_HEADS))
    x2, h2 = _merge_proj(xf, o_attn.reshape(n, ATTN_WIDTH), y_ssm.reshape(n, D_INNER), br_g, norm_ffn_w,
                         lw["w_attn"], lw["w_ssm"], lw["w_out"])
    experts, gates = _peer_route(h2, lw["wq"], lw["keys"])
    peer = _peer_gather(h2, experts, gates, lw["u_tab"], lw["v_tab"])
    kv_shape = (b, T, KV_HEADS, 2, ATTN_HEAD_DIM)
    return x2, peer, kv_c.reshape(kv_shape), kv_s.reshape(kv_shape), win_new, conv_new, ssm_new


def kernel(x_prompt, x_sample, cache_cmp_kv, cache_sel_kv, cache_win_kv, state_ssm, state_conv, page_table,
           norm_mix_w, w_in, conv_w, conv_b, dt_bias, a_log, d_skip, ssm_norm_w, w_ssm_branch, w_attn_branch,
           w_out, norm_ffn_w, peer_wq, peer_sub_keys, peer_u, peer_v, final_norm_w):
    xp, xs = x_prompt, x_sample
    cmp_p, cmp_s, sel_p, sel_s, win_p, win_s = [], [], [], [], [], []
    ssm_p, ssm_s, conv_p, conv_s = [], [], [], []
    for layer in range(DEPTH):
        lw = _layer_weights(*(a[layer] for a in (w_in, w_ssm_branch, w_attn_branch, w_out, peer_wq, peer_sub_keys,
                                                   peer_u, peer_v)))
        rest = [a[layer] for a in (norm_mix_w, conv_w, conv_b, dt_bias, a_log, d_skip, ssm_norm_w, norm_ffn_w)]
        last = layer == DEPTH - 1
        bp = xp.shape[0]
        conv0 = jnp.zeros((bp, CONV_WIDTH - 1, CONV_DIM), xp.dtype)
        ssm0 = jnp.zeros((bp, SSM_HEADS, SSM_HEAD_DIM, SSM_STATE), xp.dtype)
        x2, peer, kc, ks, kw, cv, ss = _block(xp, conv0, ssm0, _nsa_prompt, lw, *rest)
        xp = (_add_norm(x2, peer, final_norm_w) if last else x2 + peer).reshape(xp.shape)
        cmp_p.append(kc); sel_p.append(ks); win_p.append(kw); conv_p.append(cv); ssm_p.append(ss)
        core = functools.partial(_nsa_sample, cache_cmp_kv[layer], cache_sel_kv[layer], cache_win_kv[layer], page_table)
        x2, peer, kc, ks, kw, cv, ss = _block(xs, state_conv[layer], state_ssm[layer], core, lw, *rest)
        xs = (_add_norm(x2, peer, final_norm_w) if last else x2 + peer).reshape(xs.shape)
        cmp_s.append(kc); sel_s.append(ks); win_s.append(kw); conv_s.append(cv); ssm_s.append(ss)
    return (xp, xs, jnp.stack(cmp_p), jnp.stack(cmp_s), jnp.stack(sel_p), jnp.stack(sel_s),
            jnp.stack(win_p), jnp.stack(win_s), jnp.stack(ssm_p), jnp.stack(ssm_s), jnp.stack(conv_p), jnp.stack(conv_s))
```

```python
import math, functools
import jax, jax.numpy as jnp
from jax import lax
import numpy as np
from jax.experimental import pallas as pl
from jax.experimental.pallas import tpu as pltpu

D_MODEL = 1024
BATCH = 8
SEQ = 4096
DEPTH = 1
DEC_BATCH = 32
DEC_SEQ = 8
PAST_LEN = 16384
PAGE_SIZE = 128

SSM_EXPAND = 2
D_INNER = SSM_EXPAND * D_MODEL
SSM_HEAD_DIM = 64
SSM_HEADS = D_INNER // SSM_HEAD_DIM
SSM_GROUPS = 4
SSM_HPG = SSM_HEADS // SSM_GROUPS
SSM_STATE = 128
CONV_WIDTH = 4
CONV_DIM = D_INNER + 2 * SSM_GROUPS * SSM_STATE
SSD_CHUNK = 128
ATTN_HEADS = 16
ATTN_HEAD_DIM = 64
ATTN_WIDTH = ATTN_HEADS * ATTN_HEAD_DIM
KV_HEADS = 2
HEADS_PER_KV = ATTN_HEADS // KV_HEADS
KV_WIDTH = 2 * KV_HEADS * ATTN_HEAD_DIM
CMP_BLOCK = 64
SEL_BLOCK = CMP_BLOCK
SEL_TOPK = 16
SEL_BONUS = 1000.0
WINDOW = 512
WIN_QBLOCK = 128
SEL_QBLOCK = 32
ATTN_SCALE = ATTN_HEAD_DIM ** -0.5
PEER_HEADS = 8
PEER_KEYS = 128
N_EXPERTS = PEER_KEYS * PEER_KEYS
PEER_QDIM = 256
PEER_HALF = PEER_QDIM // 2
PEER_TOPK = 16
PEER_TOKEN_CHUNK = 128
EPS = 1e-6
PROJ_SIZES = (D_INNER, CONV_DIM, SSM_HEADS, ATTN_WIDTH, KV_WIDTH, KV_WIDTH, KV_WIDTH, 3 * ATTN_HEADS, 2 * D_MODEL)
PROJ_DIM = sum(PROJ_SIZES)


PROJ_TM = 512
PROJ_VMEM_BYTES = 56 * 1024 * 1024


def _rms(x, w):
    return x * lax.rsqrt(jnp.mean(x * x, axis=-1, keepdims=True) + EPS) * w


def _resident(shape):
    return pl.BlockSpec(shape, lambda i: (0,) * len(shape), pipeline_mode=pl.Buffered(1))


def _norm_proj_kernel(x_ref, nw_ref, *refs):
    n_out = len(refs) // 2
    h = _rms(x_ref[...], nw_ref[...]).astype(jnp.bfloat16)
    for w_ref, o_ref in zip(refs[:n_out], refs[n_out:]):
        o_ref[...] = jnp.dot(h, w_ref[...], preferred_element_type=jnp.float32)


def _norm_proj(x, norm_w, weights, name):
    n = x.shape[0]
    tm = min(PROJ_TM, n)
    return pl.pallas_call(
        _norm_proj_kernel,
        grid=(n // tm,),
        in_specs=[pl.BlockSpec((tm, D_MODEL), lambda i: (i, 0)), _resident((1, D_MODEL))]
                 + [_resident(w.shape) for w in weights],
        out_specs=[pl.BlockSpec((tm, w.shape[1]), lambda i: (i, 0)) for w in weights],
        out_shape=[jax.ShapeDtypeStruct((n, w.shape[1]), jnp.float32) for w in weights],
        compiler_params=pltpu.CompilerParams(dimension_semantics=("arbitrary",), vmem_limit_bytes=PROJ_VMEM_BYTES),
        name=name,
    )(x, norm_w.reshape(1, D_MODEL), *weights)


def _merge_proj_kernel(x_ref, oa_ref, ys_ref, g_ref, nw_ref, wa_ref, ws_ref, wo_ref, x2_ref, h2_ref):
    f32, bf16 = jnp.float32, jnp.bfloat16
    y_attn = jnp.dot(oa_ref[...].astype(bf16), wa_ref[...], preferred_element_type=f32)
    y_ssm = jnp.dot(ys_ref[...].astype(bf16), ws_ref[...], preferred_element_type=f32)
    g = jax.nn.sigmoid(g_ref[...])
    merged = g[:, :D_MODEL] * y_ssm + g[:, D_MODEL:] * y_attn
    x2 = x_ref[...] + jnp.dot(merged.astype(bf16), wo_ref[...], preferred_element_type=f32)
    x2_ref[...] = x2
    h2_ref[...] = _rms(x2, nw_ref[...])


def _merge_proj(x, o_attn, y_ssm, br_g, norm_w, wa_b, ws_b, wo_b):
    n = x.shape[0]
    tm = min(PROJ_TM, n)
    rows = lambda c: pl.BlockSpec((tm, c), lambda i: (i, 0))
    return pl.pallas_call(
        _merge_proj_kernel,
        grid=(n // tm,),
        in_specs=[rows(D_MODEL), rows(ATTN_WIDTH), rows(D_INNER), rows(2 * D_MODEL), _resident((1, D_MODEL)),
                  _resident(wa_b.shape), _resident(ws_b.shape), _resident(wo_b.shape)],
        out_specs=[rows(D_MODEL), rows(D_MODEL)],
        out_shape=[jax.ShapeDtypeStruct((n, D_MODEL), jnp.float32)] * 2,
        compiler_params=pltpu.CompilerParams(dimension_semantics=("arbitrary",), vmem_limit_bytes=PROJ_VMEM_BYTES),
        name="merge_proj",
    )(x, o_attn, y_ssm, br_g, norm_w.reshape(1, D_MODEL), wa_b, ws_b, wo_b)


def _add_norm_kernel(x_ref, y_ref, w_ref, o_ref):
    o_ref[...] = _rms(x_ref[...] + y_ref[...], w_ref[...])


def _add_norm(x, y, w):
    n = x.shape[0]
    tm = min(PROJ_TM, n)
    rows = pl.BlockSpec((tm, D_MODEL), lambda i: (i, 0))
    return pl.pallas_call(
        _add_norm_kernel,
        grid=(n // tm,),
        in_specs=[rows, rows, _resident((1, D_MODEL))],
        out_specs=rows,
        out_shape=jax.ShapeDtypeStruct((n, D_MODEL), jnp.float32),
        name="add_norm",
    )(x, y, w.reshape(1, D_MODEL))


SMALL_WIDTH = 128
CONV_TAIL = 8


def _ssd_kernel(z_ref, xbc_ref, sm_ref, cprev_ref, h0_ref, cw_ref, cbias_ref, dtb_ref, alog_ref, dskip_ref, nw_ref,
                y_ref, hout_ref, state_ref, tail_ref, *, valid):
    f32, bf16 = jnp.float32, jnp.bfloat16
    L, P, N = SSD_CHUNK, SSM_HEAD_DIM, SSM_STATE
    c = pl.program_id(1)

    @pl.when(c == 0)
    def _init():
        state_ref[...] = h0_ref[0]
        tail_ref[...] = cprev_ref[0]

    xbc = xbc_ref[0]
    padded = jnp.concatenate([tail_ref[...], xbc], axis=0)
    acc = jnp.broadcast_to(cbias_ref[...], (L, CONV_DIM))
    for w in range(CONV_WIDTH):
        lo = CONV_TAIL - (CONV_WIDTH - 1) + w
        acc = acc + cw_ref[w:w + 1, :] * padded[lo:lo + L, :]
    xc = acc * jax.nn.sigmoid(acc)
    tail_ref[...] = xbc[L - CONV_TAIL:, :]

    row = lax.broadcasted_iota(jnp.int32, (L, 1), 0)
    live = row < (valid - c * L)
    dt = jnp.where(live, jax.nn.softplus(sm_ref[0] + dtb_ref[...]), 0.0)
    a = dt * (-jnp.exp(alog_ref[...]))
    tril = (row >= lax.broadcasted_iota(jnp.int32, (L, L), 1))
    a_cum = jnp.dot(tril.astype(f32), a, preferred_element_type=f32, precision=lax.Precision.HIGHEST)
    a_cum_t = a_cum.T
    a_last = a_cum[L - 1:L, :]
    grow = jnp.exp(a_cum)
    to_end = jnp.exp(a_last - a_cum)
    carry = jnp.exp(a_last)

    W = SSM_HPG * P

    def per_head(v, g):
        cols = [jnp.broadcast_to(v[:, g * SSM_HPG + j:g * SSM_HPG + j + 1], (v.shape[0], P)) for j in range(SSM_HPG)]
        return jnp.concatenate(cols, axis=1)

    xs = xc[:, :D_INNER]
    ys = []
    for g in range(SSM_GROUPS):
        bg = xc[:, D_INNER + g * N:D_INNER + (g + 1) * N]
        cg = xc[:, D_INNER + SSM_GROUPS * N + g * N:D_INNER + SSM_GROUPS * N + (g + 1) * N]
        bg_b, cg_b = bg.astype(bf16), cg.astype(bf16)
        bg_t = bg.T.astype(bf16)
        cb = lax.dot_general(cg_b, bg_b, (((1,), (1,)), ((), ())), preferred_element_type=f32)
        xs_g = xs[:, g * W:(g + 1) * W]
        xdt = xs_g * per_head(dt, g)
        xdt_b = xdt.astype(bf16)
        st = state_ref[g]
        y = jnp.dot(cg_b, st.astype(bf16), preferred_element_type=f32) * per_head(grow, g)
        state_ref[g] = st * per_head(carry, g) + jnp.dot(bg_t, (xdt * per_head(to_end, g)).astype(bf16),
                                                         preferred_element_type=f32)
        diag = []
        for j in range(SSM_HPG):
            h = g * SSM_HPG + j
            diff = a_cum[:, h:h + 1] - a_cum_t[h:h + 1, :]
            m = (cb * jnp.exp(jnp.where(tril, diff, -jnp.inf))).astype(bf16)
            diag.append(jnp.dot(m, xdt_b[:, j * P:(j + 1) * P], preferred_element_type=f32))
        ys.append(jnp.concatenate(diag, axis=1) + y + per_head(dskip_ref[...], g) * xs_g)
    z = z_ref[0]
    y_ref[0] = _rms(jnp.concatenate(ys, axis=1) * (z * jax.nn.sigmoid(z)), nw_ref[...])

    @pl.when(c == pl.num_programs(1) - 1)
    def _emit():
        hout_ref[0] = state_ref[...]


def _mamba_branch(z, xbc, small, conv_prev, ssm_prev, conv_w, conv_b, dt_bias, a_log, d_skip, ssm_norm_w):
    b, L, _ = xbc.shape
    Lc = SSD_CHUNK
    Lp = -(-L // Lc) * Lc
    keep = CONV_WIDTH - 1
    conv_new = xbc[:, L - keep:] if L >= keep else jnp.concatenate([conv_prev, xbc], axis=1)[:, -keep:]
    if Lp != L:
        padrows = lambda a: jnp.pad(a, ((0, 0), (0, Lp - L), (0, 0)))
        z, xbc, small = padrows(z), padrows(xbc), padrows(small)
    cprev = jnp.pad(conv_prev, ((0, 0), (CONV_TAIL - (CONV_WIDTH - 1), 0), (0, 0)))
    G, J, P, N = SSM_GROUPS, SSM_HPG, SSM_HEAD_DIM, SSM_STATE
    state_block = (1, G, N, J * P)
    h0 = jnp.transpose(ssm_prev.reshape(b, G, J, P, N), (0, 1, 4, 2, 3)).reshape(b, G, N, J * P)
    lane = lambda v: jnp.pad(v.reshape(1, SSM_HEADS), ((0, 0), (0, SMALL_WIDTH - SSM_HEADS)))
    full = lambda shape: pl.BlockSpec(shape, lambda bi, c: (0,) * len(shape))
    kern = functools.partial(_ssd_kernel, valid=L)
    y, hout = pl.pallas_call(
        kern,
        grid=(b, Lp // Lc),
        in_specs=[pl.BlockSpec((1, Lc, D_INNER), lambda bi, c: (bi, c, 0)),
                  pl.BlockSpec((1, Lc, CONV_DIM), lambda bi, c: (bi, c, 0)),
                  pl.BlockSpec((1, Lc, SMALL_WIDTH), lambda bi, c: (bi, c, 0)),
                  pl.BlockSpec((1, CONV_TAIL, CONV_DIM), lambda bi, c: (bi, 0, 0)),
                  pl.BlockSpec(state_block, lambda bi, c: (bi, 0, 0, 0)),
                  full((CONV_WIDTH, CONV_DIM)), full((1, CONV_DIM)), full((1, SMALL_WIDTH)), full((1, SMALL_WIDTH)),
                  full((1, SMALL_WIDTH)), full((1, D_INNER))],
        out_specs=[pl.BlockSpec((1, Lc, D_INNER), lambda bi, c: (bi, c, 0)),
                   pl.BlockSpec(state_block, lambda bi, c: (bi, 0, 0, 0))],
        out_shape=[jax.ShapeDtypeStruct((b, Lp, D_INNER), jnp.float32),
                   jax.ShapeDtypeStruct((b,) + state_block[1:], jnp.float32)],
        scratch_shapes=[pltpu.VMEM(state_block[1:], jnp.float32),
                        pltpu.VMEM((CONV_TAIL, CONV_DIM), jnp.float32)],
        compiler_params=pltpu.CompilerParams(dimension_semantics=("arbitrary", "arbitrary"),
                                             vmem_limit_bytes=48 * 1024 * 1024),
        name="mamba_ssd",
    )(z, xbc, small, cprev, h0, conv_w, conv_b.reshape(1, CONV_DIM), lane(dt_bias), lane(a_log), lane(d_skip),
      ssm_norm_w.reshape(1, D_INNER))
    ssm_new = jnp.transpose(hout.reshape(b, G, N, J, P), (0, 1, 3, 4, 2)).reshape(b, SSM_HEADS, P, N)
    return y[:, :L], conv_new, ssm_new


NSA_TQ = 128
NSA_TK = 512
NSA_WSPAN = WINDOW + NSA_TQ
NEG_BIG = -1e30


def _nsa_prompt_kernel(q_ref, kvc_ref, ks_ref, vst_ref, kw_ref, vwt_ref, gate_ref, o_ref,
                       pk_ref, pvt_ref, m_ref, l_ref, acc_ref, *, seq):
    f32, bf16 = jnp.float32, jnp.bfloat16
    D, J, TQ, TK = ATTN_HEAD_DIM, HEADS_PER_KV, NSA_TQ, NSA_TK
    n_blk = seq // CMP_BLOCK
    i = pl.program_id(2)
    t0 = i * TQ

    @pl.when(i == 0)
    def _pool():
        c = kvc_ref[0]
        pooled = jnp.sum(c.reshape(n_blk, CMP_BLOCK, 2 * D), axis=1) * (1.0 / CMP_BLOCK)
        pk_ref[...] = pooled[:, :D].astype(bf16)
        pvt_ref[...] = pooled[:, D:].T.astype(bf16)

    qt = (q_ref[0] * ATTN_SCALE).T
    qs = jnp.concatenate([qt[j * D:(j + 1) * D, :] for j in range(J)], axis=1).astype(bf16)
    qpos1 = t0 + lax.broadcasted_iota(jnp.int32, (1, TQ), 1)
    qpos = jnp.concatenate([qpos1] * J, axis=1)

    s = jnp.dot(pk_ref[...], qs, preferred_element_type=f32)
    blk = lax.broadcasted_iota(jnp.int32, (n_blk, 1), 0)
    vis = (blk * CMP_BLOCK + (CMP_BLOCK - 1)) <= qpos
    s = jnp.where(vis, s, -jnp.inf)
    m = jnp.max(s, axis=0, keepdims=True)
    m = jnp.where(m == -jnp.inf, 0.0, m)
    p = jnp.exp(s - m)
    p = p / jnp.maximum(jnp.sum(p, axis=0, keepdims=True), 1e-30)
    o_cmp = jnp.dot(pvt_ref[...], p.astype(bf16), preferred_element_type=f32)
    imp = p[:, 0:TQ]
    for j in range(1, J):
        imp = imp + p[:, j * TQ:(j + 1) * TQ]

    cur = qpos1 // CMP_BLOCK
    blk2 = lax.broadcasted_iota(jnp.int32, (n_blk, TQ), 0)
    forced = ((blk2 == 0) | (blk2 == cur) | (blk2 == cur - 1)).astype(f32)
    score = jnp.where(blk2 <= cur, imp + SEL_BONUS * forced, -jnp.inf)
    sel = jnp.zeros((n_blk, TQ), f32)
    for _ in range(min(SEL_TOPK, n_blk)):
        mx = jnp.max(score, axis=0, keepdims=True)
        first = jnp.min(jnp.where(score == mx, blk2, n_blk), axis=0, keepdims=True)
        pick = blk2 == first
        sel = jnp.where(pick, 1.0, sel)
        score = jnp.where(pick, -jnp.inf, score)
    sel_bias = ((sel - 1.0) * -NEG_BIG).astype(bf16)
    qs_sel = jnp.concatenate([qs, jnp.concatenate([sel_bias] * J, axis=1)], axis=0)

    m_ref[...] = jnp.full(m_ref.shape, NEG_BIG, f32)
    l_ref[...] = jnp.zeros(l_ref.shape, f32)
    acc_ref[...] = jnp.zeros(acc_ref.shape, f32)
    n_kt = (t0 + TQ + TK - 1) // TK

    def sel_tile(kt, causal):
        k0 = pl.multiple_of(kt * TK, TK)
        st = jnp.dot(ks_ref[0, 0, pl.ds(k0, TK), :], qs_sel, preferred_element_type=f32)
        if causal:
            kpos = k0 + lax.broadcasted_iota(jnp.int32, (TK, 1), 0)
            st = st + jnp.concatenate([jnp.where(kpos <= qpos1, 0.0, NEG_BIG)] * J, axis=1)
        m_old = m_ref[...]
        m_new = jnp.maximum(m_old, jnp.max(st, axis=0, keepdims=True))
        alpha = jnp.exp(m_old - m_new)
        pt = jnp.exp(st - m_new)
        l_ref[...] = alpha * l_ref[...] + jnp.sum(pt, axis=0, keepdims=True)
        acc_ref[...] = alpha * acc_ref[...] + jnp.dot(vst_ref[0, 0, :, pl.ds(k0, TK)], pt.astype(bf16),
                                                      preferred_element_type=f32)
        m_ref[...] = m_new

    def past_tile(kt, carry):
        sel_tile(kt, causal=False)
        return carry

    lax.fori_loop(0, n_kt - 1, past_tile, 0)
    sel_tile(n_kt - 1, causal=True)
    o_sel = acc_ref[...] / l_ref[...]

    w0 = pl.multiple_of(jnp.maximum(t0 - WINDOW, 0), TQ)
    span = min(NSA_WSPAN, seq)
    kposw = w0 + lax.broadcasted_iota(jnp.int32, (span, 1), 0)
    rel = qpos1 - kposw
    biasw1 = jnp.where((rel >= 0) & (rel <= WINDOW), 0.0, NEG_BIG)
    biasw = jnp.concatenate([biasw1] * J, axis=1)
    sw = jnp.dot(kw_ref[0, 0, pl.ds(w0, span), :], qs, preferred_element_type=f32) + biasw
    pw = jnp.exp(sw - jnp.max(sw, axis=0, keepdims=True))
    lw = jnp.sum(pw, axis=0, keepdims=True)
    o_win = jnp.dot(vwt_ref[0, 0, :, pl.ds(w0, span)], pw.astype(bf16), preferred_element_type=f32) / lw

    def gate_row(br):
        g = jax.nn.sigmoid(gate_ref[0, 0, br])
        return jnp.concatenate([g[j:j + 1, :] for j in range(J)], axis=1)

    out = gate_row(0) * o_cmp + gate_row(1) * o_sel + gate_row(2) * o_win
    out = jnp.concatenate([out[:, j * TQ:(j + 1) * TQ] for j in range(J)], axis=0)
    o_ref[0] = out.T


def _nsa_prompt_attention(q, kv_c, kv_s, kv_w, nsa_g):
    b, T, _ = q.shape
    G, J, D, TQ = KV_HEADS, HEADS_PER_KV, ATTN_HEAD_DIM, NSA_TQ
    bf16 = jnp.bfloat16

    def split_kv(kv):
        r = kv.reshape(b, T, G, 2, D).astype(bf16)
        return jnp.transpose(r[:, :, :, 0], (0, 2, 1, 3)), jnp.transpose(r[:, :, :, 1], (0, 2, 3, 1))

    ks, vst = split_kv(kv_s)
    kw, vwt = split_kv(kv_w)
    gates_t = jnp.transpose(nsa_g.reshape(b, T, G, J, 3), (0, 2, 4, 3, 1))
    n_blk = T // CMP_BLOCK
    member = (jnp.arange(T)[:, None] // SEL_BLOCK == jnp.arange(n_blk)[None, :]).astype(bf16)
    ks = jnp.concatenate([ks, jnp.broadcast_to(member, (b, G, T, n_blk))], axis=-1)
    kern = functools.partial(_nsa_prompt_kernel, seq=T)
    return pl.pallas_call(
        kern,
        grid=(b, G, T // TQ),
        in_specs=[
            pl.BlockSpec((1, TQ, J * D), lambda bi, g, i: (bi, i, g)),
            pl.BlockSpec((1, T, 2 * D), lambda bi, g, i: (bi, 0, g)),
            pl.BlockSpec((1, 1, T, D + n_blk), lambda bi, g, i: (bi, g, 0, 0)),
            pl.BlockSpec((1, 1, D, T), lambda bi, g, i: (bi, g, 0, 0)),
            pl.BlockSpec((1, 1, T, D), lambda bi, g, i: (bi, g, 0, 0)),
            pl.BlockSpec((1, 1, D, T), lambda bi, g, i: (bi, g, 0, 0)),
            pl.BlockSpec((1, 1, 3, J, TQ), lambda bi, g, i: (bi, g, 0, 0, i)),
        ],
        out_specs=pl.BlockSpec((1, TQ, J * D), lambda bi, g, i: (bi, i, g)),
        out_shape=jax.ShapeDtypeStruct((b, T, G * J * D), jnp.float32),
        scratch_shapes=[
            pltpu.VMEM((n_blk, D), bf16),
            pltpu.VMEM((D, n_blk), bf16),
            pltpu.VMEM((1, J * TQ), jnp.float32),
            pltpu.VMEM((1, J * TQ), jnp.float32),
            pltpu.VMEM((D, J * TQ), jnp.float32),
        ],
        compiler_params=pltpu.CompilerParams(
            dimension_semantics=("arbitrary", "arbitrary", "arbitrary"),
            vmem_limit_bytes=48 * 1024 * 1024),
        name="nsa_prompt",
    )(q, kv_c, ks, vst, kw, vwt, gates_t)


def _nsa_prompt(q, kv_c, kv_s, kv_w, nsa_g):
    b, T = q.shape[:2]
    win_new = kv_w.reshape(b, T, KV_HEADS, 2, ATTN_HEAD_DIM)[:, -min(WINDOW, T):]
    return _nsa_prompt_attention(q, kv_c, kv_s, kv_w, nsa_g), win_new


NSAS_PPS = 16


def _nsa_sample_kernel(pt_ref, qs_ref, gate_ref, *refs, n_pages, t_len, w_eff):
    f32, bf16 = jnp.float32, jnp.bfloat16
    PPS, G, D, J, T = NSAS_PPS, KV_HEADS, ATTN_HEAD_DIM, HEADS_PER_KV, t_len
    cmp_refs, sel_refs = refs[:PPS], refs[PPS:2 * PPS]
    tail_ref, win_ref, o_ref, pooled_ref, ocmp_ref, sel_ref, m_ref, l_ref, acc_ref = refs[2 * PPS:]
    C = J * T
    ph, pg = pl.program_id(1), pl.program_id(2)
    n_pg = n_pages // PPS
    past = n_pages * PAGE_SIZE
    n_blk = past // CMP_BLOCK
    n_sel_pad = sel_ref.shape[1]
    blk_per_page = PAGE_SIZE // CMP_BLOCK
    col_t = lax.broadcasted_iota(jnp.int32, (1, C), 1) % T
    qpos = past + col_t

    def queries(g):
        return (qs_ref[0, g] * ATTN_SCALE).astype(bf16)

    def flash(g, kv, ok):
        k, v = kv[:, :D], kv[:, D:]
        st = jnp.where(ok, jnp.dot(k.astype(bf16), queries(g), preferred_element_type=f32), NEG_BIG)
        m_old = m_ref[g]
        m_new = jnp.maximum(m_old, jnp.max(st, axis=0, keepdims=True))
        alpha = jnp.exp(m_old - m_new)
        p = jnp.exp(st - m_new)
        l_ref[g] = alpha * l_ref[g] + jnp.sum(p, axis=0, keepdims=True)
        acc_ref[g] = alpha * acc_ref[g] + jnp.dot(v.T.astype(bf16), p.astype(bf16), preferred_element_type=f32)
        m_ref[g] = m_new

    @pl.when(ph == 0)
    def _compressed():
        means = []
        for r in range(PPS):
            page = cmp_refs[r][0]
            means.append(jnp.sum(page.reshape(blk_per_page, CMP_BLOCK, G * 2 * D), axis=1) * (1.0 / CMP_BLOCK))
        rows = PPS * blk_per_page
        pooled_ref[pl.ds(pl.multiple_of(pg * rows, rows), rows), :] = jnp.concatenate(means, axis=0)

        @pl.when(pg == n_pg - 1)
        def _select():
            blk = lax.broadcasted_iota(jnp.int32, (n_blk, 1), 0)
            vis = (blk * CMP_BLOCK + (CMP_BLOCK - 1)) <= qpos
            fold = (lax.broadcasted_iota(jnp.int32, (C, T), 0) % T == lax.broadcasted_iota(jnp.int32, (C, T), 1))
            blk2 = lax.broadcasted_iota(jnp.int32, (n_sel_pad, T), 0)
            cur = (past + lax.broadcasted_iota(jnp.int32, (1, T), 1)) // SEL_BLOCK
            forced = ((blk2 == 0) | (blk2 == cur) | (blk2 == cur - 1)).astype(f32)
            for g in range(G):
                pooled = pooled_ref[:, g * 2 * D:(g + 1) * 2 * D]
                s = jnp.dot(pooled[:, :D].astype(bf16), queries(g), preferred_element_type=f32)
                s = jnp.where(vis, s, -jnp.inf)
                m = jnp.max(s, axis=0, keepdims=True)
                m = jnp.where(m == -jnp.inf, 0.0, m)
                p = jnp.exp(s - m)
                p = p / jnp.maximum(jnp.sum(p, axis=0, keepdims=True), 1e-30)
                ocmp_ref[g] = jnp.dot(pooled[:, D:].T.astype(bf16), p.astype(bf16), preferred_element_type=f32)
                imp = jnp.dot(p, fold.astype(f32), preferred_element_type=f32, precision=lax.Precision.HIGHEST)
                imp = jnp.concatenate([imp, jnp.zeros((n_sel_pad - n_blk, T), f32)], axis=0)
                score = jnp.where(blk2 <= cur, imp + SEL_BONUS * forced, -jnp.inf)
                sel = jnp.zeros((n_sel_pad, T), f32)
                for _ in range(SEL_TOPK):
                    mx = jnp.max(score, axis=0, keepdims=True)
                    first = jnp.min(jnp.where(score == mx, blk2, n_sel_pad), axis=0, keepdims=True)
                    pick = blk2 == first
                    sel = jnp.where(pick, 1.0, sel)
                    score = jnp.where(pick, -jnp.inf, score)
                sel_ref[g] = jnp.concatenate([sel] * J, axis=1)

    @pl.when(ph == 1)
    def _selected():
        @pl.when(pg == 0)
        def _init():
            m_ref[...] = jnp.full(m_ref.shape, NEG_BIG, f32)
            l_ref[...] = jnp.zeros(l_ref.shape, f32)
            acc_ref[...] = jnp.zeros(acc_ref.shape, f32)

        key = lax.broadcasted_iota(jnp.int32, (PAGE_SIZE, 1), 0)
        for r in range(PPS):
            page = sel_refs[r][0]
            first_blk = (pg * PPS + r) * blk_per_page
            for g in range(G):
                chosen = jnp.where(key < CMP_BLOCK, sel_ref[g, pl.ds(first_blk, 1), :], sel_ref[g, pl.ds(first_blk + 1, 1), :])
                flash(g, page[:, g * 2 * D:(g + 1) * 2 * D], chosen > 0.5)

        @pl.when(pg == n_pg - 1)
        def _finish():
            tail = tail_ref[0]
            wkeys = win_ref[0]
            span = wkeys.shape[0]
            wi = lax.broadcasted_iota(jnp.int32, (span, 1), 0)
            rel = qpos - (past - w_eff + wi)
            wok = (rel >= 0) & (rel <= WINDOW) & (wi < w_eff + T)
            for g in range(G):
                ok = (sel_ref[g, n_blk:n_blk + 1, :] > 0.5) & (key <= col_t) & (key < T)
                flash(g, tail[:, g * 2 * D:(g + 1) * 2 * D], ok)
                o_sel = acc_ref[g] / l_ref[g]
                kvw = wkeys[:, g * 2 * D:(g + 1) * 2 * D]
                sw = jnp.where(wok, jnp.dot(kvw[:, :D].astype(bf16), queries(g), preferred_element_type=f32), NEG_BIG)
                pw = jnp.exp(sw - jnp.max(sw, axis=0, keepdims=True))
                o_win = jnp.dot(kvw[:, D:].T.astype(bf16), pw.astype(bf16), preferred_element_type=f32)
                o_win = o_win / jnp.sum(pw, axis=0, keepdims=True)
                gates = jax.nn.sigmoid(gate_ref[0, g])
                o_ref[0, g] = gates[0:1] * ocmp_ref[g] + gates[1:2] * o_sel + gates[2:3] * o_win


def _nsa_sample(pool_cmp, pool_sel, win_buf, page_table, q, kv_cmp, kv_sel, kv_win, nsa_g):
    b, T, _ = q.shape
    assert T < CMP_BLOCK and page_table.shape[1] % NSAS_PPS == 0
    G, J, D, PPS = KV_HEADS, HEADS_PER_KV, ATTN_HEAD_DIM, NSAS_PPS
    n_pages = page_table.shape[1]
    n_pg = n_pages // PPS
    past = n_pages * PAGE_SIZE
    n_blk = past // CMP_BLOCK
    n_sel_pad = -(-(n_blk + 1) // 8) * 8
    w_eff = win_buf.shape[1]
    C = J * T
    n_pool = pool_cmp.shape[0]
    pool_cmp = pool_cmp.reshape(n_pool, PAGE_SIZE, KV_WIDTH)
    pool_sel = pool_sel.reshape(n_pool, PAGE_SIZE, KV_WIDTH)
    qs = jnp.transpose(q.reshape(b, T, G, J, D), (0, 2, 4, 3, 1)).reshape(b, G, D, C)
    gates = jnp.transpose(nsa_g.reshape(b, T, G, J, 3), (0, 2, 4, 3, 1)).reshape(b, G, 3, C)
    tail = jnp.pad(kv_sel, ((0, 0), (0, PAGE_SIZE - T), (0, 0)))
    keys = jnp.concatenate([win_buf.reshape(b, w_eff, KV_WIDTH), kv_win], axis=1)
    span = -(-(w_eff + T) // PAGE_SIZE) * PAGE_SIZE
    wkeys = jnp.pad(keys, ((0, 0), (0, span - (w_eff + T)), (0, 0)))

    def cmp_map(r):
        return lambda bi, ph, pg, pt: (pt[bi, jnp.where(ph == 0, pg, n_pg - 1) * PPS + r], 0, 0)

    def sel_map(r):
        return lambda bi, ph, pg, pt: (pt[bi, jnp.where(ph == 1, pg, 0) * PPS + r], 0, 0)

    page_block = (1, PAGE_SIZE, KV_WIDTH)
    per_b = lambda shape: pl.BlockSpec(shape, lambda bi, ph, pg, pt: (bi,) + (0,) * (len(shape) - 1))
    kern = functools.partial(_nsa_sample_kernel, n_pages=n_pages, t_len=T, w_eff=w_eff)
    out = pl.pallas_call(
        kern,
        grid_spec=pltpu.PrefetchScalarGridSpec(
            num_scalar_prefetch=1,
            grid=(b, 2, n_pg),
            in_specs=[per_b((1, G, D, C)), per_b((1, G, 3, C))]
                     + [pl.BlockSpec(page_block, cmp_map(r)) for r in range(PPS)]
                     + [pl.BlockSpec(page_block, sel_map(r)) for r in range(PPS)]
                     + [per_b((1, PAGE_SIZE, KV_WIDTH)), per_b((1, span, KV_WIDTH))],
            out_specs=per_b((1, G, D, C)),
            scratch_shapes=[pltpu.VMEM((n_blk, KV_WIDTH), jnp.float32),
                            pltpu.VMEM((G, D, C), jnp.float32),
                            pltpu.VMEM((G, n_sel_pad, C), jnp.float32),
                            pltpu.VMEM((G, 1, C), jnp.float32),
                            pltpu.VMEM((G, 1, C), jnp.float32),
                            pltpu.VMEM((G, D, C), jnp.float32)]),
        out_shape=jax.ShapeDtypeStruct((b, G, D, C), jnp.float32),
        compiler_params=pltpu.CompilerParams(dimension_semantics=("arbitrary", "arbitrary", "arbitrary"),
                                             vmem_limit_bytes=32 * 1024 * 1024),
        name="nsa_sample",
    )(page_table, qs, gates, *([pool_cmp] * PPS), *([pool_sel] * PPS), tail, wkeys)
    o_attn = jnp.transpose(out.reshape(b, G, D, J, T), (0, 4, 1, 3, 2)).reshape(b, T, ATTN_WIDTH)
    return o_attn, keys[:, -w_eff:].reshape(b, w_eff, KV_HEADS, 2, D)


PEER_PICKS = PEER_HEADS * PEER_TOPK
PEER_TT = 128
PEER_GT = 128
ROW_SUBLANES = D_MODEL // 128


def _top16_cols(s, n_rows, iota_rows):
    vals, rows = [], []
    for _ in range(PEER_TOPK):
        mx = jnp.max(s, axis=0, keepdims=True)
        first = jnp.min(jnp.where(s == mx, iota_rows, n_rows), axis=0, keepdims=True)
        vals.append(mx)
        rows.append(first)
        s = jnp.where(iota_rows == first, -jnp.inf, s)
    return jnp.concatenate(vals, axis=0), jnp.concatenate(rows, axis=0)


def _peer_route_kernel(h_ref, wq_ref, keys_ref, exp_ref, gate_ref):
    f32 = jnp.float32
    TT = PEER_TT
    q = jnp.dot(h_ref[...].astype(jnp.bfloat16), wq_ref[...], preferred_element_type=f32)
    iota_k = lax.broadcasted_iota(jnp.int32, (PEER_KEYS, TT), 0)
    widths = [PEER_TOPK // (a + 1) for a in range(PEER_TOPK)]
    n_cand = -(-sum(widths) // 8) * 8
    sub = lax.broadcasted_iota(jnp.int32, (n_cand, 1), 0)
    iota_c = jnp.full((n_cand, 1), PEER_TOPK * PEER_TOPK, jnp.int32)
    row = 0
    for a in range(PEER_TOPK):
        iota_c = jnp.where((sub >= row) & (sub < row + widths[a]), a * PEER_TOPK + (sub - row), iota_c)
        row += widths[a]
    iota_c = jnp.broadcast_to(iota_c, (n_cand, TT))
    exp_rows, gate_rows = [], []
    for hd in range(PEER_HEADS):
        tops = []
        for c in range(2):
            j = hd * 2 + c
            qj = q[:, j * PEER_HALF:(j + 1) * PEER_HALF].astype(jnp.bfloat16)
            st = lax.dot_general(keys_ref[j], qj, (((1,), (1,)), ((), ())), preferred_element_type=f32)
            tops.append(_top16_cols(st, PEER_KEYS, iota_k))
        (s0, i0), (s1, i1) = tops
        fill = n_cand - sum(widths)
        cand_s = jnp.concatenate([s0[a:a + 1, :] + s1[:widths[a], :] for a in range(PEER_TOPK)]
                                 + [jnp.full((fill, TT), -jnp.inf, f32)], axis=0)
        cand_i = jnp.concatenate([i0[a:a + 1, :] * PEER_KEYS + i1[:widths[a], :] for a in range(PEER_TOPK)]
                                 + [jnp.zeros((fill, TT), jnp.int32)], axis=0)
        best_s, best_e = [], []
        s = cand_s
        for _ in range(PEER_TOPK):
            mx = jnp.max(s, axis=0, keepdims=True)
            first = jnp.min(jnp.where(s == mx, iota_c, PEER_TOPK * PEER_TOPK), axis=0, keepdims=True)
            pick = iota_c == first
            best_s.append(mx)
            best_e.append(jnp.sum(jnp.where(pick, cand_i, 0), axis=0, keepdims=True))
            s = jnp.where(pick, -jnp.inf, s)
        bs = jnp.concatenate(best_s, axis=0)
        be = jnp.concatenate(best_e, axis=0)
        ex = jnp.exp(bs - bs[0:1, :])
        gate_rows.append(ex / jnp.sum(ex, axis=0, keepdims=True))
        exp_rows.append(be)
    exp_ref[...] = _row_code(jnp.concatenate(exp_rows, axis=0)).T
    gate_ref[...] = jnp.concatenate(gate_rows, axis=0).T


def _peer_route(h, wq_b, keys_b):
    n = h.shape[0]
    TT = PEER_TT
    return pl.pallas_call(
        _peer_route_kernel,
        grid=(n // TT,),
        in_specs=[pl.BlockSpec((TT, D_MODEL), lambda i: (i, 0)),
                  pl.BlockSpec((D_MODEL, PEER_HEADS * PEER_QDIM), lambda i: (0, 0)),
                  pl.BlockSpec((PEER_HEADS * 2, PEER_KEYS, PEER_HALF), lambda i: (0, 0, 0))],
        out_specs=[pl.BlockSpec((TT, PEER_PICKS), lambda i: (i, 0)),
                   pl.BlockSpec((TT, PEER_PICKS), lambda i: (i, 0))],
        out_shape=[jax.ShapeDtypeStruct((n, PEER_PICKS), jnp.int32),
                   jax.ShapeDtypeStruct((n, PEER_PICKS), jnp.float32)],
        compiler_params=pltpu.CompilerParams(dimension_semantics=("arbitrary",),
                                             vmem_limit_bytes=40 * 1024 * 1024),
        name="peer_route",
    )(h, wq_b, keys_b)


def _pack_rows(w):
    b = lax.bitcast_convert_type(w.astype(jnp.bfloat16), jnp.uint16).astype(jnp.uint32)
    b = b.reshape(N_EXPERTS // 2, 2, D_MODEL)
    return ((b[:, 0] << 16) | b[:, 1]).reshape(N_EXPERTS // 2 * ROW_SUBLANES, 128)


def _row_code(e):
    return (e >> 1) * ROW_SUBLANES + (e & 1)


PARITY_SHIFT_BIT = 4


def _load_row(tab_ref, start, shift_word):
    w = tab_ref[pl.ds(pl.multiple_of(start, ROW_SUBLANES), ROW_SUBLANES), :]
    w = (w << (shift_word & jnp.uint32(1 << PARITY_SHIFT_BIT))) & jnp.uint32(0xFFFF0000)
    return lax.bitcast_convert_type(w, jnp.float32)


def _group_shift_words(codes):
    par = (codes & 1).reshape(-1, PEER_GROUP)
    return jnp.sum(par << (PARITY_SHIFT_BIT + jnp.arange(PEER_GROUP, dtype=jnp.int32)), axis=1, dtype=jnp.int32)


def _splat_u32(s):
    return lax.bitcast_convert_type(jnp.full((ROW_SUBLANES, 128), s, jnp.int32), jnp.uint32)


def _sublane_sums(tiles):
    sub = lax.broadcasted_iota(jnp.int32, (ROW_SUBLANES, 128), 0)
    step = ROW_SUBLANES // 2
    while step >= 1:
        low = (sub % (2 * step)) < step
        tiles = [jnp.where(low, a + pltpu.roll(a, ROW_SUBLANES - step, 0), pltpu.roll(b, step, 0) + b)
                 for a, b in zip(tiles[:len(tiles) // 2], tiles[len(tiles) // 2:])]
        step //= 2
    return tiles[0]


PEER_GROUP = 16


PEER_GROUPS_PER_TOKEN = PEER_PICKS // PEER_GROUP
PEER_LOOP_GROUPS = 4


def _pipelined_groups(n_groups, produce, consume, stage_a, stage_b):
    def put(stage, tiles):
        for k, tile in enumerate(tiles):
            stage[k] = tile

    def get(stage):
        return [stage[k] for k in range(PEER_GROUP)]

    stages = (stage_a, stage_b)

    def step(i, parity):
        put(stages[parity], produce(i))
        consume(i - 1, get(stages[1 - parity]))

    def body(j, carry):
        first = j * PEER_LOOP_GROUPS + 1
        for u in range(PEER_LOOP_GROUPS):
            step(first + u, (1 + u) % 2)
        return carry

    put(stage_a, produce(0))
    trips = (n_groups - 1) // PEER_LOOP_GROUPS
    lax.fori_loop(0, trips, body, 0)
    for i in range(trips * PEER_LOOP_GROUPS + 1, n_groups):
        step(i, i % 2)
    consume(n_groups - 1, get(stages[(n_groups - 1) % 2]))


def _tree_sum(tiles):
    while len(tiles) > 1:
        tiles = [a + b for a, b in zip(tiles[0::2], tiles[1::2])]
    return tiles[0]


def _peer_up_kernel(start_ref, shift_ref, h_ref, gate_ref, tab_ref, coef_ref, dots_ref, stage_a, stage_b):
    GT = PEER_GT

    def products(i):
        t = i // PEER_GROUPS_PER_TOKEN
        h = h_ref[pl.ds(pl.multiple_of(t * ROW_SUBLANES, ROW_SUBLANES), ROW_SUBLANES), :]
        shifts = _splat_u32(shift_ref[i])
        return tuple(_load_row(tab_ref, start_ref[i * PEER_GROUP + kk], shifts >> jnp.uint32(kk)) * h
                     for kk in range(PEER_GROUP))

    def reduce_rows(i, prods):
        rows = [_sublane_sums(list(prods[k:k + ROW_SUBLANES])) for k in range(0, PEER_GROUP, ROW_SUBLANES)]
        dots_ref[pl.ds(pl.multiple_of(i * PEER_GROUP, PEER_GROUP), PEER_GROUP), :] = jnp.concatenate(rows, axis=0)

    _pipelined_groups(GT * PEER_GROUPS_PER_TOKEN, products, reduce_rows, stage_a, stage_b)
    rows = [jnp.sum(dots_ref[t * PEER_PICKS:(t + 1) * PEER_PICKS, :].T, axis=0, keepdims=True) for t in range(GT)]
    coef_ref[...] = gate_ref[...] * jax.nn.gelu(jnp.concatenate(rows, axis=0))


def _peer_down_kernel(start_ref, shift_ref, coef_ref, tab_ref, o_ref, part_ref, stage_a, stage_b):
    GT = PEER_GT

    def products(i):
        shifts = _splat_u32(shift_ref[i])
        return tuple(_load_row(tab_ref, start_ref[i * PEER_GROUP + kk], shifts >> jnp.uint32(kk))
                     * jnp.full((ROW_SUBLANES, 128), coef_ref[i * PEER_GROUP + kk], jnp.float32)
                     for kk in range(PEER_GROUP))

    def partial_sum(i, prods):
        part_ref[pl.ds(pl.multiple_of(i * ROW_SUBLANES, ROW_SUBLANES), ROW_SUBLANES), :] = _tree_sum(list(prods))

    _pipelined_groups(GT * PEER_GROUPS_PER_TOKEN, products, partial_sum, stage_a, stage_b)
    parts = part_ref[...].reshape(GT, PEER_GROUPS_PER_TOKEN, ROW_SUBLANES, 128)
    o_ref[...] = jnp.sum(parts, axis=1).reshape(GT * ROW_SUBLANES, 128)


def _peer_gather(h, experts, gates, u_tab, v_tab):
    n = h.shape[0]
    GT = PEER_GT
    tab_rows = N_EXPERTS // 2 * ROW_SUBLANES
    flat_smem = pl.BlockSpec((GT * PEER_PICKS,), lambda i: (i,), memory_space=pltpu.MemorySpace.SMEM)
    tab_spec = pl.BlockSpec((tab_rows, 128), lambda i: (0, 0), pipeline_mode=pl.Buffered(1))
    params = pltpu.CompilerParams(dimension_semantics=("arbitrary",), vmem_limit_bytes=56 * 1024 * 1024)
    group_smem = pl.BlockSpec((GT * PEER_GROUPS_PER_TOKEN,), lambda i: (i,), memory_space=pltpu.MemorySpace.SMEM)
    starts = (experts & ~(ROW_SUBLANES - 1)).reshape(n * PEER_PICKS)
    shifts = _group_shift_words(experts)
    stage = pltpu.VMEM((PEER_GROUP, ROW_SUBLANES, 128), jnp.float32)
    coef = pl.pallas_call(
        _peer_up_kernel,
        grid=(n // GT,),
        in_specs=[flat_smem, group_smem,
                  pl.BlockSpec((GT * ROW_SUBLANES, 128), lambda i: (i, 0)),
                  pl.BlockSpec((GT, PEER_PICKS), lambda i: (i, 0)),
                  tab_spec],
        out_specs=pl.BlockSpec((GT, PEER_PICKS), lambda i: (i, 0)),
        out_shape=jax.ShapeDtypeStruct((n, PEER_PICKS), jnp.float32),
        scratch_shapes=[pltpu.VMEM((GT * PEER_PICKS, 128), jnp.float32), stage, stage],
        compiler_params=params,
        name="peer_up",
    )(starts, shifts, h.reshape(n * ROW_SUBLANES, 128), gates, u_tab)
    out = pl.pallas_call(
        _peer_down_kernel,
        grid=(n // GT,),
        in_specs=[flat_smem, group_smem, flat_smem, tab_spec],
        out_specs=pl.BlockSpec((GT * ROW_SUBLANES, 128), lambda i: (i, 0)),
        out_shape=jax.ShapeDtypeStruct((n * ROW_SUBLANES, 128), jnp.float32),
        scratch_shapes=[pltpu.VMEM((GT * PEER_GROUPS_PER_TOKEN * ROW_SUBLANES, 128), jnp.float32), stage, stage],
        compiler_params=params,
        name="peer_down",
    )(starts, shifts, coef.reshape(n * PEER_PICKS), v_tab)
    return out.reshape(n, D_MODEL)


def _layer_weights(w_in, w_ssm_branch, w_attn_branch, w_out, peer_wq, peer_sub_keys, peer_u, peer_v):
    bf16 = jnp.bfloat16
    offs = np.cumsum((0,) + PROJ_SIZES).tolist()
    seg = lambda k: w_in[:, offs[k]:offs[k + 1]]
    pad = jnp.zeros((D_MODEL, SMALL_WIDTH - SSM_HEADS - 3 * ATTN_HEADS), w_in.dtype)
    return dict(
        in_ssm=[seg(0).astype(bf16), seg(1).astype(bf16)],
        in_attn=[seg(3).astype(bf16), jnp.concatenate([seg(4), seg(5), seg(6)], axis=1).astype(bf16),
                 seg(8).astype(bf16), jnp.concatenate([seg(2), seg(7), pad], axis=1).astype(bf16)],
        w_ssm=w_ssm_branch.astype(bf16), w_attn=w_attn_branch.astype(bf16), w_out=w_out.astype(bf16),
        wq=peer_wq.astype(bf16),
        keys=peer_sub_keys.reshape(PEER_HEADS * 2, PEER_KEYS, PEER_HALF).astype(bf16),
        u_tab=_pack_rows(peer_u), v_tab=_pack_rows(peer_v))


def _block(x, conv_prev, ssm_prev, nsa_core, lw, norm_mix_w, conv_w, conv_b, dt_bias, a_log, d_skip, ssm_norm_w,
           norm_ffn_w):
    b, T, _ = x.shape
    n = b * T
    xf = x.reshape(n, D_MODEL)
    z, xbc = _norm_proj(xf, norm_mix_w, lw["in_ssm"], "in_proj_ssm")
    q, kv, br_g, small = _norm_proj(xf, norm_mix_w, lw["in_attn"], "in_proj_attn")
    nsa_g = small[:, SSM_HEADS:SSM_HEADS + 3 * ATTN_HEADS]
    kv_c, kv_s, kv_w = (kv[:, k * KV_WIDTH:(k + 1) * KV_WIDTH].reshape(b, T, KV_WIDTH) for k in range(3))
    y_ssm, conv_new, ssm_new = _mamba_branch(z.reshape(b, T, D_INNER), xbc.reshape(b, T, CONV_DIM),
                                             small.reshape(b, T, SMALL_WIDTH), conv_prev, ssm_prev, conv_w, conv_b,
                                             dt_bias, a_log, d_skip, ssm_norm_w)
    o_attn, win_new = nsa_core(q.reshape(b, T, ATTN_WIDTH), kv_c, kv_s, kv_w, nsa_g.reshape(b, T, 3 * ATTN_HEADS))
    x2, h2 = _merge_proj(xf, o_attn.reshape(n, ATTN_WIDTH), y_ssm.reshape(n, D_INNER), br_g, norm_ffn_w,
                         lw["w_attn"], lw["w_ssm"], lw["w_out"])
    experts, gates = _peer_route(h2, lw["wq"], lw["keys"])
    peer = _peer_gather(h2, experts, gates, lw["u_tab"], lw["v_tab"])
    kv_shape = (b, T, KV_HEADS, 2, ATTN_HEAD_DIM)
    return x2, peer, kv_c.reshape(kv_shape), kv_s.reshape(kv_shape), win_new, conv_new, ssm_new


def kernel(x_prompt, x_sample, cache_cmp_kv, cache_sel_kv, cache_win_kv, state_ssm, state_conv, page_table,
           norm_mix_w, w_in, conv_w, conv_b, dt_bias, a_log, d_skip, ssm_norm_w, w_ssm_branch, w_attn_branch,
           w_out, norm_ffn_w, peer_wq, peer_sub_keys, peer_u, peer_v, final_norm_w):
    xp, xs = x_prompt, x_sample
    cmp_p, cmp_s, sel_p, sel_s, win_p, win_s = [], [], [], [], [], []
    ssm_p, ssm_s, conv_p, conv_s = [], [], [], []
    for layer in range(DEPTH):
        lw = _layer_weights(*(a[layer] for a in (w_in, w_ssm_branch, w_attn_branch, w_out, peer_wq, peer_sub_keys,
                                                   peer_u, peer_v)))
        rest = [a[layer] for a in (norm_mix_w, conv_w, conv_b, dt_bias, a_log, d_skip, ssm_norm_w, norm_ffn_w)]
        last = layer == DEPTH - 1
        bp = xp.shape[0]
        conv0 = jnp.zeros((bp, CONV_WIDTH - 1, CONV_DIM), xp.dtype)
        ssm0 = jnp.zeros((bp, SSM_HEADS, SSM_HEAD_DIM, SSM_STATE), xp.dtype)
        x2, peer, kc, ks, kw, cv, ss = _block(xp, conv0, ssm0, _nsa_prompt, lw, *rest)
        xp = (_add_norm(x2, peer, final_norm_w) if last else x2 + peer).reshape(xp.shape)
        cmp_p.append(kc); sel_p.append(ks); win_p.append(kw); conv_p.append(cv); ssm_p.append(ss)
        core = functools.partial(_nsa_sample, cache_cmp_kv[layer], cache_sel_kv[layer], cache_win_kv[layer], page_table)
        x2, peer, kc, ks, kw, cv, ss = _block(xs, state_conv[layer], state_ssm[layer], core, lw, *rest)
        xs = (_add_norm(x2, peer, final_norm_w) if last else x2 + peer).reshape(xs.shape)
        cmp_s.append(kc); sel_s.append(ks); win_s.append(kw); conv_s.append(cv); ssm_s.append(ss)
    return (xp, xs, jnp.stack(cmp_p), jnp.stack(cmp_s), jnp.stack(sel_p), jnp.stack(sel_s),
            jnp.stack(win_p), jnp.stack(win_s), jnp.stack(ssm_p), jnp.stack(ssm_s), jnp.stack(conv_p), jnp.stack(conv_s))
```
